```python
import math
import jax, jax.numpy as jnp
from jax import lax
import numpy as np

D_MODEL = 1024
BATCH = 16
SEQ = 256
DEPTH = 4
DEC_BATCH = 8
DEC_SEQ = 4096
PAST_LEN = 512

GRID_W = 64
HEAD_DIM = 64
ATT_Q_HEADS = 8
ATT_KV_HEADS = 2
Q_BLOCK = 128
ROPE_BASE = 10000.0
ROPE_PAIRS = HEAD_DIM // 4
DN_HEADS = 4
DN_DK = 64
DN_DV = 64
DN_CHUNK = 64
LRU_WIDTH = 256
LRU_BLOCKS = 4
LRU_BLOCK_W = LRU_WIDTH // LRU_BLOCKS
LRU_C = 8.0
CONV_W = 4
CONV_PAD = (2, 1)
D_FF = 2816
N_BRANCH = 3
N_MOD = 9
EPS = 1e-6
IN_SIZES = (DN_HEADS * DN_DK, DN_HEADS * DN_DK, DN_HEADS * DN_DV, DN_HEADS * DN_DV, 2 * DN_HEADS, 2 * DN_HEADS,
            ATT_Q_HEADS * HEAD_DIM, ATT_KV_HEADS * HEAD_DIM, ATT_KV_HEADS * HEAD_DIM, LRU_WIDTH, LRU_WIDTH,
            N_BRANCH * D_MODEL)
N_IN = sum(IN_SIZES)

kernel_name = 'hybrid_diffusion_prefix_trunk_step'


def rmsnorm(x, w):
    xf = x.astype(jnp.float32)
    y = xf * lax.rsqrt(jnp.mean(xf * xf, axis=-1, keepdims=True) + EPS)
    return (y * w.astype(jnp.float32)).astype(x.dtype)


def l2norm(x):
    return x * lax.rsqrt(jnp.sum(x * x, axis=-1, keepdims=True) + EPS)


def swiglu(h, w_gu, w_down):
    g, u = jnp.split(h @ w_gu, 2, axis=-1)
    return (jax.nn.silu(g) * u) @ w_down


def depthwise_conv(x, w):
    C = x.shape[-1]
    return lax.conv_general_dilated(x, w[:, None, :].astype(x.dtype), window_strides=(1,), padding=[CONV_PAD],
                                    dimension_numbers=('NWC', 'WIO', 'NWC'), feature_group_count=C)


def axial_rope(n_tokens):
    rows = n_tokens // GRID_W
    row = jnp.broadcast_to(jnp.arange(rows, dtype=jnp.float32)[:, None], (rows, GRID_W)).reshape(-1)
    col = jnp.broadcast_to(jnp.arange(GRID_W, dtype=jnp.float32)[None, :], (rows, GRID_W)).reshape(-1)
    freqs = ROPE_BASE ** (-jnp.arange(ROPE_PAIRS, dtype=jnp.float32) / ROPE_PAIRS)
    ang = jnp.stack([row[:, None] * freqs, col[:, None] * freqs], axis=1)
    return jnp.cos(ang), jnp.sin(ang)


def apply_axial_rope(x, cos, sin):
    B, T, H, hd = x.shape
    xs = x.astype(jnp.float32).reshape(B, T, H, 2, 2, ROPE_PAIRS)
    x1, x2 = xs[..., 0, :], xs[..., 1, :]
    c = cos[None, :, None]
    s = sin[None, :, None]
    out = jnp.stack([x1 * c - x2 * s, x2 * c + x1 * s], axis=-2)
    return out.reshape(B, T, H, hd).astype(x.dtype)


def blocked_attention(q, k, v):
    B, T, Hq, hd = q.shape
    Hkv = k.shape[2]
    G = Hq // Hkv
    nb = T // Q_BLOCK
    qb = jnp.moveaxis(q.reshape(B, nb, Q_BLOCK, Hkv, G, hd), 1, 0)
    scale = hd ** -0.5

    def one_block(qblk):
        s = jnp.einsum('bqkgd,bskd->bkgqs', qblk, k, preferred_element_type=jnp.float32) * scale
        p = jax.nn.softmax(s, axis=-1).astype(v.dtype)
        return jnp.einsum('bkgqs,bskd->bqkgd', p, v)

    o = lax.map(one_block, qb)
    return jnp.moveaxis(o, 0, 1).reshape(B, T, Hq * hd)


def gated_delta_chunked(q, k, v, beta, g, s0):
    B, T, H, dk = q.shape
    dv = v.shape[-1]
    C = DN_CHUNK
    n = T // C
    to_c = lambda a: jnp.moveaxis(a.reshape(B, n, C, H, -1), 3, 2)
    qc, kc, vc = to_c(q), to_c(k), to_c(v)
    bc = to_c(beta[..., None])[..., 0]
    gc = jnp.cumsum(to_c(g[..., None])[..., 0], axis=-1)
    idx = jnp.arange(C)
    incl = idx[:, None] >= idx[None, :]
    strict = idx[:, None] > idx[None, :]
    diff = gc[..., :, None] - gc[..., None, :]
    dec_incl = jnp.exp(jnp.where(incl, diff, -jnp.inf))
    dec_strict = jnp.where(strict, dec_incl, 0.0)
    kk = jnp.einsum('bnhcd,bnhed->bnhce', kc, kc)
    a_mat = jnp.eye(C, dtype=jnp.float32) + bc[..., :, None] * kk * dec_strict
    eg = jnp.exp(gc)
    rhs = jnp.concatenate([bc[..., None] * vc, (bc * eg)[..., None] * kc], axis=-1)
    sol = lax.linalg.triangular_solve(a_mat, rhs, left_side=True, lower=True, unit_diagonal=True)
    u, wk = sol[..., :dv], sol[..., dv:]
    qk = jnp.einsum('bnhcd,bnhed->bnhce', qc, kc) * dec_incl
    q_dec = qc * eg[..., None]
    k_dec = kc * jnp.exp(gc[..., -1:] - gc)[..., None]
    g_tot = eg[..., -1]
    xs = tuple(jnp.moveaxis(a, 1, 0) for a in (u, wk, qk, q_dec, k_dec, g_tot))

    def step(S, inp):
        u_c, wk_c, qk_c, qd_c, kd_c, gt_c = inp
        w = u_c - jnp.einsum('bhck,bhkv->bhcv', wk_c, S)
        o = jnp.einsum('bhck,bhkv->bhcv', qd_c, S) + jnp.einsum('bhcs,bhsv->bhcv', qk_c, w)
        S = gt_c[..., None, None] * S + jnp.einsum('bhck,bhcv->bhkv', kd_c, w)
        return S, o

    s_fin, o = lax.scan(step, s0.astype(jnp.float32), xs)
    o = jnp.moveaxis(jnp.moveaxis(o, 0, 1), 2, 3).reshape(B, T, H, dv)
    return o, s_fin


def deltanet_branch(q_in, k_in, v_in, z, b_logit, a_logit, conv_w, a_log, dt_bias, onorm_w, s0):
    B, T, _ = q_in.shape
    qkv = jax.nn.silu(depthwise_conv(jnp.concatenate([q_in, k_in, v_in], axis=-1), conv_w)).astype(jnp.float32)
    q, k, v = jnp.split(qkv, [DN_HEADS * DN_DK, 2 * DN_HEADS * DN_DK], axis=-1)
    q = l2norm(q.reshape(B, T, DN_HEADS, DN_DK)) * (DN_DK ** -0.5)
    k = l2norm(k.reshape(B, T, DN_HEADS, DN_DK))
    v = v.reshape(B, T, DN_HEADS, DN_DV)
    beta = jax.nn.sigmoid(b_logit.astype(jnp.float32)).reshape(B, T, 2, DN_HEADS)
    g = -jnp.exp(a_log.astype(jnp.float32)) * jax.nn.softplus(
        a_logit.astype(jnp.float32).reshape(B, T, 2, DN_HEADS) + dt_bias.astype(jnp.float32))
    rev = lambda a: jnp.flip(a, axis=1)
    o_f, s_f = gated_delta_chunked(q, k, v, beta[:, :, 0], g[:, :, 0], s0[:, 0])
    o_b, s_b = gated_delta_chunked(rev(q), rev(k), rev(v), rev(beta[:, :, 1]), rev(g[:, :, 1]), s0[:, 1])
    o = o_f + rev(o_b)
    o = rmsnorm(o, onorm_w) * jax.nn.silu(z.astype(jnp.float32).reshape(B, T, DN_HEADS, DN_DV))
    return o.reshape(B, T, DN_HEADS * DN_DV).astype(q_in.dtype), jnp.stack([s_f, s_b], axis=1)


def linear_scan(a, u, h0):
    def comb(l, r):
        return (l[0] * r[0], r[0] * l[1] + r[1])
    a_cum, b_cum = lax.associative_scan(comb, (a, u), axis=1)
    h = a_cum * h0[:, None, :] + b_cum
    return h, h[:, -1]


def rglru_branch(x_in, y_in, conv_w, conv_b, w_r, b_r, w_i, b_i, lam, h0):
    B, T, W = x_in.shape
    x = (depthwise_conv(x_in, conv_w) + conv_b.astype(x_in.dtype)).astype(jnp.float32)
    xb = x.reshape(B, T, LRU_BLOCKS, LRU_BLOCK_W)
    h0 = h0.astype(jnp.float32)
    outs, finals = [], []
    for d in range(2):
        r = jax.nn.sigmoid(jnp.einsum('btnc,ncd->btnd', xb, w_r[d].astype(jnp.float32)).reshape(B, T, W)
                           + b_r[d].astype(jnp.float32))
        i = jax.nn.sigmoid(jnp.einsum('btnc,ncd->btnd', xb, w_i[d].astype(jnp.float32)).reshape(B, T, W)
                           + b_i[d].astype(jnp.float32))
        log_a = -LRU_C * r * jax.nn.softplus(-lam[d].astype(jnp.float32))
        a = jnp.exp(log_a)
        u = jnp.sqrt(-jnp.expm1(2.0 * log_a)) * (i * x)
        if d == 1:
            a, u = jnp.flip(a, axis=1), jnp.flip(u, axis=1)
        h, h_last = linear_scan(a, u, h0[:, d])
        if d == 1:
            h = jnp.flip(h, axis=1)
        outs.append(h)
        finals.append(h_last)
    out = jax.nn.gelu(y_in.astype(jnp.float32)) * (outs[0] + outs[1])
    return out.astype(x_in.dtype), jnp.stack(finals, axis=1)


def token_mix(h, lp, rope, ctx):
    B, T, _ = h.shape
    split_at = [int(s) for s in np.cumsum(IN_SIZES)[:-1]]
    dq, dk, dv, dz, db, da, aq, ak, av, lx, ly, gl = jnp.split(h @ lp['w_in'], split_at, axis=-1)
    if ctx is None:
        s_dn0 = jnp.zeros((B, 2, DN_HEADS, DN_DK, DN_DV), jnp.float32)
        h_lru0 = jnp.zeros((B, 2, LRU_WIDTH), jnp.float32)
    else:
        k_ctx, v_ctx, s_dn0, h_lru0 = ctx
    o_dn, s_dn = deltanet_branch(dq, dk, dv, dz, db, da, lp['dn_conv_w'], lp['dn_a_log'], lp['dn_dt_bias'],
                                 lp['dn_onorm_w'], s_dn0)
    q = rmsnorm(aq.reshape(B, T, ATT_Q_HEADS, HEAD_DIM), lp['att_qnorm_w'])
    k = rmsnorm(ak.reshape(B, T, ATT_KV_HEADS, HEAD_DIM), lp['att_knorm_w'])
    v = av.reshape(B, T, ATT_KV_HEADS, HEAD_DIM)
    if ctx is None:
        o_att = blocked_attention(q, k, v)
    else:
        cos, sin = rope
        o_att = blocked_attention(apply_axial_rope(q, cos, sin),
                                  jnp.concatenate([k_ctx.astype(k.dtype), apply_axial_rope(k, cos, sin)], axis=1),
                                  jnp.concatenate([v_ctx.astype(v.dtype), v], axis=1))
    o_lru, h_lru = rglru_branch(lx, ly, lp['lru_conv_w'], lp['lru_conv_b'], lp['lru_wr'], lp['lru_br'],
                                lp['lru_wi'], lp['lru_bi'], lp['lru_lam'], h_lru0)
    gates = jax.nn.sigmoid(gl.astype(jnp.float32)).astype(h.dtype).reshape(B, T, N_BRANCH, D_MODEL)
    merged = (gates[:, :, 0] * (o_dn @ lp['w_pa'])
              + gates[:, :, 1] * (o_att @ lp['w_pb'])
              + gates[:, :, 2] * (o_lru @ lp['w_pc']))
    out = merged @ lp['w_o']
    new_ctx = (k, v, s_dn, h_lru) if ctx is None else None
    return out, new_ctx


def run_layer(x, cond, lp, rope, ctx):
    mod = (jax.nn.silu(cond) @ lp['w_mod'] + lp['b_mod']).reshape(cond.shape[0], N_MOD, 1, D_MODEL)
    nw = lp['norm_w']
    h = rmsnorm(x, nw[0]) * (1 + mod[:, 1]) + mod[:, 0]
    x = x + 0.5 * mod[:, 2] * swiglu(h, lp['ffn1_wgu'], lp['ffn1_wd'])
    h = rmsnorm(x, nw[1]) * (1 + mod[:, 4]) + mod[:, 3]
    out, new_ctx = token_mix(h, lp, rope, ctx)
    x = x + mod[:, 5] * out
    h = rmsnorm(x, nw[2]) * (1 + mod[:, 7]) + mod[:, 6]
    x = x + 0.5 * mod[:, 8] * swiglu(h, lp['ffn2_wgu'], lp['ffn2_wd'])
    return x, new_ctx


def setup_inputs(seed: int = 0) -> dict:
    key = jax.random.key(seed)
    ks = list(jax.random.split(key, 40))
    f32 = jnp.float32
    nrm = lambda shape, s: jax.random.normal(ks.pop(), shape, f32) * s
    L = DEPTH
    dn_qkv = 2 * DN_HEADS * DN_DK + DN_HEADS * DN_DV
    a0 = jax.random.uniform(ks.pop(), (L, 2, LRU_WIDTH), f32, 0.9, 0.999)
    p = a0 ** (1.0 / LRU_C)
    lam = jnp.log(p) - jnp.log1p(-p)
    dt = jnp.exp(jax.random.uniform(ks.pop(), (L, 2, DN_HEADS), f32, math.log(1e-3), math.log(1e-1)))
    dt_bias = dt + jnp.log(-jnp.expm1(-dt))
    a_log = jnp.log(jax.random.uniform(ks.pop(), (L, 2, DN_HEADS), f32, 1.0, 16.0))
    return {
        'x_prompt': nrm((BATCH, SEQ, D_MODEL), 1.0),
        'x_sample': nrm((DEC_BATCH, DEC_SEQ, D_MODEL), 1.0),
        'cache_k': nrm((DEC_BATCH, L, PAST_LEN, ATT_KV_HEADS, HEAD_DIM), 1.0),
        'cache_v': nrm((DEC_BATCH, L, PAST_LEN, ATT_KV_HEADS, HEAD_DIM), 1.0),
        'state_delta': nrm((DEC_BATCH, L, 2, DN_HEADS, DN_DK, DN_DV), 0.1),
        'state_lru': nrm((DEC_BATCH, L, 2, LRU_WIDTH), 0.5),
        'c': nrm((DEC_BATCH, D_MODEL), 1.0),
        'c_ctx': nrm((D_MODEL,), 1.0),
        'w_mod': nrm((L, D_MODEL, N_MOD * D_MODEL), 0.5 * D_MODEL ** -0.5),
        'b_mod': nrm((L, N_MOD * D_MODEL), 0.01),
        'norm_w': 1.0 + nrm((L, 3, D_MODEL), 0.02),
        'ffn1_wgu': nrm((L, D_MODEL, 2 * D_FF), D_MODEL ** -0.5),
        'ffn1_wd': nrm((L, D_FF, D_MODEL), D_FF ** -0.5),
        'ffn2_wgu': nrm((L, D_MODEL, 2 * D_FF), D_MODEL ** -0.5),
        'ffn2_wd': nrm((L, D_FF, D_MODEL), D_FF ** -0.5),
        'w_in': nrm((L, D_MODEL, N_IN), D_MODEL ** -0.5),
        'dn_conv_w': nrm((L, CONV_W, dn_qkv), CONV_W ** -0.5),
        'dn_a_log': a_log,
        'dn_dt_bias': dt_bias,
        'dn_onorm_w': 1.0 + nrm((L, DN_DV), 0.02),
        'att_qnorm_w': 1.0 + nrm((L, HEAD_DIM), 0.02),
        'att_knorm_w': 1.0 + nrm((L, HEAD_DIM), 0.02),
        'lru_conv_w': nrm((L, CONV_W, LRU_WIDTH), CONV_W ** -0.5),
        'lru_conv_b': nrm((L, LRU_WIDTH), 0.01),
        'lru_wr': nrm((L, 2, LRU_BLOCKS, LRU_BLOCK_W, LRU_BLOCK_W), LRU_BLOCK_W ** -0.5),
        'lru_br': nrm((L, 2, LRU_WIDTH), 0.01),
        'lru_wi': nrm((L, 2, LRU_BLOCKS, LRU_BLOCK_W, LRU_BLOCK_W), LRU_BLOCK_W ** -0.5),
        'lru_bi': nrm((L, 2, LRU_WIDTH), 0.01),
        'lru_lam': lam,
        'w_pa': nrm((L, DN_HEADS * DN_DV, D_MODEL), (DN_HEADS * DN_DV) ** -0.5),
        'w_pb': nrm((L, ATT_Q_HEADS * HEAD_DIM, D_MODEL), (ATT_Q_HEADS * HEAD_DIM) ** -0.5),
        'w_pc': nrm((L, LRU_WIDTH, D_MODEL), LRU_WIDTH ** -0.5),
        'w_o': nrm((L, D_MODEL, D_MODEL), D_MODEL ** -0.5),
    }


def reference(x_prompt, x_sample, cache_k, cache_v, state_delta, state_lru, c, c_ctx, w_mod, b_mod, norm_w,
              ffn1_wgu, ffn1_wd, ffn2_wgu, ffn2_wd, w_in, dn_conv_w, dn_a_log, dn_dt_bias, dn_onorm_w,
              att_qnorm_w, att_knorm_w, lru_conv_w, lru_conv_b, lru_wr, lru_br, lru_wi, lru_bi, lru_lam,
              w_pa, w_pb, w_pc, w_o):
    rope = axial_rope(x_sample.shape[1])
    xp, xs = x_prompt, x_sample
    new_k, new_v, new_sd, new_sl = [], [], [], []
    for l in range(DEPTH):
        lp = {
            'w_mod': w_mod[l], 'b_mod': b_mod[l], 'norm_w': norm_w[l],
            'ffn1_wgu': ffn1_wgu[l], 'ffn1_wd': ffn1_wd[l], 'ffn2_wgu': ffn2_wgu[l], 'ffn2_wd': ffn2_wd[l],
            'w_in': w_in[l], 'dn_conv_w': dn_conv_w[l], 'dn_a_log': dn_a_log[l], 'dn_dt_bias': dn_dt_bias[l],
            'dn_onorm_w': dn_onorm_w[l], 'att_qnorm_w': att_qnorm_w[l], 'att_knorm_w': att_knorm_w[l],
            'lru_conv_w': lru_conv_w[l], 'lru_conv_b': lru_conv_b[l], 'lru_wr': lru_wr[l], 'lru_br': lru_br[l],
            'lru_wi': lru_wi[l], 'lru_bi': lru_bi[l], 'lru_lam': lru_lam[l],
            'w_pa': w_pa[l], 'w_pb': w_pb[l], 'w_pc': w_pc[l], 'w_o': w_o[l],
        }
        xp, (k_l, v_l, sd_l, sl_l) = run_layer(xp, c_ctx[None, :], lp, None, None)
        new_k.append(k_l)
        new_v.append(v_l)
        new_sd.append(sd_l)
        new_sl.append(sl_l)
        xs, _ = run_layer(xs, c, lp, rope, (cache_k[:, l], cache_v[:, l], state_delta[:, l], state_lru[:, l]))
    return (xp, xs, jnp.stack(new_k, axis=1), jnp.stack(new_v, axis=1), jnp.stack(new_sd, axis=1),
            jnp.stack(new_sl, axis=1))
```

```python
import functools

import numpy as np
import jax
import jax.numpy as jnp
from jax import lax
from jax.experimental import pallas as pl
from jax.experimental.pallas import tpu as pltpu

F32 = jnp.float32
BF16 = jnp.bfloat16

EPS = 1e-6
GRID_W = 64
HEAD_DIM = 64
ATT_Q_HEADS = 8
ATT_KV_HEADS = 2
ROPE_BASE = 10000.0
ROPE_PAIRS = HEAD_DIM // 4
DN_HEADS = 4
DN_DK = 64
DN_CHUNK = 64
LRU_WIDTH = 256
LRU_BLOCKS = 4
LRU_C = 8.0
N_MOD = 9
V7X_VMEM_LIMIT_BYTES = 56 * 1024 * 1024


def _params(*sem):
    return pltpu.CompilerParams(dimension_semantics=sem, vmem_limit_bytes=V7X_VMEM_LIMIT_BYTES)


def _mm(a, b):
    return jnp.dot(a.astype(BF16), b.astype(BF16), preferred_element_type=F32)


def _mm_nt(a, b):
    return lax.dot_general(a.astype(BF16), b.astype(BF16), (((1,), (1,)), ((), ())), preferred_element_type=F32)


def _mm_tn(a, b):
    return lax.dot_general(a.astype(BF16), b.astype(BF16), (((0,), (0,)), ((), ())), preferred_element_type=F32)


def _split3(x):
    hi = x.astype(BF16)
    r = x - hi.astype(F32)
    mid = r.astype(BF16)
    lo = (r - mid.astype(F32)).astype(BF16)
    return hi, mid, lo


def _mm01(x, m01):
    hi, mid, lo = _split3(x)
    d = functools.partial(jnp.dot, preferred_element_type=F32)
    return d(hi, m01) + d(mid, m01) + d(lo, m01)


def _mm01_nt(m01, x):
    hi, mid, lo = _split3(x)
    d = functools.partial(lax.dot_general, dimension_numbers=(((1,), (1,)), ((), ())), preferred_element_type=F32)
    return d(m01, hi) + d(m01, mid) + d(m01, lo)


def _mm3(a, b):
    n = a.shape[0]
    ah = a.astype(BF16)
    al = (a - ah.astype(F32)).astype(BF16)
    bh = b.astype(BF16)
    bl = (b - bh.astype(F32)).astype(BF16)
    top = jnp.dot(jnp.concatenate([ah, al], axis=0), bh, preferred_element_type=F32)
    return (top[0:n] + top[n:2 * n]) + jnp.dot(ah, bl, preferred_element_type=F32)


def _silu(x):
    return x * jax.nn.sigmoid(x)


def _softplus(x):
    return jnp.maximum(x, 0.0) + jnp.log1p(jnp.exp(-jnp.abs(x)))


def _gelu_tanh(x):
    return x * (0.5 * (1.0 + jnp.tanh(0.7978845608028654 * (x + 0.044715 * (x * x * x)))))


def _norm_mod(x, nw, scale, shift):
    ms = jnp.mean(x * x, axis=-1, keepdims=True)
    y = (x * lax.rsqrt(ms + EPS)) * nw
    return y * (1.0 + scale) + shift


def _seg_masks(n, seg):
    r = lax.broadcasted_iota(jnp.int32, (n, n), 0)
    c = lax.broadcasted_iota(jnp.int32, (n, n), 1)
    return r, c, (r // seg) == (c // seg)


def _swap16(x):
    w = x.shape[1]
    lane = lax.broadcasted_iota(jnp.int32, x.shape, 1)
    return jnp.where((lane & 16) == 0, pltpu.roll(x, w - 16, axis=1), pltpu.roll(x, 16, axis=1))


def _mod_kernel(c_ref, w_ref, b_ref, o_ref):
    o_ref[...] = _mm(_silu(c_ref[...]), w_ref[...]) + b_ref[...]


def _mod_call(cond, w_mod, b_mod):
    L, D, N = w_mod.shape
    R = cond.shape[0]
    tn = D
    return pl.pallas_call(
        _mod_kernel,
        grid=(L, N // tn),
        in_specs=[pl.BlockSpec((R, D), lambda l, j: (0, 0)),
                  pl.BlockSpec((None, D, tn), lambda l, j: (l, 0, j)),
                  pl.BlockSpec((None, 1, tn), lambda l, j: (l, 0, j))],
        out_specs=pl.BlockSpec((None, R, tn), lambda l, j: (l, 0, j)),
        out_shape=jax.ShapeDtypeStruct((L, R, N), F32),
        compiler_params=_params("parallel", "parallel"),
        name="mod",
    )(cond, w_mod, b_mod.reshape(L, 1, N))


def _ffn_kernel(x_ref, mod_ref, nw_ref, wg_ref, wu_ref, wd_ref, o_ref, h_sc, acc_sc, *, sub):
    f = pl.program_id(2)

    @pl.when(f == 0)
    def _():
        h = _norm_mod(x_ref[...], nw_ref[sub:sub + 1, :], mod_ref[3 * sub + 1:3 * sub + 2, :],
                      mod_ref[3 * sub:3 * sub + 1, :])
        h_sc[...] = h.astype(BF16)
        acc_sc[...] = jnp.zeros_like(acc_sc)

    h = h_sc[...]
    g = jnp.dot(h, wg_ref[...], preferred_element_type=F32)
    u = jnp.dot(h, wu_ref[...], preferred_element_type=F32)
    acc_sc[...] += _mm(_silu(g) * u, wd_ref[...])

    @pl.when(f == pl.num_programs(2) - 1)
    def _():
        o_ref[...] = x_ref[...] + (0.5 * mod_ref[3 * sub + 2:3 * sub + 3, :]) * acc_sc[...]


def _ffn_call(x, mod, norm_w, w_gu, w_d, *, layer, sub, mod_off, tm=512, tf=1408):
    B, T, D = x.shape
    F = w_d.shape[1]
    tm = min(tm, T)
    tf = tf if F % tf == 0 else F
    nf = F // tf
    return pl.pallas_call(
        functools.partial(_ffn_kernel, sub=sub),
        grid=(B, T // tm, nf),
        in_specs=[pl.BlockSpec((None, tm, D), lambda b, t, f: (b, t, 0)),
                  pl.BlockSpec((None, None, N_MOD, D), lambda b, t, f: (layer, b + mod_off, 0, 0)),
                  pl.BlockSpec((None, 3, D), lambda b, t, f: (layer, 0, 0)),
                  pl.BlockSpec((None, D, tf), lambda b, t, f: (layer, 0, f)),
                  pl.BlockSpec((None, D, tf), lambda b, t, f: (layer, 0, f + nf)),
                  pl.BlockSpec((None, tf, D), lambda b, t, f: (layer, f, 0))],
        out_specs=pl.BlockSpec((None, tm, D), lambda b, t, f: (b, t, 0)),
        out_shape=jax.ShapeDtypeStruct((B, T, D), F32),
        scratch_shapes=[pltpu.VMEM((tm, D), BF16), pltpu.VMEM((tm, D), F32)],
        compiler_params=_params("parallel", "parallel", "arbitrary"),
        name="ffn",
    )(x, mod, norm_w, w_gu, w_gu, w_d)


def _inproj_kernel(*refs, rope, emit_kv):
    (x_ref, mod_ref, nw_ref, w_ref, wba_ref, qw_ref, kw_ref, bd_ref) = refs[:8]
    refs = refs[8:]
    if rope:
        rc_ref, rs_ref = refs[:2]
        refs = refs[2:]
    qkv_ref, z_ref, xy_ref, q_ref, kT_ref, v_ref, ba_ref = refs[:7]
    refs = refs[7:]

    h = _norm_mod(x_ref[...], nw_ref[1:2, :], mod_ref[4:5, :], mod_ref[3:4, :]).astype(BF16)
    y = jnp.dot(h, w_ref[...], preferred_element_type=F32)
    qkv_ref[...] = y[:, 0:768]
    z_ref[...] = y[:, 768:1024]
    xy_ref[...] = y[:, 1792:2304]
    ba_ref[...] = lax.dot_general(wba_ref[...], h, (((1,), (1,)), ((), ())), preferred_element_type=F32)

    aq = y[:, 1024:1536]
    ak = y[:, 1536:1664]
    v = y[:, 1664:1792]
    bd = bd_ref[...]
    inv_hd = 1.0 / HEAD_DIM
    qn = (aq * lax.rsqrt(_mm01(aq * aq, bd) * inv_hd + EPS)) * qw_ref[...]
    kn = (ak * lax.rsqrt(_mm01(ak * ak, bd[0:128, 0:128]) * inv_hd + EPS)) * kw_ref[...]
    if emit_kv:
        kf_ref, vf_ref = refs
        kf_ref[...] = kn
        vf_ref[...] = v
    if rope:
        rc = rc_ref[...]
        rs = rs_ref[...]
        kn = kn * rc + _swap16(kn) * rs
        qn = qn * jnp.concatenate([rc] * 4, axis=1) + _swap16(qn) * jnp.concatenate([rs] * 4, axis=1)
    qs = qn * (HEAD_DIM ** -0.5)
    for hh in range(ATT_Q_HEADS):
        q_ref[hh] = qs[:, hh * HEAD_DIM:(hh + 1) * HEAD_DIM].astype(BF16)
    kT = kn.T
    kT_ref[0] = kT[0:HEAD_DIM, :].astype(BF16)
    kT_ref[1] = kT[HEAD_DIM:2 * HEAD_DIM, :].astype(BF16)
    v_ref[...] = v.astype(BF16)


def _inproj_call(x, mod, norm_w, w_main, w_ba, qw, kw, bd512, rope_tabs, *, layer, mod_off, emit_kv, tm=512):
    B, T, D = x.shape
    tm = min(tm, T)
    NW = w_main.shape[2]
    rope = rope_tabs is not None
    im = lambda b, t: (b, t, 0)
    in_specs = [pl.BlockSpec((None, tm, D), im),
                pl.BlockSpec((None, None, N_MOD, D), lambda b, t: (layer, b + mod_off, 0, 0)),
                pl.BlockSpec((None, 3, D), lambda b, t: (layer, 0, 0)),
                pl.BlockSpec((None, D, NW), lambda b, t: (layer, 0, 0)),
                pl.BlockSpec((None, 16, D), lambda b, t: (layer, 0, 0)),
                pl.BlockSpec((None, 1, 512), lambda b, t: (layer, 0, 0)),
                pl.BlockSpec((None, 1, 128), lambda b, t: (layer, 0, 0)),
                pl.BlockSpec((512, 512), lambda b, t: (0, 0))]
    args = [x, mod, norm_w, w_main, w_ba, qw, kw, bd512]
    if rope:
        in_specs += [pl.BlockSpec((tm, 128), lambda b, t: (t, 0))] * 2
        args += list(rope_tabs)
    out_shape = [jax.ShapeDtypeStruct((B, T, 768), F32), jax.ShapeDtypeStruct((B, T, 256), F32),
                 jax.ShapeDtypeStruct((B, T, 512), F32), jax.ShapeDtypeStruct((B, ATT_Q_HEADS, T, HEAD_DIM), BF16),
                 jax.ShapeDtypeStruct((B, ATT_KV_HEADS, HEAD_DIM, T), BF16), jax.ShapeDtypeStruct((B, T, 128), BF16),
                 jax.ShapeDtypeStruct((B, 16, T), F32)]
    out_specs = [pl.BlockSpec((None, tm, 768), im), pl.BlockSpec((None, tm, 256), im), pl.BlockSpec((None, tm, 512), im),
                 pl.BlockSpec((None, ATT_Q_HEADS, tm, HEAD_DIM), lambda b, t: (b, 0, t, 0)),
                 pl.BlockSpec((None, ATT_KV_HEADS, HEAD_DIM, tm), lambda b, t: (b, 0, 0, t)),
                 pl.BlockSpec((None, tm, 128), im),
                 pl.BlockSpec((None, 16, tm), lambda b, t: (b, 0, t))]
    if emit_kv:
        out_shape += [jax.ShapeDtypeStruct((B, T, 128), F32)] * 2
        out_specs += [pl.BlockSpec((None, tm, 128), im)] * 2
    return pl.pallas_call(
        functools.partial(_inproj_kernel, rope=rope, emit_kv=emit_kv),
        grid=(B, T // tm),
        in_specs=in_specs, out_specs=out_specs, out_shape=out_shape,
        compiler_params=_params("parallel", "parallel"),
        name="inproj",
    )(*args)


def _conv_window(main, prev, nxt, at_start, at_end):
    prev = jnp.where(at_start, 0.0, prev)
    nxt = jnp.where(at_end, 0.0, nxt)
    return jnp.concatenate([prev, main, nxt], axis=0)


def _conv4(win, w, n):
    return (w[0:1, :] * win[6:6 + n] + w[1:2, :] * win[7:7 + n]) + (w[2:3, :] * win[8:8 + n] + w[3:4, :] * win[9:9 + n])


def _dnprep_kernel(main_ref, prev_ref, next_ref, cw_ref, bd_ref, o_ref, *, tm, seq):
    t = pl.program_id(1)
    at_start = (t * tm) % seq == 0
    at_end = ((t + 1) * tm) % seq == 0
    win = _conv_window(main_ref[...], prev_ref[...], next_ref[...], at_start, at_end)
    y = _silu(_conv4(win, cw_ref[...], tm))
    bd = bd_ref[...]
    q = y[:, 0:256]
    k = y[:, 256:512]
    o_ref[:, 0:256] = (q * lax.rsqrt(_mm01(q * q, bd) + EPS)) * (DN_DK ** -0.5)
    o_ref[:, 256:512] = k * lax.rsqrt(_mm01(k * k, bd) + EPS)
    o_ref[:, 512:768] = y[:, 512:768]


def _dnprep_call(qkv, conv_w, bd256, *, layer, seq, tm=512):
    B, T, C = qkv.shape
    tm = min(tm, seq)
    nb8 = T // 8
    return pl.pallas_call(
        functools.partial(_dnprep_kernel, tm=tm, seq=seq),
        grid=(B, T // tm),
        in_specs=[pl.BlockSpec((None, tm, C), lambda b, t: (b, t, 0)),
                  pl.BlockSpec((None, 8, C), lambda b, t: (b, jnp.maximum(t * (tm // 8) - 1, 0), 0)),
                  pl.BlockSpec((None, 8, C), lambda b, t: (b, jnp.minimum((t + 1) * (tm // 8), nb8 - 1), 0)),
                  pl.BlockSpec((None, 4, C), lambda b, t: (layer, 0, 0)),
                  pl.BlockSpec((256, 256), lambda b, t: (0, 0))],
        out_specs=pl.BlockSpec((None, tm, C), lambda b, t: (b, t, 0)),
        out_shape=jax.ShapeDtypeStruct((B, T, C), F32),
        compiler_params=_params("parallel", "parallel"),
        name="dnprep",
    )(qkv, qkv, qkv, conv_w, bd256)


def _dn_kernel(q_ref, k_ref, v_ref, ba_ref, prm_ref, s0_ref, onw_ref, o_ref, sfin_ref,
               of_sc, ob_sc, rows_sc, cols_sc, *, n_chunks, norm_rows):
    C = DN_CHUNK
    W = 2 * C
    r, c, same = _seg_masks(W, C)
    i_loc = r % C
    j_loc = c % C
    m01 = lambda mask: jnp.where(mask, 1.0, 0.0).astype(BF16)
    cum_f = m01(same & (i_loc <= j_loc))
    cum_b = m01(same & (i_loc >= j_loc))
    ones_bd = m01(same)
    eye = m01(r == c)
    eye_f = jnp.where(r == c, 1.0, 0.0)
    blk = tuple((r // s) == (c // s) for s in (8, 16, 32, 64))
    incl = (same & (i_loc >= j_loc), same & (i_loc <= j_loc))
    strict = (same & (i_loc > j_loc), same & (i_loc < j_loc))
    lane = lax.broadcasted_iota(jnp.int32, (C, W), 1)
    head0 = lane < C

    alog = prm_ref[0:2, :]
    dtb = prm_ref[2:4, :]
    row2 = lax.broadcasted_iota(jnp.int32, (2, W), 0)

    def prologue(ci, carry):
        bg = ba_ref[ci]
        beta = jax.nn.sigmoid(bg[0:2, :])
        g = -jnp.exp(alog) * _softplus(bg[2:4, :] + dtb)
        gc = jnp.where(row2 == 0, _mm01(g, cum_f), _mm01(g, cum_b))
        tot = _mm01(g, ones_bd)
        rows = jnp.concatenate([beta, gc, tot, jnp.zeros((2, W), F32)], axis=0)
        rows_sc[ci] = rows
        cols_sc[ci] = _mm01_nt(eye, rows)
        return carry

    lax.fori_loop(0, n_chunks, prologue, 0)

    def stack(x):
        return jnp.concatenate([jnp.where(head0, x, 0.0), jnp.where(head0, 0.0, x)], axis=0)

    def chunk_step(ci, d, S):
        rows = rows_sc[ci]
        cols = cols_sc[ci]
        beta_c = cols[:, d:d + 1]
        gc_c = cols[:, 2 + d:3 + d]
        tot_c = cols[:, 4 + d:5 + d]
        gc_r = rows[2 + d:3 + d, :]
        st = pl.multiple_of(ci * C, C)
        Kst = stack(k_ref[pl.ds(st, C), :])
        Qst = stack(q_ref[pl.ds(st, C), :])
        Vst = stack(v_ref[pl.ds(st, C), :])
        dec = jnp.exp(jnp.where(incl[d], gc_c - gc_r, -jnp.inf))
        Kb = Kst.astype(BF16)
        kq = _mm_nt(jnp.concatenate([Kb, Qst.astype(BF16)], axis=0), Kb)
        L = (beta_c * kq[0:W]) * jnp.where(strict[d], dec, 0.0)
        QK = kq[W:2 * W] * dec
        D8 = jnp.where(blk[0], L, 0.0)
        M = _mm3(D8, D8)
        R = (M - D8) - _mm3(D8, M)
        M = _mm3(M, M)
        Tm = eye_f + ((R + M) + _mm3(R, M))
        for lvl in range(3):
            Bm = jnp.where(blk[lvl + 1] & jnp.logical_not(blk[lvl]), L, 0.0)
            Tm = Tm - _mm3(_mm3(Tm, Bm), Tm)
        eg_c = jnp.exp(gc_c)
        rhs = jnp.concatenate([beta_c * Vst, (beta_c * eg_c) * Kst], axis=1)
        X = _mm3(Tm, rhs)
        u = X[:, 0:W]
        wk = X[:, W:2 * W]
        qd = Qst * eg_c
        kd = Kst * jnp.exp(tot_c - gc_c)
        ws = _mm(jnp.concatenate([wk, qd], axis=0), S)
        w = u - ws[0:W]
        o = ws[W:2 * W] + _mm(QK, w)
        S_new = jnp.exp(tot_c) * S + _mm_tn(kd, w)
        return o[0:C] + o[C:W], S_new, st

    def step(s, carry):
        Sf, Sb = carry
        of, Sf, stf = chunk_step(s, 0, Sf)
        of_sc[pl.ds(stf, C), :] = of
        ob, Sb, stb = chunk_step(n_chunks - 1 - s, 1, Sb)
        ob_sc[pl.ds(stb, C), :] = ob
        return Sf, Sb

    Sf, Sb = lax.fori_loop(0, n_chunks, step, (s0_ref[0], s0_ref[1]))
    sfin_ref[0] = Sf
    sfin_ref[1] = Sb

    onw = onw_ref[...]
    inv_dv = 1.0 / C

    def epilogue(i, carry):
        st = pl.multiple_of(i * norm_rows, norm_rows)
        o = of_sc[pl.ds(st, norm_rows), :] + ob_sc[pl.ds(st, norm_rows), :]
        ms = _mm01(o * o, ones_bd) * inv_dv
        o_ref[pl.ds(st, norm_rows), :] = (o * lax.rsqrt(ms + EPS)) * onw
        return carry

    lax.fori_loop(0, (n_chunks * C) // norm_rows, epilogue, 0)


def _dn_call(qkv, ba, prm, s0, onw, *, layer):
    B, T, _ = qkv.shape
    n = T // DN_CHUNK
    W = 2 * DN_CHUNK
    norm_rows = min(256, T)
    return pl.pallas_call(
        functools.partial(_dn_kernel, n_chunks=n, norm_rows=norm_rows),
        grid=(B, 2),
        in_specs=[pl.BlockSpec((None, T, W), lambda b, p: (b, 0, p)),
                  pl.BlockSpec((None, T, W), lambda b, p: (b, 0, 2 + p)),
                  pl.BlockSpec((None, T, W), lambda b, p: (b, 0, 4 + p)),
                  pl.BlockSpec((None, None, n, 4, W), lambda b, p: (b, p, 0, 0, 0)),
                  pl.BlockSpec((None, None, 4, W), lambda b, p: (layer, p, 0, 0)),
                  pl.BlockSpec((None, None, 2, W, W), lambda b, p: (b, p, 0, 0, 0)),
                  pl.BlockSpec((None, 1, W), lambda b, p: (layer, 0, 0))],
        out_specs=[pl.BlockSpec((None, T, W), lambda b, p: (b, 0, p)),
                   pl.BlockSpec((None, None, 2, W, W), lambda b, p: (b, p, 0, 0, 0))],
        out_shape=[jax.ShapeDtypeStruct((B, T, 2 * W), F32), jax.ShapeDtypeStruct((B, 2, 2, W, W), F32)],
        scratch_shapes=[pltpu.VMEM((T, W), F32), pltpu.VMEM((T, W), F32),
                        pltpu.VMEM((n, 8, W), F32), pltpu.VMEM((n, W, 8), F32)],
        compiler_params=_params("parallel", "parallel"),
        name="dn",
    )(qkv, qkv, qkv, ba, prm, s0, onw)


def _attn_kernel(*refs, seg_blocks, tq):
    q_ref = refs[0]
    nseg = len(seg_blocks)
    e_ref = refs[1 + 2 * nseg]
    o_ref = refs[2 + 2 * nseg]
    G = ATT_Q_HEADS // ATT_KV_HEADS
    q = q_ref[...].reshape(G * tq, HEAD_DIM)
    m = jnp.full((G * tq, 1), -jnp.inf, F32)
    l = jnp.zeros((G * tq, 1), F32)
    acc = jnp.zeros((G * tq, 2 * HEAD_DIM), F32)
    for si, (nblk, kb) in enumerate(seg_blocks):
        kT_ref = refs[1 + 2 * si]
        v_ref = refs[2 + 2 * si]
        for j in range(nblk):
            s = jnp.dot(q, kT_ref[:, j * kb:(j + 1) * kb], preferred_element_type=F32)
            m_new = jnp.maximum(m, jnp.max(s, axis=-1, keepdims=True))
            alpha = jnp.exp(m - m_new)
            p = jnp.exp(s - m_new)
            l = alpha * l + jnp.sum(p, axis=-1, keepdims=True)
            acc = alpha * acc + jnp.dot(p.astype(BF16), v_ref[j * kb:(j + 1) * kb, :], preferred_element_type=F32)
            m = m_new
    o = (acc / l).astype(BF16)
    out = jnp.dot(o[0:tq], e_ref[0], preferred_element_type=F32)
    for hh in range(1, G):
        out = out + jnp.dot(o[hh * tq:(hh + 1) * tq], e_ref[hh], preferred_element_type=F32)
    o_ref[...] = out.astype(BF16)


def _attn_call(q, segs, place, *, grid, q_map, seg_maps, out_map, out_rows, tq, kb):
    G = ATT_Q_HEADS // ATT_KV_HEADS
    in_specs = [pl.BlockSpec((None, G, tq, HEAD_DIM), q_map)]
    args = [q]
    seg_blocks = []
    for (kT, v, S), (k_map, v_map) in zip(segs, seg_maps):
        blk = min(kb, S)
        seg_blocks.append((S // blk, blk))
        in_specs.append(pl.BlockSpec((None,) * (kT.ndim - 2) + (HEAD_DIM, S), k_map))
        in_specs.append(pl.BlockSpec((None,) * (v.ndim - 2) + (S, 2 * HEAD_DIM), v_map))
        args += [kT, v]
    in_specs.append(pl.BlockSpec((None, G, 2 * HEAD_DIM, G * HEAD_DIM), lambda b, g, t: (g, 0, 0, 0)))
    args.append(place)
    return pl.pallas_call(
        functools.partial(_attn_kernel, seg_blocks=tuple(seg_blocks), tq=tq),
        grid=grid,
        in_specs=in_specs,
        out_specs=pl.BlockSpec((None, tq, G * HEAD_DIM), out_map),
        out_shape=jax.ShapeDtypeStruct((1 if out_rows[0] is None else out_rows[0], out_rows[1], ATT_Q_HEADS * HEAD_DIM), BF16),
        compiler_params=_params("parallel", "parallel", "parallel"),
        name="attn",
    )(*args)


def _lru_kernel(xy_ref, cw_ref, cb_ref, wr_ref, br_ref, wi_ref, bi_ref, lam_ref, h0_ref, o_ref, hfin_ref,
                hf_sc, hb_sc, a_sc, u_sc, *, T, tt):
    W = LRU_WIDTH
    nt = T // tt
    cw = cw_ref[...]
    cb = cb_ref[...]
    rid = lax.broadcasted_iota(jnp.int32, (8, W), 0)

    def gates(i, d):
        st = pl.multiple_of(i * tt, tt)
        main = xy_ref[pl.ds(st, tt), 0:W]
        prev = xy_ref[pl.ds(pl.multiple_of(jnp.maximum(st - 8, 0), 8), 8), 0:W]
        nxt = xy_ref[pl.ds(pl.multiple_of(jnp.minimum(st + tt, T - 8), 8), 8), 0:W]
        x = _conv4(_conv_window(main, prev, nxt, i == 0, i == nt - 1), cw, tt) + cb
        rg = jax.nn.sigmoid(_mm(x, wr_ref[d]) + br_ref[d:d + 1, :])
        ig = jax.nn.sigmoid(_mm(x, wi_ref[d]) + bi_ref[d:d + 1, :])
        log_a = (-LRU_C * rg) * _softplus(-lam_ref[d:d + 1, :])
        a = jnp.exp(log_a)
        a_sc[...] = a
        u_sc[...] = jnp.sqrt(-jnp.tanh(log_a) * (a * a + 1.0)) * (ig * x)
        return st

    def scan8(j8, h, dst_ref, base, reverse):
        r0 = pl.multiple_of(j8 * 8, 8)
        a8 = a_sc[pl.ds(r0, 8), :]
        u8 = u_sc[pl.ds(r0, 8), :]
        out = jnp.zeros((8, W), F32)
        order = range(7, -1, -1) if reverse else range(8)
        for j in order:
            h = jnp.broadcast_to(a8[j:j + 1, :], (8, W)) * h + jnp.broadcast_to(u8[j:j + 1, :], (8, W))
            out = jnp.where(rid == j, h, out)
        dst_ref[pl.ds(pl.multiple_of(base + r0, 8), 8), :] = out
        return h

    def fwd_tile(i, h):
        st = gates(i, 0)
        return lax.fori_loop(0, tt // 8, lambda j8, hh: scan8(j8, hh, hf_sc, st, False), h)

    h = lax.fori_loop(0, nt, fwd_tile, jnp.broadcast_to(h0_ref[0:1, :], (8, W)))
    hfin_ref[0:1, :] = h[0:1, :]

    def bwd_tile(ii, h):
        i = nt - 1 - ii
        st = gates(i, 1)
        h = lax.fori_loop(0, tt // 8, lambda jj, hh: scan8(tt // 8 - 1 - jj, hh, hb_sc, 0, True), h)
        y = xy_ref[pl.ds(st, tt), W:2 * W]
        o_ref[pl.ds(st, tt), :] = _gelu_tanh(y) * (hf_sc[pl.ds(st, tt), :] + hb_sc[...])
        return h

    h = lax.fori_loop(0, nt, bwd_tile, jnp.broadcast_to(h0_ref[1:2, :], (8, W)))
    hfin_ref[1:2, :] = h[0:1, :]


def _lru_call(xy, conv_w, conv_b, wr, br, wi, bi, lam, h0, *, layer, h0_layer):
    B, T, _ = xy.shape
    W = LRU_WIDTH
    tt = min(256, T)
    lmap = lambda b: (layer, 0, 0)
    h0_spec = (pl.BlockSpec((None, 2, W), lambda b: (b, 0, 0)) if h0_layer is None
               else pl.BlockSpec((None, None, 2, W), lambda b: (b, h0_layer, 0, 0)))
    return pl.pallas_call(
        functools.partial(_lru_kernel, T=T, tt=tt),
        grid=(B,),
        in_specs=[pl.BlockSpec((None, T, 2 * W), lambda b: (b, 0, 0)),
                  pl.BlockSpec((None, 4, W), lmap),
                  pl.BlockSpec((None, 1, W), lmap),
                  pl.BlockSpec((None, 2, W, W), lambda b: (layer, 0, 0, 0)),
                  pl.BlockSpec((None, 2, W), lmap),
                  pl.BlockSpec((None, 2, W, W), lambda b: (layer, 0, 0, 0)),
                  pl.BlockSpec((None, 2, W), lmap),
                  pl.BlockSpec((None, 2, W), lmap),
                  h0_spec],
        out_specs=[pl.BlockSpec((None, T, W), lambda b: (b, 0, 0)),
                   pl.BlockSpec((None, 2, W), lambda b: (b, 0, 0))],
        out_shape=[jax.ShapeDtypeStruct((B, T, W), F32), jax.ShapeDtypeStruct((B, 2, W), F32)],
        scratch_shapes=[pltpu.VMEM((T, W), F32), pltpu.VMEM((tt, W), F32),
                        pltpu.VMEM((tt, W), F32), pltpu.VMEM((tt, W), F32)],
        compiler_params=_params("parallel"),
        name="lru",
    )(xy, conv_w, conv_b, wr, br, wi, bi, lam, h0)


def _merge_kernel(x_ref, mod_ref, nw_ref, odn_ref, z_ref, oatt_ref, olru_ref, wg_ref, wpa_ref, wpb_ref, wpc_ref,
                  wo_ref, o_ref):
    D = x_ref.shape[-1]
    x = x_ref[...]
    h = _norm_mod(x, nw_ref[1:2, :], mod_ref[4:5, :], mod_ref[3:4, :]).astype(BF16)
    gates = jax.nn.sigmoid(jnp.dot(h, wg_ref[...], preferred_element_type=F32))
    a = _mm(odn_ref[...] * _silu(z_ref[...]), wpa_ref[...])
    b = jnp.dot(oatt_ref[...], wpb_ref[...], preferred_element_type=F32)
    c = _mm(olru_ref[...], wpc_ref[...])
    merged = (gates[:, 0:D] * a + gates[:, D:2 * D] * b) + gates[:, 2 * D:3 * D] * c
    o_ref[...] = x + mod_ref[5:6, :] * _mm(merged, wo_ref[...])


def _merge_call(x, mod, norm_w, o_dn, z, o_att, o_lru, w_gate, w_pa, w_pb, w_pc, w_o, *, layer, mod_off, tm=256):
    B, T, D = x.shape
    tm = min(tm, T)
    im = lambda b, t: (b, t, 0)
    wspec = lambda w: pl.BlockSpec((None,) + w.shape[1:], lambda b, t: (layer, 0, 0))
    return pl.pallas_call(
        _merge_kernel,
        grid=(B, T // tm),
        in_specs=[pl.BlockSpec((None, tm, D), im),
                  pl.BlockSpec((None, None, N_MOD, D), lambda b, t: (layer, b + mod_off, 0, 0)),
                  pl.BlockSpec((None, 3, D), lambda b, t: (layer, 0, 0)),
                  pl.BlockSpec((None, tm, 256), im), pl.BlockSpec((None, tm, 256), im),
                  pl.BlockSpec((None, tm, 512), im), pl.BlockSpec((None, tm, 256), im),
                  wspec(w_gate), wspec(w_pa), wspec(w_pb), wspec(w_pc), wspec(w_o)],
        out_specs=pl.BlockSpec((None, tm, D), im),
        out_shape=jax.ShapeDtypeStruct((B, T, D), F32),
        compiler_params=_params("parallel", "parallel"),
        name="merge",
    )(x, mod, norm_w, o_dn, z, o_att, o_lru, w_gate, w_pa, w_pb, w_pc, w_o)


def _blockdiag_ones(n, seg):
    i = np.arange(n)
    return jnp.asarray((i[:, None] // seg) == (i[None, :] // seg), BF16)


def _placement():
    G = ATT_Q_HEADS // ATT_KV_HEADS
    e = np.zeros((ATT_KV_HEADS, G, 2 * HEAD_DIM, G * HEAD_DIM), np.float32)
    d = np.arange(HEAD_DIM)
    for g in range(ATT_KV_HEADS):
        for hh in range(G):
            e[g, hh, g * HEAD_DIM + d, hh * HEAD_DIM + d] = 1.0
    return jnp.asarray(e, BF16)


def _rope_tables(n_tokens):
    rows = n_tokens // GRID_W
    row = jnp.broadcast_to(jnp.arange(rows, dtype=F32)[:, None], (rows, GRID_W)).reshape(-1)
    col = jnp.broadcast_to(jnp.arange(GRID_W, dtype=F32)[None, :], (rows, GRID_W)).reshape(-1)
    freqs = ROPE_BASE ** (-jnp.arange(ROPE_PAIRS, dtype=F32) / ROPE_PAIRS)
    ang = jnp.stack([row[:, None] * freqs, col[:, None] * freqs], axis=1)
    cos = jnp.cos(ang)[:, :, None, :]
    sin = jnp.sin(ang)[:, :, None, :]
    c = jnp.broadcast_to(cos, (n_tokens, 2, 2, ROPE_PAIRS)).reshape(n_tokens, HEAD_DIM)
    s = jnp.concatenate([-sin, sin], axis=2).reshape(n_tokens, HEAD_DIM)
    return jnp.concatenate([c, c], axis=1), jnp.concatenate([s, s], axis=1)


def _ba_layout(ba, n):
    B = ba.shape[0]
    x = ba.reshape(B, 2, 2, 2, 2, n, DN_CHUNK)
    x = x.transpose(0, 3, 5, 1, 2, 4, 6)
    return x.reshape(B, 2, n, 4, 2 * DN_CHUNK)


def _state_to_blockdiag(s):
    B = s.shape[0]
    x = s.reshape(B, 2, 2, 2, DN_DK, DN_DK)
    z = jnp.zeros_like(x[:, :, :, 0])
    top = jnp.concatenate([x[:, :, :, 0], z], axis=-1)
    bot = jnp.concatenate([z, x[:, :, :, 1]], axis=-1)
    return jnp.concatenate([top, bot], axis=-2).transpose(0, 2, 1, 3, 4)


def _blockdiag_to_state(sb):
    B = sb.shape[0]
    x = sb.transpose(0, 2, 1, 3, 4)
    h0 = x[..., 0:DN_DK, 0:DN_DK]
    h1 = x[..., DN_DK:, DN_DK:]
    return jnp.stack([h0, h1], axis=3).reshape(B, 2, DN_HEADS, DN_DK, DN_DK)


def _lru_blockdiag(w):
    L = w.shape[0]
    bw = LRU_WIDTH // LRU_BLOCKS
    out = jnp.zeros((L, 2, LRU_WIDTH, LRU_WIDTH), w.dtype)
    for n in range(LRU_BLOCKS):
        out = out.at[:, :, n * bw:(n + 1) * bw, n * bw:(n + 1) * bw].set(w[:, :, n])
    return out


def kernel(x_prompt, x_sample, cache_k, cache_v, state_delta, state_lru, c, c_ctx, w_mod, b_mod, norm_w, ffn1_wgu,
           ffn1_wd, ffn2_wgu, ffn2_wd, w_in, dn_conv_w, dn_a_log, dn_dt_bias, dn_onorm_w, att_qnorm_w, att_knorm_w,
           lru_conv_w, lru_conv_b, lru_wr, lru_br, lru_wi, lru_bi, lru_lam, w_pa, w_pb, w_pc, w_o):
    NB, SEQ, D = x_prompt.shape
    DB, DSEQ, _ = x_sample.shape
    L = w_mod.shape[0]
    PAST = cache_k.shape[2]
    TC = NB * SEQ

    cond = jnp.zeros((16, D), F32).at[0].set(c_ctx).at[1:1 + DB].set(c)
    mod = _mod_call(cond, w_mod, b_mod).reshape(L, 16, N_MOD, D)

    bf = lambda w: w.astype(BF16)
    ffn1_wgu, ffn1_wd, ffn2_wgu, ffn2_wd = bf(ffn1_wgu), bf(ffn1_wd), bf(ffn2_wgu), bf(ffn2_wd)
    w_main = bf(jnp.concatenate([w_in[:, :, 0:1024], w_in[:, :, 1040:2320]], axis=-1))
    w_ba = bf(jnp.swapaxes(w_in[:, :, 1024:1040], 1, 2))
    w_gate = bf(w_in[:, :, 2320:])
    w_pa, w_pb, w_pc, w_o = bf(w_pa), bf(w_pb), bf(w_pc), bf(w_o)
    qw = jnp.tile(att_qnorm_w, (1, ATT_Q_HEADS)).reshape(L, 1, ATT_Q_HEADS * HEAD_DIM)
    kw = jnp.tile(att_knorm_w, (1, ATT_KV_HEADS)).reshape(L, 1, ATT_KV_HEADS * HEAD_DIM)
    onw = jnp.tile(dn_onorm_w, (1, 2)).reshape(L, 1, 2 * DN_DK)
    pr = lambda p: jnp.repeat(p.reshape(L, 2, 2, 2), DN_CHUNK, axis=-1).reshape(L, 2, 2, 2 * DN_CHUNK).transpose(0, 2, 1, 3)
    dn_prm = jnp.concatenate([pr(dn_a_log), pr(dn_dt_bias)], axis=2)
    wr_bd, wi_bd = bf(_lru_blockdiag(lru_wr)), bf(_lru_blockdiag(lru_wi))
    lru_cb = lru_conv_b.reshape(L, 1, LRU_WIDTH)
    bd512 = _blockdiag_ones(512, HEAD_DIM)
    bd256 = _blockdiag_ones(256, DN_DK)
    place = _placement()
    rope_tabs = _rope_tables(DSEQ)
    cache_kT = bf(cache_k.transpose(0, 1, 3, 4, 2))
    cache_v2 = bf(cache_v.reshape(DB, L, PAST, ATT_KV_HEADS * HEAD_DIM))
    s0_lat = _state_to_blockdiag(state_delta.transpose(1, 0, 2, 3, 4, 5).reshape(L * DB, 2, DN_HEADS, DN_DK, DN_DK))
    s0_lat = s0_lat.reshape(L, DB, 2, 2, 2 * DN_DK, 2 * DN_DK)
    s0_ctx = jnp.zeros((NB, 2, 2, 2 * DN_DK, 2 * DN_DK), F32)
    h0_ctx = jnp.zeros((NB, 2, LRU_WIDTH), F32)

    xp = x_prompt.reshape(1, TC, D)
    xs = x_sample
    new_k, new_v, new_sd, new_sl = [], [], [], []
    G = ATT_Q_HEADS // ATT_KV_HEADS
    for l in range(L):
        for ctx in (True, False):
            x = xp if ctx else xs
            mod_off = 0 if ctx else 1
            seq = SEQ if ctx else DSEQ
            nseq = NB if ctx else DB
            x = _ffn_call(x, mod, norm_w, ffn1_wgu, ffn1_wd, layer=l, sub=0, mod_off=mod_off)
            outs = _inproj_call(x, mod, norm_w, w_main, w_ba, qw, kw, bd512, None if ctx else rope_tabs,
                                layer=l, mod_off=mod_off, emit_kv=ctx)
            qkv, z, xy, q_hm, kT, v_bf, ba = outs[:7]
            qkv = _dnprep_call(qkv, dn_conv_w, bd256, layer=l, seq=seq)
            n = seq // DN_CHUNK
            ba_l = _ba_layout(ba.reshape(1 if ctx else DB, 16, -1, seq).transpose(0, 2, 1, 3).reshape(nseq, 16, seq), n)
            o_dn, s_fin = _dn_call(qkv.reshape(nseq, seq, 768), ba_l, dn_prm, s0_ctx if ctx else s0_lat[l], onw, layer=l)
            if ctx:
                o_att = _attn_call(
                    q_hm, [(kT, v_bf, SEQ)], place, grid=(NB, ATT_KV_HEADS, 1),
                    q_map=lambda s, g, t: (0, g, s, 0),
                    seg_maps=[(lambda s, g, t: (0, g, 0, s), lambda s, g, t: (0, s, 0))],
                    out_map=lambda s, g, t: (0, s, g), out_rows=(None, TC), tq=SEQ, kb=SEQ)
            else:
                tq = 128
                o_att = _attn_call(
                    q_hm, [(cache_kT, cache_v2, PAST), (kT, v_bf, DSEQ)], place, grid=(DB, ATT_KV_HEADS, DSEQ // tq),
                    q_map=lambda b, g, t: (b, g, t, 0),
                    seg_maps=[(lambda b, g, t: (b, l, g, 0, 0), lambda b, g, t: (b, l, 0, 0)),
                              (lambda b, g, t: (b, g, 0, 0), lambda b, g, t: (b, 0, 0))],
                    out_map=lambda b, g, t: (b, t, g), out_rows=(DB, DSEQ), tq=tq, kb=512)
            if ctx:
                o_lru, h_fin = _lru_call(xy.reshape(nseq, seq, 512), lru_conv_w, lru_cb, wr_bd, lru_br, wi_bd, lru_bi,
                                         lru_lam, h0_ctx, layer=l, h0_layer=None)
            else:
                o_lru, h_fin = _lru_call(xy, lru_conv_w, lru_cb, wr_bd, lru_br, wi_bd, lru_bi, lru_lam, state_lru,
                                         layer=l, h0_layer=l)
            shp = x.shape[:2]
            x = _merge_call(x, mod, norm_w, o_dn.reshape(shp + (256,)), z, o_att, o_lru.reshape(shp + (256,)),
                            w_gate, w_pa, w_pb, w_pc, w_o, layer=l, mod_off=mod_off)
            x = _ffn_call(x, mod, norm_w, ffn2_wgu, ffn2_wd, layer=l, sub=2, mod_off=mod_off)
            if ctx:
                xp = x
                kf, vf = outs[7:9]
                new_k.append(kf.reshape(NB, SEQ, ATT_KV_HEADS, HEAD_DIM))
                new_v.append(vf.reshape(NB, SEQ, ATT_KV_HEADS, HEAD_DIM))
                new_sd.append(_blockdiag_to_state(s_fin))
                new_sl.append(h_fin)
            else:
                xs = x
    return (xp.reshape(NB, SEQ, D), xs, jnp.stack(new_k, axis=1), jnp.stack(new_v, axis=1),
            jnp.stack(new_sd, axis=1), jnp.stack(new_sl, axis=1))
```

```python
import functools
import math

import numpy as np
import jax
import jax.numpy as jnp
from jax import lax
from jax.experimental import pallas as pl
from jax.experimental.pallas import tpu as pltpu

F32 = jnp.float32
BF16 = jnp.bfloat16

EPS = 1e-6
GRID_W = 64
HEAD_DIM = 64
ATT_Q_HEADS = 8
ATT_KV_HEADS = 2
ROPE_BASE = 10000.0
ROPE_PAIRS = HEAD_DIM // 4
DN_HEADS = 4
DN_DK = 64
DN_CHUNK = 64
LRU_WIDTH = 256
LRU_BLOCKS = 4
LRU_C = 8.0
N_MOD = 9
LANES = 128
V7X_VMEM_LIMIT_BYTES = 56 * 1024 * 1024


def _params(*sem):
    return pltpu.CompilerParams(dimension_semantics=sem, vmem_limit_bytes=V7X_VMEM_LIMIT_BYTES)


def _mm(a, b):
    return jnp.dot(a.astype(BF16), b.astype(BF16), preferred_element_type=F32)


def _mm_nt(a, b):
    return lax.dot_general(a.astype(BF16), b.astype(BF16), (((1,), (1,)), ((), ())), preferred_element_type=F32)


def _mm_tn(a, b):
    return lax.dot_general(a.astype(BF16), b.astype(BF16), (((0,), (0,)), ((), ())), preferred_element_type=F32)


def _split3(x):
    hi = x.astype(BF16)
    r = x - hi.astype(F32)
    mid = r.astype(BF16)
    lo = (r - mid.astype(F32)).astype(BF16)
    return hi, mid, lo


def _mm01(x, m01):
    hi, mid, lo = _split3(x)
    d = functools.partial(jnp.dot, preferred_element_type=F32)
    return d(hi, m01) + d(mid, m01) + d(lo, m01)


def _mm01_nt(m01, x):
    hi, mid, lo = _split3(x)
    d = functools.partial(lax.dot_general, dimension_numbers=(((1,), (1,)), ((), ())), preferred_element_type=F32)
    return d(m01, hi) + d(m01, mid) + d(m01, lo)


def _mm3(a, b):
    n = a.shape[0]
    ah = a.astype(BF16)
    al = (a - ah.astype(F32)).astype(BF16)
    bh = b.astype(BF16)
    bl = (b - bh.astype(F32)).astype(BF16)
    top = jnp.dot(jnp.concatenate([ah, al], axis=0), bh, preferred_element_type=F32)
    return (top[0:n] + top[n:2 * n]) + jnp.dot(ah, bl, preferred_element_type=F32)


def _mm3_many(As, Bs):
    n = As[0].shape[0]
    ah = [a.astype(BF16) for a in As]
    bh = [b.astype(BF16) for b in Bs]
    al = [(a - h.astype(F32)).astype(BF16) for a, h in zip(As, ah)]
    bl = [(b - h.astype(F32)).astype(BF16) for b, h in zip(Bs, bh)]
    top = [jnp.dot(jnp.concatenate([h, l], axis=0), b, preferred_element_type=F32) for h, l, b in zip(ah, al, bh)]
    low = [jnp.dot(h, b, preferred_element_type=F32) for h, b in zip(ah, bl)]
    return [(t[0:n] + t[n:2 * n]) + w for t, w in zip(top, low)]


def _silu(x):
    return x * jax.nn.sigmoid(x)


def _softplus(x):
    return jnp.maximum(x, 0.0) + jnp.log1p(jnp.exp(-jnp.abs(x)))


def _gelu_tanh(x):
    return x * (0.5 * (1.0 + jnp.tanh(0.7978845608028654 * (x + 0.044715 * (x * x * x)))))


def _norm_mod(x, nw, scale, shift):
    ms = jnp.mean(x * x, axis=-1, keepdims=True)
    y = (x * lax.rsqrt(ms + EPS)) * nw
    return y * (1.0 + scale) + shift


def _seg_masks(n, seg):
    r = lax.broadcasted_iota(jnp.int32, (n, n), 0)
    c = lax.broadcasted_iota(jnp.int32, (n, n), 1)
    return r, c, (r // seg) == (c // seg)


def _swap16(x):
    w = x.shape[1]
    lane = lax.broadcasted_iota(jnp.int32, x.shape, 1)
    return jnp.where((lane & 16) == 0, pltpu.roll(x, w - 16, axis=1), pltpu.roll(x, 16, axis=1))


def _mod_kernel(c_ref, w_ref, b_ref, o_ref):
    o_ref[...] = _mm(_silu(c_ref[...]), w_ref[...]) + b_ref[...]


def _mod_call(cond, w_mod, b_mod):
    L, D, N = w_mod.shape
    R = cond.shape[0]
    tn = D
    return pl.pallas_call(
        _mod_kernel,
        grid=(L, N // tn),
        in_specs=[pl.BlockSpec((R, D), lambda l, j: (0, 0)),
                  pl.BlockSpec((None, D, tn), lambda l, j: (l, 0, j)),
                  pl.BlockSpec((None, 1, tn), lambda l, j: (l, 0, j))],
        out_specs=pl.BlockSpec((None, R, tn), lambda l, j: (l, 0, j)),
        out_shape=jax.ShapeDtypeStruct((L, R, N), F32),
        compiler_params=_params("parallel", "parallel"),
        name="mod",
    )(cond, w_mod, b_mod.reshape(L, 1, N))


def _ffn_kernel(x_ref, mod_ref, nw_ref, wg_ref, wu_ref, wd_ref, o_ref, h_sc, acc_sc, *, sub):
    f = pl.program_id(2)

    @pl.when(f == 0)
    def _():
        h = _norm_mod(x_ref[...], nw_ref[sub:sub + 1, :], mod_ref[3 * sub + 1:3 * sub + 2, :],
                      mod_ref[3 * sub:3 * sub + 1, :])
        h_sc[...] = h.astype(BF16)
        acc_sc[...] = jnp.zeros_like(acc_sc)

    h = h_sc[...]
    g = jnp.dot(h, wg_ref[...], preferred_element_type=F32)
    u = jnp.dot(h, wu_ref[...], preferred_element_type=F32)
    acc_sc[...] += _mm(_silu(g) * u, wd_ref[...])

    @pl.when(f == pl.num_programs(2) - 1)
    def _():
        o_ref[...] = x_ref[...] + (0.5 * mod_ref[3 * sub + 2:3 * sub + 3, :]) * acc_sc[...]


def _ffn_call(x, mod, norm_w, w_gu, w_d, *, layer, sub, mod_off, tm=512, tf=1408):
    B, T, D = x.shape
    F = w_d.shape[1]
    tm = min(tm, T)
    tf = tf if F % tf == 0 else F
    nf = F // tf
    return pl.pallas_call(
        functools.partial(_ffn_kernel, sub=sub),
        grid=(B, T // tm, nf),
        in_specs=[pl.BlockSpec((None, tm, D), lambda b, t, f: (b, t, 0)),
                  pl.BlockSpec((None, None, N_MOD, D), lambda b, t, f: (layer, b + mod_off, 0, 0)),
                  pl.BlockSpec((None, 3, D), lambda b, t, f: (layer, 0, 0)),
                  pl.BlockSpec((None, D, tf), lambda b, t, f: (layer, 0, f)),
                  pl.BlockSpec((None, D, tf), lambda b, t, f: (layer, 0, f + nf)),
                  pl.BlockSpec((None, tf, D), lambda b, t, f: (layer, f, 0))],
        out_specs=pl.BlockSpec((None, tm, D), lambda b, t, f: (b, t, 0)),
        out_shape=jax.ShapeDtypeStruct((B, T, D), F32),
        scratch_shapes=[pltpu.VMEM((tm, D), BF16), pltpu.VMEM((tm, D), F32)],
        compiler_params=_params("parallel", "parallel", "arbitrary"),
        name="ffn",
    )(x, mod, norm_w, w_gu, w_gu, w_d)


def _inproj_kernel(*refs, rope, emit_kv):
    (x_ref, mod_ref, nw_ref, w_ref, wba_ref, qw_ref, kw_ref, bd_ref) = refs[:8]
    refs = refs[8:]
    if rope:
        rc_ref, rs_ref = refs[:2]
        refs = refs[2:]
    qkv_ref, z_ref, xy_ref, q_ref, kT_ref, v_ref, ba_ref = refs[:7]
    refs = refs[7:]

    h = _norm_mod(x_ref[...], nw_ref[1:2, :], mod_ref[4:5, :], mod_ref[3:4, :]).astype(BF16)
    y = jnp.dot(h, w_ref[...], preferred_element_type=F32)
    qkv_ref[...] = y[:, 0:768]
    z_ref[...] = y[:, 768:1024]
    xy_ref[...] = y[:, 1792:2304]
    ba_ref[...] = lax.dot_general(wba_ref[...], h, (((1,), (1,)), ((), ())), preferred_element_type=F32)

    aq = y[:, 1024:1536]
    ak = y[:, 1536:1664]
    v = y[:, 1664:1792]
    bd = bd_ref[...]
    inv_hd = 1.0 / HEAD_DIM
    qn = (aq * lax.rsqrt(_mm01(aq * aq, bd) * inv_hd + EPS)) * qw_ref[...]
    kn = (ak * lax.rsqrt(_mm01(ak * ak, bd[0:128, 0:128]) * inv_hd + EPS)) * kw_ref[...]
    if emit_kv:
        kf_ref, vf_ref = refs
        kf_ref[...] = kn
        vf_ref[...] = v
    if rope:
        rc = rc_ref[...]
        rs = rs_ref[...]
        kn = kn * rc + _swap16(kn) * rs
        qn = qn * jnp.concatenate([rc] * 4, axis=1) + _swap16(qn) * jnp.concatenate([rs] * 4, axis=1)
    qs = qn * (HEAD_DIM ** -0.5 * math.log2(math.e))
    for hh in range(ATT_Q_HEADS):
        q_ref[hh] = qs[:, hh * HEAD_DIM:(hh + 1) * HEAD_DIM].astype(BF16)
    kT = kn.T
    kT_ref[0] = kT[0:HEAD_DIM, :].astype(BF16)
    kT_ref[1] = kT[HEAD_DIM:2 * HEAD_DIM, :].astype(BF16)
    v_ref[...] = v.astype(BF16)


def _inproj_call(x, mod, norm_w, w_main, w_ba, qw, kw, bd512, rope_tabs, *, layer, mod_off, emit_kv, tm=512):
    B, T, D = x.shape
    tm = min(tm, T)
    NW = w_main.shape[2]
    rope = rope_tabs is not None
    im = lambda b, t: (b, t, 0)
    in_specs = [pl.BlockSpec((None, tm, D), im),
                pl.BlockSpec((None, None, N_MOD, D), lambda b, t: (layer, b + mod_off, 0, 0)),
                pl.BlockSpec((None, 3, D), lambda b, t: (layer, 0, 0)),
                pl.BlockSpec((None, D, NW), lambda b, t: (layer, 0, 0)),
                pl.BlockSpec((None, 16, D), lambda b, t: (layer, 0, 0)),
                pl.BlockSpec((None, 1, 512), lambda b, t: (layer, 0, 0)),
                pl.BlockSpec((None, 1, 128), lambda b, t: (layer, 0, 0)),
                pl.BlockSpec((512, 512), lambda b, t: (0, 0))]
    args = [x, mod, norm_w, w_main, w_ba, qw, kw, bd512]
    if rope:
        in_specs += [pl.BlockSpec((tm, 128), lambda b, t: (t, 0))] * 2
        args += list(rope_tabs)
    out_shape = [jax.ShapeDtypeStruct((B, T, 768), F32), jax.ShapeDtypeStruct((B, T, 256), F32),
                 jax.ShapeDtypeStruct((B, T, 512), F32), jax.ShapeDtypeStruct((B, ATT_Q_HEADS, T, HEAD_DIM), BF16),
                 jax.ShapeDtypeStruct((B, ATT_KV_HEADS, HEAD_DIM, T), BF16), jax.ShapeDtypeStruct((B, T, 128), BF16),
                 jax.ShapeDtypeStruct((B, 16, T), F32)]
    out_specs = [pl.BlockSpec((None, tm, 768), im), pl.BlockSpec((None, tm, 256), im), pl.BlockSpec((None, tm, 512), im),
                 pl.BlockSpec((None, ATT_Q_HEADS, tm, HEAD_DIM), lambda b, t: (b, 0, t, 0)),
                 pl.BlockSpec((None, ATT_KV_HEADS, HEAD_DIM, tm), lambda b, t: (b, 0, 0, t)),
                 pl.BlockSpec((None, tm, 128), im),
                 pl.BlockSpec((None, 16, tm), lambda b, t: (b, 0, t))]
    if emit_kv:
        out_shape += [jax.ShapeDtypeStruct((B, T, 128), F32)] * 2
        out_specs += [pl.BlockSpec((None, tm, 128), im)] * 2
    return pl.pallas_call(
        functools.partial(_inproj_kernel, rope=rope, emit_kv=emit_kv),
        grid=(B, T // tm),
        in_specs=in_specs, out_specs=out_specs, out_shape=out_shape,
        compiler_params=_params("parallel", "parallel"),
        name="inproj",
    )(*args)


def _conv_window(main, prev, nxt, at_start, at_end):
    prev = jnp.where(at_start, 0.0, prev)
    nxt = jnp.where(at_end, 0.0, nxt)
    return jnp.concatenate([prev, main, nxt], axis=0)


def _conv4(win, w, n):
    return (w[0:1, :] * win[6:6 + n] + w[1:2, :] * win[7:7 + n]) + (w[2:3, :] * win[8:8 + n] + w[3:4, :] * win[9:9 + n])


def _dnprep_kernel(main_ref, prev_ref, next_ref, cw_ref, bd_ref, o_ref, *, tm, seq):
    t = pl.program_id(1)
    at_start = (t * tm) % seq == 0
    at_end = ((t + 1) * tm) % seq == 0
    win = _conv_window(main_ref[...], prev_ref[...], next_ref[...], at_start, at_end)
    y = _silu(_conv4(win, cw_ref[...], tm))
    bd = bd_ref[...]
    q = y[:, 0:256]
    k = y[:, 256:512]
    o_ref[:, 0:256] = (q * lax.rsqrt(_mm01(q * q, bd) + EPS)) * (DN_DK ** -0.5)
    o_ref[:, 256:512] = k * lax.rsqrt(_mm01(k * k, bd) + EPS)
    o_ref[:, 512:768] = y[:, 512:768]


def _dnprep_call(qkv, conv_w, bd256, *, layer, seq, tm=512):
    B, T, C = qkv.shape
    tm = min(tm, seq)
    nb8 = T // 8
    return pl.pallas_call(
        functools.partial(_dnprep_kernel, tm=tm, seq=seq),
        grid=(B, T // tm),
        in_specs=[pl.BlockSpec((None, tm, C), lambda b, t: (b, t, 0)),
                  pl.BlockSpec((None, 8, C), lambda b, t: (b, jnp.maximum(t * (tm // 8) - 1, 0), 0)),
                  pl.BlockSpec((None, 8, C), lambda b, t: (b, jnp.minimum((t + 1) * (tm // 8), nb8 - 1), 0)),
                  pl.BlockSpec((None, 4, C), lambda b, t: (layer, 0, 0)),
                  pl.BlockSpec((256, 256), lambda b, t: (0, 0))],
        out_specs=pl.BlockSpec((None, tm, C), lambda b, t: (b, t, 0)),
        out_shape=jax.ShapeDtypeStruct((B, T, C), F32),
        compiler_params=_params("parallel", "parallel"),
        name="dnprep",
    )(qkv, qkv, qkv, conv_w, bd256)


def _dnchunk_kernel(q_ref, k_ref, v_ref, ba_ref, prm_ref, p_ref, qm_ref, o1_ref, o2_ref, gs_ref, *, G):
    C = DN_CHUNK
    W = 2 * C
    r, c, same = _seg_masks(W, C)
    i_loc = r % C
    j_loc = c % C
    m01 = lambda mask: jnp.where(mask, 1.0, 0.0).astype(BF16)
    cum_f = m01(same & (i_loc <= j_loc))
    cum_b = m01(same & (i_loc >= j_loc))
    ones_bd = m01(same)
    eye = m01(r == c)
    eye_f = jnp.where(r == c, 1.0, 0.0)
    blk = tuple((r // s) == (c // s) for s in (8, 16, 32, 64))
    incl = (same & (i_loc >= j_loc), same & (i_loc <= j_loc))
    strict = (same & (i_loc > j_loc), same & (i_loc < j_loc))
    head0 = lax.broadcasted_iota(jnp.int32, (C, W), 1) < C
    row2 = lax.broadcasted_iota(jnp.int32, (2, W), 0)
    alog = prm_ref[0:2, :]
    dtb = prm_ref[2:4, :]

    def stack(x):
        return jnp.concatenate([jnp.where(head0, x, 0.0), jnp.where(head0, 0.0, x)], axis=0)

    chunks = range(G)
    bg = [ba_ref[gi] for gi in chunks]
    beta = [jax.nn.sigmoid(b[0:2, :]) for b in bg]
    g = [-jnp.exp(alog) * _softplus(b[2:4, :] + dtb) for b in bg]
    gcf = [_mm01(x, cum_f) for x in g]
    gcb = [_mm01(x, cum_b) for x in g]
    tot = [_mm01(x, ones_bd) for x in g]
    rows = [jnp.concatenate([beta[i], jnp.where(row2 == 0, gcf[i], gcb[i]), tot[i], jnp.zeros((2, W), F32)], axis=0)
            for i in chunks]
    cols = [_mm01_nt(eye, x) for x in rows]
    Kst = [stack(k_ref[gi * C:(gi + 1) * C, :]) for gi in chunks]
    Qst = [stack(q_ref[gi * C:(gi + 1) * C, :]) for gi in chunks]
    Vst = [stack(v_ref[gi * C:(gi + 1) * C, :]) for gi in chunks]
    kq = [_mm_nt(jnp.concatenate([Kst[i], Qst[i]], axis=0), Kst[i]) for i in chunks]

    chains = [(gi, d) for gi in chunks for d in range(2)]
    col = lambda gi, k: cols[gi][:, k:k + 1]
    dec = [jnp.exp(jnp.where(incl[d], col(gi, 2 + d) - rows[gi][2 + d:3 + d, :], -jnp.inf)) for gi, d in chains]
    L = [(col(gi, d) * kq[gi][0:W]) * jnp.where(strict[d], dec[i], 0.0) for i, (gi, d) in enumerate(chains)]
    QK = [kq[gi][W:2 * W] * dec[i] for i, (gi, d) in enumerate(chains)]
    D8 = [jnp.where(blk[0], x, 0.0) for x in L]
    M = _mm3_many(D8, D8)
    DM = _mm3_many(D8, M)
    R = [(m - d8) - dm for m, d8, dm in zip(M, D8, DM)]
    M = _mm3_many(M, M)
    RM = _mm3_many(R, M)
    Tm = [eye_f + ((r_ + m) + rm) for r_, m, rm in zip(R, M, RM)]
    for lvl in range(3):
        off = blk[lvl + 1] & jnp.logical_not(blk[lvl])
        Bm = [jnp.where(off, x, 0.0) for x in L]
        TBT = _mm3_many(_mm3_many(Tm, Bm), Tm)
        Tm = [t - x for t, x in zip(Tm, TBT)]
    rhs = [jnp.concatenate([(col(gi, d) * jnp.exp(col(gi, 2 + d))) * Kst[gi], col(gi, d) * Vst[gi]], axis=1)
           for gi, d in chains]
    X = _mm3_many(Tm, rhs)
    kd = [Kst[gi] * jnp.exp(col(gi, 4 + d) - col(gi, 2 + d)) for gi, d in chains]
    PQ = [_mm_tn(a, x) for a, x in zip(kd, X)]
    OO = [_mm(a, x) for a, x in zip(QK, X)]
    for i, (gi, d) in enumerate(chains):
        o1 = Qst[gi] * jnp.exp(col(gi, 2 + d)) - OO[i][:, 0:W]
        o2 = OO[i][:, W:2 * W]
        p_ref[d, gi] = PQ[i][:, 0:W].astype(BF16)
        qm_ref[d, gi] = PQ[i][:, W:2 * W]
        o1_ref[d, gi] = (o1[0:C] + o1[C:W]).astype(BF16)
        o2_ref[d, gi] = o2[0:C] + o2[C:W]
        gs_ref[d, gi] = jnp.exp(rows[gi][4 + d:5 + d, :])


def _dnchunk_call(qkv, ba, prm, *, layer, G=4):
    B, T, _ = qkv.shape
    C = DN_CHUNK
    W = 2 * C
    n = T // C
    G = min(G, n)
    sds = jax.ShapeDtypeStruct
    mat = lambda rows, dt: (sds((B, 2, 2, n, rows, W), dt),
                            pl.BlockSpec((None, None, 2, G, rows, W), lambda b, p, j: (b, p, 0, j, 0, 0)))
    outs = [mat(W, BF16), mat(W, F32), mat(C, BF16), mat(C, F32), mat(1, F32)]
    return pl.pallas_call(
        functools.partial(_dnchunk_kernel, G=G),
        grid=(B, 2, n // G),
        in_specs=[pl.BlockSpec((None, G * C, W), lambda b, p, j: (b, j, p)),
                  pl.BlockSpec((None, G * C, W), lambda b, p, j: (b, j, 2 + p)),
                  pl.BlockSpec((None, G * C, W), lambda b, p, j: (b, j, 4 + p)),
                  pl.BlockSpec((None, None, G, 4, W), lambda b, p, j: (b, p, j, 0, 0)),
                  pl.BlockSpec((None, None, 4, W), lambda b, p, j: (layer, p, 0, 0))],
        out_specs=[o[1] for o in outs],
        out_shape=[o[0] for o in outs],
        compiler_params=_params("parallel", "parallel", "parallel"),
        name="dnchunk",
    )(qkv, qkv, qkv, ba, prm)


def _dnscan_kernel(pf_ref, qf_ref, o1f_ref, o2f_ref, gf_ref, pb_ref, qb_ref, o1b_ref, o2b_ref, gb_ref, s0_ref,
                   of_ref, ob_ref, sfin_ref, s_sc, *, Gs):
    C = DN_CHUNK
    W = 2 * C
    j = pl.program_id(1)

    @pl.when(j == 0)
    def _():
        s_sc[...] = s0_ref[...]

    S = [[s_sc[p, d] for d in range(2)] for p in range(2)]
    fwd = (pf_ref, qf_ref, o1f_ref, o2f_ref, gf_ref, of_ref)
    bwd = (pb_ref, qb_ref, o1b_ref, o2b_ref, gb_ref, ob_ref)
    for i in range(Gs):
        for d, (P, Qm, O1, O2, GS, out) in enumerate((fwd, bwd)):
            ci = i if d == 0 else Gs - 1 - i
            for p in range(2):
                Sb = S[p][d].astype(BF16)
                out[ci * C:(ci + 1) * C, p * W:(p + 1) * W] = (
                    jnp.dot(O1[p, ci], Sb, preferred_element_type=F32) + O2[p, ci])
                S[p][d] = (GS[p, ci] * S[p][d] - jnp.dot(P[p, ci], Sb, preferred_element_type=F32)) + Qm[p, ci]
    for p in range(2):
        for d in range(2):
            s_sc[p, d] = S[p][d]

    @pl.when(j == pl.num_programs(1) - 1)
    def _():
        sfin_ref[...] = s_sc[...]


def _dnscan_call(ops, s0, *, T, Gs=8):
    B = s0.shape[0]
    C = DN_CHUNK
    W = 2 * C
    n = T // C
    Gs = min(Gs, n)
    nb = n // Gs
    specs = []
    for d in range(2):
        for a in ops:
            rows = a.shape[4]
            if d == 0:
                specs.append(pl.BlockSpec((None, 2, None, Gs, rows, W), lambda b, j: (b, 0, 0, j, 0, 0)))
            else:
                specs.append(pl.BlockSpec((None, 2, None, Gs, rows, W), lambda b, j: (b, 0, 1, nb - 1 - j, 0, 0)))
    st_spec = pl.BlockSpec((None, 2, 2, W, W), lambda b, j: (b, 0, 0, 0, 0))
    return pl.pallas_call(
        functools.partial(_dnscan_kernel, Gs=Gs),
        grid=(B, nb),
        in_specs=specs + [st_spec],
        out_specs=[pl.BlockSpec((None, Gs * C, 2 * W), lambda b, j: (b, j, 0)),
                   pl.BlockSpec((None, Gs * C, 2 * W), lambda b, j: (b, nb - 1 - j, 0)),
                   st_spec],
        out_shape=[jax.ShapeDtypeStruct((B, T, 2 * W), F32), jax.ShapeDtypeStruct((B, T, 2 * W), F32),
                   jax.ShapeDtypeStruct((B, 2, 2, W, W), F32)],
        scratch_shapes=[pltpu.VMEM((2, 2, W, W), F32)],
        compiler_params=_params("parallel", "arbitrary"),
        name="dnscan",
    )(*ops, *ops, s0)


def _attn_kernel(*refs, seg_blocks, tq, rt):
    q_ref = refs[0]
    nseg = len(seg_blocks)
    e_ref, o_ref = refs[1 + 2 * nseg:3 + 2 * nseg]
    G = ATT_Q_HEADS // ATT_KV_HEADS
    rows = G * tq
    q = q_ref[...].reshape(rows, HEAD_DIM)
    rt = rows if rt is None else rt
    nrt = rows // rt
    m = [None] * nrt
    l = [None] * nrt
    acc = [None] * nrt
    for si, (nblk, kb) in enumerate(seg_blocks):
        for j in range(nblk):
            kT = refs[1 + 2 * si][:, j * kb:(j + 1) * kb]
            v = refs[2 + 2 * si][j * kb:(j + 1) * kb, :]
            for r in range(nrt):
                s = jnp.dot(q[r * rt:(r + 1) * rt], kT, preferred_element_type=F32)
                smax = jnp.max(s, axis=-1, keepdims=True)
                if m[r] is None:
                    m[r] = smax
                    p = jnp.exp2(s - smax)
                    l[r] = jnp.sum(p, axis=-1, keepdims=True)
                    acc[r] = jnp.dot(p.astype(BF16), v, preferred_element_type=F32)
                else:
                    m_new = jnp.maximum(m[r], smax)
                    alpha = jnp.exp2(m[r] - m_new)
                    p = jnp.exp2(s - m_new)
                    l[r] = alpha * l[r] + jnp.sum(p, axis=-1, keepdims=True)
                    acc[r] = alpha * acc[r] + jnp.dot(p.astype(BF16), v, preferred_element_type=F32)
                    m[r] = m_new
    o = jnp.concatenate([a / d for a, d in zip(acc, l)], axis=0).astype(BF16)
    out = jnp.dot(o[0:tq], e_ref[0], preferred_element_type=F32)
    for hh in range(1, G):
        out = out + jnp.dot(o[hh * tq:(hh + 1) * tq], e_ref[hh], preferred_element_type=F32)
    o_ref[...] = out.astype(BF16)


def _attn_call(q, segs, place, *, grid, q_map, seg_maps, out_map, out_rows, tq, kb, rt=None):
    G = ATT_Q_HEADS // ATT_KV_HEADS
    in_specs = [pl.BlockSpec((None, G, tq, HEAD_DIM), q_map)]
    args = [q]
    seg_blocks = []
    s_total = 0
    for (kT, v, S), (k_map, v_map) in zip(segs, seg_maps):
        blk = min(kb, S)
        seg_blocks.append((S // blk, blk))
        s_total += S
        in_specs.append(pl.BlockSpec((None,) * (kT.ndim - 2) + (HEAD_DIM, S), k_map))
        in_specs.append(pl.BlockSpec((None,) * (v.ndim - 2) + (S, 2 * HEAD_DIM), v_map))
        args += [kT, v]
    in_specs.append(pl.BlockSpec((None, G, 2 * HEAD_DIM, G * HEAD_DIM), lambda b, g, t: (g, 0, 0, 0)))
    args.append(place)
    return pl.pallas_call(
        functools.partial(_attn_kernel, seg_blocks=tuple(seg_blocks), tq=tq, rt=rt),
        grid=grid,
        in_specs=in_specs,
        out_specs=pl.BlockSpec((None, tq, G * HEAD_DIM), out_map),
        out_shape=jax.ShapeDtypeStruct(out_rows + (ATT_Q_HEADS * HEAD_DIM,), BF16),
        compiler_params=_params("parallel", "parallel", "parallel"),
        name="attn",
    )(*args)


def _lru_kernel(xy_ref, cw_ref, cb_ref, wr_ref, br_ref, wi_ref, bi_ref, lam_ref, h0_ref, o_ref, hfin_ref,
                hf_sc, hb_sc, a_sc, u_sc, *, T, tt):
    W = LRU_WIDTH
    nt = T // tt
    cw = cw_ref[...]
    cb = cb_ref[...]
    rid = lax.broadcasted_iota(jnp.int32, (8, W), 0)

    def gates(i, d):
        st = pl.multiple_of(i * tt, tt)
        main = xy_ref[pl.ds(st, tt), 0:W]
        prev = xy_ref[pl.ds(pl.multiple_of(jnp.maximum(st - 8, 0), 8), 8), 0:W]
        nxt = xy_ref[pl.ds(pl.multiple_of(jnp.minimum(st + tt, T - 8), 8), 8), 0:W]
        x = _conv4(_conv_window(main, prev, nxt, i == 0, i == nt - 1), cw, tt) + cb
        rg = jax.nn.sigmoid(_mm(x, wr_ref[d]) + br_ref[d:d + 1, :])
        ig = jax.nn.sigmoid(_mm(x, wi_ref[d]) + bi_ref[d:d + 1, :])
        log_a = (-LRU_C * rg) * _softplus(-lam_ref[d:d + 1, :])
        a = jnp.exp(log_a)
        a_sc[...] = a
        u_sc[...] = jnp.sqrt(-jnp.tanh(log_a) * (a * a + 1.0)) * (ig * x)
        return st

    def scan8(j8, h, dst_ref, base, reverse):
        r0 = pl.multiple_of(j8 * 8, 8)
        a8 = a_sc[pl.ds(r0, 8), :]
        u8 = u_sc[pl.ds(r0, 8), :]
        out = jnp.zeros((8, W), F32)
        order = range(7, -1, -1) if reverse else range(8)
        for j in order:
            h = jnp.broadcast_to(a8[j:j + 1, :], (8, W)) * h + jnp.broadcast_to(u8[j:j + 1, :], (8, W))
            out = jnp.where(rid == j, h, out)
        dst_ref[pl.ds(pl.multiple_of(base + r0, 8), 8), :] = out
        return h

    def fwd_tile(i, h):
        st = gates(i, 0)
        return lax.fori_loop(0, tt // 8, lambda j8, hh: scan8(j8, hh, hf_sc, st, False), h)

    h = lax.fori_loop(0, nt, fwd_tile, jnp.broadcast_to(h0_ref[0:1, :], (8, W)))
    hfin_ref[0:1, :] = h[0:1, :]

    def bwd_tile(ii, h):
        i = nt - 1 - ii
        st = gates(i, 1)
        h = lax.fori_loop(0, tt // 8, lambda jj, hh: scan8(tt // 8 - 1 - jj, hh, hb_sc, 0, True), h)
        y = xy_ref[pl.ds(st, tt), W:2 * W]
        o_ref[pl.ds(st, tt), :] = _gelu_tanh(y) * (hf_sc[pl.ds(st, tt), :] + hb_sc[...])
        return h

    h = lax.fori_loop(0, nt, bwd_tile, jnp.broadcast_to(h0_ref[1:2, :], (8, W)))
    hfin_ref[1:2, :] = h[0:1, :]


def _lru_call(xy, conv_w, conv_b, wr, br, wi, bi, lam, h0, *, layer, h0_layer):
    B, T, _ = xy.shape
    W = LRU_WIDTH
    tt = min(256, T)
    lmap = lambda b: (layer, 0, 0)
    h0_spec = (pl.BlockSpec((None, 2, W), lambda b: (b, 0, 0)) if h0_layer is None
               else pl.BlockSpec((None, None, 2, W), lambda b: (b, h0_layer, 0, 0)))
    return pl.pallas_call(
        functools.partial(_lru_kernel, T=T, tt=tt),
        grid=(B,),
        in_specs=[pl.BlockSpec((None, T, 2 * W), lambda b: (b, 0, 0)),
                  pl.BlockSpec((None, 4, W), lmap),
                  pl.BlockSpec((None, 1, W), lmap),
                  pl.BlockSpec((None, 2, W, W), lambda b: (layer, 0, 0, 0)),
                  pl.BlockSpec((None, 2, W), lmap),
                  pl.BlockSpec((None, 2, W, W), lambda b: (layer, 0, 0, 0)),
                  pl.BlockSpec((None, 2, W), lmap),
                  pl.BlockSpec((None, 2, W), lmap),
                  h0_spec],
        out_specs=[pl.BlockSpec((None, T, W), lambda b: (b, 0, 0)),
                   pl.BlockSpec((None, 2, W), lambda b: (b, 0, 0))],
        out_shape=[jax.ShapeDtypeStruct((B, T, W), F32), jax.ShapeDtypeStruct((B, 2, W), F32)],
        scratch_shapes=[pltpu.VMEM((T, W), F32), pltpu.VMEM((tt, W), F32),
                        pltpu.VMEM((tt, W), F32), pltpu.VMEM((tt, W), F32)],
        compiler_params=_params("parallel"),
        name="lru",
    )(xy, conv_w, conv_b, wr, br, wi, bi, lam, h0)


def _merge_kernel(x_ref, mod_ref, nw_ref, of_ref, ob_ref, onw_ref, bd_ref, z_ref, oatt_ref, olru_ref,
                  wg_ref, wpa_ref, wpb_ref, wpc_ref, wo_ref, o_ref):
    D = x_ref.shape[-1]
    x = x_ref[...]
    h = _norm_mod(x, nw_ref[1:2, :], mod_ref[4:5, :], mod_ref[3:4, :]).astype(BF16)
    gates = jax.nn.sigmoid(jnp.dot(h, wg_ref[...], preferred_element_type=F32))
    odn = of_ref[...] + ob_ref[...]
    ms = _mm01(odn * odn, bd_ref[...]) * (1.0 / DN_DK)
    odn = (odn * lax.rsqrt(ms + EPS)) * onw_ref[...]
    a = _mm(odn * _silu(z_ref[...]), wpa_ref[...])
    b = jnp.dot(oatt_ref[...], wpb_ref[...], preferred_element_type=F32)
    c = _mm(olru_ref[...], wpc_ref[...])
    merged = (gates[:, 0:D] * a + gates[:, D:2 * D] * b) + gates[:, 2 * D:3 * D] * c
    o_ref[...] = x + mod_ref[5:6, :] * _mm(merged, wo_ref[...])


def _merge_call(x, mod, norm_w, o_f, o_b, onw, bd256, z, o_att, o_lru, w_gate, w_pa, w_pb, w_pc, w_o, *,
                layer, mod_off, tm=256):
    B, T, D = x.shape
    tm = min(tm, T)
    im = lambda b, t: (b, t, 0)
    wspec = lambda w: pl.BlockSpec((None,) + w.shape[1:], lambda b, t: (layer, 0, 0))
    return pl.pallas_call(
        _merge_kernel,
        grid=(B, T // tm),
        in_specs=[pl.BlockSpec((None, tm, D), im),
                  pl.BlockSpec((None, None, N_MOD, D), lambda b, t: (layer, b + mod_off, 0, 0)),
                  pl.BlockSpec((None, 3, D), lambda b, t: (layer, 0, 0)),
                  pl.BlockSpec((None, tm, 256), im), pl.BlockSpec((None, tm, 256), im),
                  pl.BlockSpec((None, 1, 256), lambda b, t: (layer, 0, 0)),
                  pl.BlockSpec((256, 256), lambda b, t: (0, 0)),
                  pl.BlockSpec((None, tm, 256), im),
                  pl.BlockSpec((None, tm, 512), im), pl.BlockSpec((None, tm, 256), im),
                  wspec(w_gate), wspec(w_pa), wspec(w_pb), wspec(w_pc), wspec(w_o)],
        out_specs=pl.BlockSpec((None, tm, D), im),
        out_shape=jax.ShapeDtypeStruct((B, T, D), F32),
        compiler_params=_params("parallel", "parallel"),
        name="merge",
    )(x, mod, norm_w, o_f, o_b, onw, bd256, z, o_att, o_lru, w_gate, w_pa, w_pb, w_pc, w_o)


def _blockdiag_ones(n, seg):
    i = np.arange(n)
    return jnp.asarray((i[:, None] // seg) == (i[None, :] // seg), BF16)


def _placement():
    G = ATT_Q_HEADS // ATT_KV_HEADS
    e = np.zeros((ATT_KV_HEADS, G, 2 * HEAD_DIM, G * HEAD_DIM), np.float32)
    d = np.arange(HEAD_DIM)
    for g in range(ATT_KV_HEADS):
        for hh in range(G):
            e[g, hh, g * HEAD_DIM + d, hh * HEAD_DIM + d] = 1.0
    return jnp.asarray(e, BF16)


def _rope_tables(n_tokens):
    rows = n_tokens // GRID_W
    row = jnp.broadcast_to(jnp.arange(rows, dtype=F32)[:, None], (rows, GRID_W)).reshape(-1)
    col = jnp.broadcast_to(jnp.arange(GRID_W, dtype=F32)[None, :], (rows, GRID_W)).reshape(-1)
    freqs = ROPE_BASE ** (-jnp.arange(ROPE_PAIRS, dtype=F32) / ROPE_PAIRS)
    ang = jnp.stack([row[:, None] * freqs, col[:, None] * freqs], axis=1)
    cos = jnp.cos(ang)[:, :, None, :]
    sin = jnp.sin(ang)[:, :, None, :]
    c = jnp.broadcast_to(cos, (n_tokens, 2, 2, ROPE_PAIRS)).reshape(n_tokens, HEAD_DIM)
    s = jnp.concatenate([-sin, sin], axis=2).reshape(n_tokens, HEAD_DIM)
    return jnp.concatenate([c, c], axis=1), jnp.concatenate([s, s], axis=1)


def _ba_layout(ba, n):
    B = ba.shape[0]
    x = ba.reshape(B, 2, 2, 2, 2, n, DN_CHUNK)
    x = x.transpose(0, 3, 5, 1, 2, 4, 6)
    return x.reshape(B, 2, n, 4, 2 * DN_CHUNK)


def _state_to_blockdiag(s):
    B = s.shape[0]
    x = s.reshape(B, 2, 2, 2, DN_DK, DN_DK)
    z = jnp.zeros_like(x[:, :, :, 0])
    top = jnp.concatenate([x[:, :, :, 0], z], axis=-1)
    bot = jnp.concatenate([z, x[:, :, :, 1]], axis=-1)
    return jnp.concatenate([top, bot], axis=-2).transpose(0, 2, 1, 3, 4)


def _blockdiag_to_state(sb):
    B = sb.shape[0]
    x = sb.transpose(0, 2, 1, 3, 4)
    h0 = x[..., 0:DN_DK, 0:DN_DK]
    h1 = x[..., DN_DK:, DN_DK:]
    return jnp.stack([h0, h1], axis=3).reshape(B, 2, DN_HEADS, DN_DK, DN_DK)


def _lru_blockdiag(w):
    L = w.shape[0]
    bw = LRU_WIDTH // LRU_BLOCKS
    out = jnp.zeros((L, 2, LRU_WIDTH, LRU_WIDTH), w.dtype)
    for n in range(LRU_BLOCKS):
        out = out.at[:, :, n * bw:(n + 1) * bw, n * bw:(n + 1) * bw].set(w[:, :, n])
    return out


def kernel(x_prompt, x_sample, cache_k, cache_v, state_delta, state_lru, c, c_ctx, w_mod, b_mod, norm_w, ffn1_wgu,
           ffn1_wd, ffn2_wgu, ffn2_wd, w_in, dn_conv_w, dn_a_log, dn_dt_bias, dn_onorm_w, att_qnorm_w, att_knorm_w,
           lru_conv_w, lru_conv_b, lru_wr, lru_br, lru_wi, lru_bi, lru_lam, w_pa, w_pb, w_pc, w_o):
    NB, SEQ, D = x_prompt.shape
    DB, DSEQ, _ = x_sample.shape
    L = w_mod.shape[0]
    PAST = cache_k.shape[2]
    TC = NB * SEQ

    cond = jnp.zeros((16, D), F32).at[0].set(c_ctx).at[1:1 + DB].set(c)
    mod = _mod_call(cond, w_mod, b_mod).reshape(L, 16, N_MOD, D)

    bf = lambda w: w.astype(BF16)
    ffn1_wgu, ffn1_wd, ffn2_wgu, ffn2_wd = bf(ffn1_wgu), bf(ffn1_wd), bf(ffn2_wgu), bf(ffn2_wd)
    w_main = bf(jnp.concatenate([w_in[:, :, 0:1024], w_in[:, :, 1040:2320]], axis=-1))
    w_ba = bf(jnp.swapaxes(w_in[:, :, 1024:1040], 1, 2))
    w_gate = bf(w_in[:, :, 2320:])
    w_pa, w_pb, w_pc, w_o = bf(w_pa), bf(w_pb), bf(w_pc), bf(w_o)
    qw = jnp.tile(att_qnorm_w, (1, ATT_Q_HEADS)).reshape(L, 1, ATT_Q_HEADS * HEAD_DIM)
    kw = jnp.tile(att_knorm_w, (1, ATT_KV_HEADS)).reshape(L, 1, ATT_KV_HEADS * HEAD_DIM)
    onw = jnp.tile(dn_onorm_w, (1, DN_HEADS)).reshape(L, 1, DN_HEADS * DN_DK)
    pr = lambda p: jnp.repeat(p.reshape(L, 2, 2, 2), DN_CHUNK, axis=-1).reshape(L, 2, 2, 2 * DN_CHUNK).transpose(0, 2, 1, 3)
    dn_prm = jnp.concatenate([pr(dn_a_log), pr(dn_dt_bias)], axis=2)
    wr_bd, wi_bd = bf(_lru_blockdiag(lru_wr)), bf(_lru_blockdiag(lru_wi))
    lru_cb = lru_conv_b.reshape(L, 1, LRU_WIDTH)
    bd512 = _blockdiag_ones(512, HEAD_DIM)
    bd256 = _blockdiag_ones(256, DN_DK)
    place = _placement()
    rope_tabs = _rope_tables(DSEQ)
    cache_kT = bf(cache_k.transpose(0, 1, 3, 4, 2))
    cache_v2 = bf(cache_v.reshape(DB, L, PAST, ATT_KV_HEADS * HEAD_DIM))
    s0_lat = _state_to_blockdiag(state_delta.transpose(1, 0, 2, 3, 4, 5).reshape(L * DB, 2, DN_HEADS, DN_DK, DN_DK))
    s0_lat = s0_lat.reshape(L, DB, 2, 2, 2 * DN_DK, 2 * DN_DK)
    s0_ctx = jnp.zeros((NB, 2, 2, 2 * DN_DK, 2 * DN_DK), F32)
    h0_ctx = jnp.zeros((NB, 2, LRU_WIDTH), F32)

    xp = x_prompt.reshape(1, TC, D)
    xs = x_sample
    new_k, new_v, new_sd, new_sl = [], [], [], []
    for l in range(L):
        for ctx in (True, False):
            x = xp if ctx else xs
            mod_off = 0 if ctx else 1
            seq = SEQ if ctx else DSEQ
            nseq = NB if ctx else DB
            shp = x.shape[:2]
            x = _ffn_call(x, mod, norm_w, ffn1_wgu, ffn1_wd, layer=l, sub=0, mod_off=mod_off)
            outs = _inproj_call(x, mod, norm_w, w_main, w_ba, qw, kw, bd512, None if ctx else rope_tabs,
                                layer=l, mod_off=mod_off, emit_kv=ctx)
            qkv, z, xy, q_hm, kT, v_bf, ba = outs[:7]
            qkv = _dnprep_call(qkv, dn_conv_w, bd256, layer=l, seq=seq)
            n = seq // DN_CHUNK
            ba_l = _ba_layout(ba.reshape(shp[0], 16, -1, seq).transpose(0, 2, 1, 3).reshape(nseq, 16, seq), n)
            ops = _dnchunk_call(qkv.reshape(nseq, seq, 768), ba_l, dn_prm, layer=l)
            o_f, o_b, s_fin = _dnscan_call(ops, s0_ctx if ctx else s0_lat[l], T=seq)
            if ctx:
                o_att = _attn_call(
                    q_hm, [(kT, v_bf, SEQ)], place, grid=(NB, ATT_KV_HEADS, 1),
                    q_map=lambda s, g, t: (0, g, s, 0),
                    seg_maps=[(lambda s, g, t: (0, g, 0, s), lambda s, g, t: (0, s, 0))],
                    out_map=lambda s, g, t: (0, s, g), out_rows=(1, TC), tq=SEQ, kb=512)
            else:
                tq = 128
                o_att = _attn_call(
                    q_hm, [(cache_kT, cache_v2, PAST), (kT, v_bf, DSEQ)], place, grid=(DB, ATT_KV_HEADS, DSEQ // tq),
                    q_map=lambda b, g, t: (b, g, t, 0),
                    seg_maps=[(lambda b, g, t: (b, l, g, 0, 0), lambda b, g, t: (b, l, 0, 0)),
                              (lambda b, g, t: (b, g, 0, 0), lambda b, g, t: (b, 0, 0))],
                    out_map=lambda b, g, t: (b, t, g), out_rows=(DB, DSEQ), tq=tq, kb=512)
            if ctx:
                o_lru, h_fin = _lru_call(xy.reshape(nseq, seq, 512), lru_conv_w, lru_cb, wr_bd, lru_br, wi_bd, lru_bi,
                                         lru_lam, h0_ctx, layer=l, h0_layer=None)
            else:
                o_lru, h_fin = _lru_call(xy, lru_conv_w, lru_cb, wr_bd, lru_br, wi_bd, lru_bi, lru_lam, state_lru,
                                         layer=l, h0_layer=l)
            x = _merge_call(x, mod, norm_w, o_f.reshape(shp + (256,)), o_b.reshape(shp + (256,)), onw, bd256, z, o_att,
                            o_lru.reshape(shp + (256,)), w_gate, w_pa, w_pb, w_pc, w_o, layer=l, mod_off=mod_off)
            x = _ffn_call(x, mod, norm_w, ffn2_wgu, ffn2_wd, layer=l, sub=2, mod_off=mod_off)
            if ctx:
                xp = x
                kf, vf = outs[7:9]
                new_k.append(kf.reshape(NB, SEQ, ATT_KV_HEADS, HEAD_DIM))
                new_v.append(vf.reshape(NB, SEQ, ATT_KV_HEADS, HEAD_DIM))
                new_sd.append(_blockdiag_to_state(s_fin))
                new_sl.append(h_fin)
            else:
                xs = x
    return (xp.reshape(NB, SEQ, D), xs, jnp.stack(new_k, axis=1), jnp.stack(new_v, axis=1),
            jnp.stack(new_sd, axis=1), jnp.stack(new_sl, axis=1))
```

```python
import functools
import math

import numpy as np
import jax
import jax.numpy as jnp
from jax import lax
from jax.experimental import pallas as pl
from jax.experimental.pallas import tpu as pltpu

F32 = jnp.float32
BF16 = jnp.bfloat16

EPS = 1e-6
GRID_W = 64
HEAD_DIM = 64
ATT_Q_HEADS = 8
ATT_KV_HEADS = 2
ROPE_BASE = 10000.0
ROPE_PAIRS = HEAD_DIM // 4
DN_HEADS = 4
DN_DK = 64
DN_CHUNK = 64
LRU_WIDTH = 256
LRU_BLOCKS = 4
LRU_C = 8.0
N_MOD = 9
LANES = 128
V7X_VMEM_LIMIT_BYTES = 56 * 1024 * 1024


def _params(*sem):
    return pltpu.CompilerParams(dimension_semantics=sem, vmem_limit_bytes=V7X_VMEM_LIMIT_BYTES)


def _mm(a, b):
    return jnp.dot(a.astype(BF16), b.astype(BF16), preferred_element_type=F32)


def _mm_nt(a, b):
    return lax.dot_general(a.astype(BF16), b.astype(BF16), (((1,), (1,)), ((), ())), preferred_element_type=F32)


def _mm_tn(a, b):
    return lax.dot_general(a.astype(BF16), b.astype(BF16), (((0,), (0,)), ((), ())), preferred_element_type=F32)


def _split3(x):
    hi = x.astype(BF16)
    r = x - hi.astype(F32)
    mid = r.astype(BF16)
    lo = (r - mid.astype(F32)).astype(BF16)
    return hi, mid, lo


def _mm01(x, m01):
    hi, mid, lo = _split3(x)
    d = functools.partial(jnp.dot, preferred_element_type=F32)
    return d(hi, m01) + d(mid, m01) + d(lo, m01)


def _mm01_nt(m01, x):
    hi, mid, lo = _split3(x)
    d = functools.partial(lax.dot_general, dimension_numbers=(((1,), (1,)), ((), ())), preferred_element_type=F32)
    return d(m01, hi) + d(m01, mid) + d(m01, lo)


def _mm3(a, b):
    n = a.shape[0]
    ah = a.astype(BF16)
    al = (a - ah.astype(F32)).astype(BF16)
    bh = b.astype(BF16)
    bl = (b - bh.astype(F32)).astype(BF16)
    top = jnp.dot(jnp.concatenate([ah, al], axis=0), bh, preferred_element_type=F32)
    return (top[0:n] + top[n:2 * n]) + jnp.dot(ah, bl, preferred_element_type=F32)


def _mm3_many(As, Bs):
    n = As[0].shape[0]
    ah = [a.astype(BF16) for a in As]
    bh = [b.astype(BF16) for b in Bs]
    al = [(a - h.astype(F32)).astype(BF16) for a, h in zip(As, ah)]
    bl = [(b - h.astype(F32)).astype(BF16) for b, h in zip(Bs, bh)]
    top = [jnp.dot(jnp.concatenate([h, l], axis=0), b, preferred_element_type=F32) for h, l, b in zip(ah, al, bh)]
    low = [jnp.dot(h, b, preferred_element_type=F32) for h, b in zip(ah, bl)]
    return [(t[0:n] + t[n:2 * n]) + w for t, w in zip(top, low)]


def _silu(x):
    return x * jax.nn.sigmoid(x)


def _softplus(x):
    return jnp.maximum(x, 0.0) + jnp.log1p(jnp.exp(-jnp.abs(x)))


def _gelu_tanh(x):
    return x * (0.5 * (1.0 + jnp.tanh(0.7978845608028654 * (x + 0.044715 * (x * x * x)))))


def _norm_mod(x, nw, scale, shift):
    ms = jnp.mean(x * x, axis=-1, keepdims=True)
    y = (x * lax.rsqrt(ms + EPS)) * nw
    return y * (1.0 + scale) + shift


def _seg_masks(n, seg):
    r = lax.broadcasted_iota(jnp.int32, (n, n), 0)
    c = lax.broadcasted_iota(jnp.int32, (n, n), 1)
    return r, c, (r // seg) == (c // seg)


def _swap16(x):
    w = x.shape[1]
    lane = lax.broadcasted_iota(jnp.int32, x.shape, 1)
    return jnp.where((lane & 16) == 0, pltpu.roll(x, w - 16, axis=1), pltpu.roll(x, 16, axis=1))


def _mod_kernel(c_ref, w_ref, b_ref, o_ref):
    o_ref[...] = _mm(_silu(c_ref[...]), w_ref[...]) + b_ref[...]


def _mod_call(cond, w_mod, b_mod):
    L, D, N = w_mod.shape
    R = cond.shape[0]
    tn = D
    return pl.pallas_call(
        _mod_kernel,
        grid=(L, N // tn),
        in_specs=[pl.BlockSpec((R, D), lambda l, j: (0, 0)),
                  pl.BlockSpec((None, D, tn), lambda l, j: (l, 0, j)),
                  pl.BlockSpec((None, 1, tn), lambda l, j: (l, 0, j))],
        out_specs=pl.BlockSpec((None, R, tn), lambda l, j: (l, 0, j)),
        out_shape=jax.ShapeDtypeStruct((L, R, N), F32),
        compiler_params=_params("parallel", "parallel"),
        name="mod",
    )(cond, w_mod, b_mod.reshape(L, 1, N))


def _ffn_kernel(x_ref, mod_ref, nw_ref, wgu_ref, wd_ref, o_ref, *, sub, F, nc):
    x = x_ref[...]
    h = _norm_mod(x, nw_ref[sub:sub + 1, :], mod_ref[3 * sub + 1:3 * sub + 2, :],
                  mod_ref[3 * sub:3 * sub + 1, :]).astype(BF16)
    dot = functools.partial(jnp.dot, preferred_element_type=F32)
    cf = F // nc
    gu = [(dot(h, wgu_ref[:, c * cf:(c + 1) * cf]), dot(h, wgu_ref[:, F + c * cf:F + (c + 1) * cf]))
          for c in range(nc)]
    acc = None
    for c, (g, u) in enumerate(gu):
        part = dot((_silu(g) * u).astype(BF16), wd_ref[c * cf:(c + 1) * cf, :])
        acc = part if acc is None else acc + part
    o_ref[...] = x + (0.5 * mod_ref[3 * sub + 2:3 * sub + 3, :]) * acc


def _ffn_call(x, mod, norm_w, w_gu, w_d, *, layer, sub, mod_off, tm=512):
    B, T, D = x.shape
    F = w_d.shape[1]
    tm = min(tm, T)
    nc = 2 if F % (2 * LANES) == 0 else 1
    return pl.pallas_call(
        functools.partial(_ffn_kernel, sub=sub, F=F, nc=nc),
        grid=(B, T // tm),
        in_specs=[pl.BlockSpec((None, tm, D), lambda b, t: (b, t, 0)),
                  pl.BlockSpec((None, None, N_MOD, D), lambda b, t: (layer, b + mod_off, 0, 0)),
                  pl.BlockSpec((None, 3, D), lambda b, t: (layer, 0, 0)),
                  pl.BlockSpec((None, D, 2 * F), lambda b, t: (layer, 0, 0), pipeline_mode=pl.Buffered(1)),
                  pl.BlockSpec((None, F, D), lambda b, t: (layer, 0, 0), pipeline_mode=pl.Buffered(1))],
        out_specs=pl.BlockSpec((None, tm, D), lambda b, t: (b, t, 0)),
        out_shape=jax.ShapeDtypeStruct((B, T, D), F32),
        compiler_params=_params("parallel", "parallel"),
        name="ffn",
    )(x, mod, norm_w, w_gu, w_d)


def _inproj_kernel(*refs, rope, emit_kv):
    (x_ref, mod_ref, nw_ref, w_ref, wba_ref, qw_ref, kw_ref, bd_ref) = refs[:8]
    refs = refs[8:]
    if rope:
        rc_ref, rs_ref = refs[:2]
        refs = refs[2:]
    qkv_ref, z_ref, xy_ref, q_ref, kT_ref, v_ref, ba_ref = refs[:7]
    refs = refs[7:]

    h = _norm_mod(x_ref[...], nw_ref[1:2, :], mod_ref[4:5, :], mod_ref[3:4, :]).astype(BF16)
    y = jnp.dot(h, w_ref[...], preferred_element_type=F32)
    qkv_ref[...] = y[:, 0:768]
    z_ref[...] = y[:, 768:1024]
    xy_ref[...] = y[:, 1792:2304]
    ba_ref[...] = lax.dot_general(wba_ref[...], h, (((1,), (1,)), ((), ())), preferred_element_type=F32)

    aq = y[:, 1024:1536]
    ak = y[:, 1536:1664]
    v = y[:, 1664:1792]
    bd = bd_ref[...]
    inv_hd = 1.0 / HEAD_DIM
    qn = (aq * lax.rsqrt(_mm01(aq * aq, bd) * inv_hd + EPS)) * qw_ref[...]
    kn = (ak * lax.rsqrt(_mm01(ak * ak, bd[0:128, 0:128]) * inv_hd + EPS)) * kw_ref[...]
    if emit_kv:
        kf_ref, vf_ref = refs
        kf_ref[...] = kn
        vf_ref[...] = v
    if rope:
        rc = rc_ref[...]
        rs = rs_ref[...]
        kn = kn * rc + _swap16(kn) * rs
        qn = qn * jnp.concatenate([rc] * 4, axis=1) + _swap16(qn) * jnp.concatenate([rs] * 4, axis=1)
    qs = qn * (HEAD_DIM ** -0.5 * math.log2(math.e))
    for hh in range(ATT_Q_HEADS):
        q_ref[hh] = qs[:, hh * HEAD_DIM:(hh + 1) * HEAD_DIM].astype(BF16)
    kT = kn.T
    kT_ref[0] = kT[0:HEAD_DIM, :].astype(BF16)
    kT_ref[1] = kT[HEAD_DIM:2 * HEAD_DIM, :].astype(BF16)
    lane = lax.broadcasted_iota(jnp.int32, v.shape, 1)
    tail = jnp.where(lane == HEAD_DIM, 1.0, 0.0)
    v_ref[0] = jnp.where(lane < HEAD_DIM, v, tail).astype(BF16)
    v_ref[1] = jnp.where(lane < HEAD_DIM, pltpu.roll(v, HEAD_DIM, axis=1), tail).astype(BF16)


def _inproj_call(x, mod, norm_w, w_main, w_ba, qw, kw, bd512, rope_tabs, *, layer, mod_off, emit_kv, tm=512):
    B, T, D = x.shape
    tm = min(tm, T)
    NW = w_main.shape[2]
    rope = rope_tabs is not None
    im = lambda b, t: (b, t, 0)
    in_specs = [pl.BlockSpec((None, tm, D), im),
                pl.BlockSpec((None, None, N_MOD, D), lambda b, t: (layer, b + mod_off, 0, 0)),
                pl.BlockSpec((None, 3, D), lambda b, t: (layer, 0, 0)),
                pl.BlockSpec((None, D, NW), lambda b, t: (layer, 0, 0)),
                pl.BlockSpec((None, 16, D), lambda b, t: (layer, 0, 0)),
                pl.BlockSpec((None, 1, 512), lambda b, t: (layer, 0, 0)),
                pl.BlockSpec((None, 1, 128), lambda b, t: (layer, 0, 0)),
                pl.BlockSpec((512, 512), lambda b, t: (0, 0))]
    args = [x, mod, norm_w, w_main, w_ba, qw, kw, bd512]
    if rope:
        in_specs += [pl.BlockSpec((tm, 128), lambda b, t: (t, 0))] * 2
        args += list(rope_tabs)
    out_shape = [jax.ShapeDtypeStruct((B, T, 768), F32), jax.ShapeDtypeStruct((B, T, 256), F32),
                 jax.ShapeDtypeStruct((B, T, 512), F32), jax.ShapeDtypeStruct((B, ATT_Q_HEADS, T, HEAD_DIM), BF16),
                 jax.ShapeDtypeStruct((B, ATT_KV_HEADS, HEAD_DIM, T), BF16),
                 jax.ShapeDtypeStruct((B, ATT_KV_HEADS, T, 128), BF16),
                 jax.ShapeDtypeStruct((B, 16, T), F32)]
    out_specs = [pl.BlockSpec((None, tm, 768), im), pl.BlockSpec((None, tm, 256), im), pl.BlockSpec((None, tm, 512), im),
                 pl.BlockSpec((None, ATT_Q_HEADS, tm, HEAD_DIM), lambda b, t: (b, 0, t, 0)),
                 pl.BlockSpec((None, ATT_KV_HEADS, HEAD_DIM, tm), lambda b, t: (b, 0, 0, t)),
                 pl.BlockSpec((None, ATT_KV_HEADS, tm, 128), lambda b, t: (b, 0, t, 0)),
                 pl.BlockSpec((None, 16, tm), lambda b, t: (b, 0, t))]
    if emit_kv:
        out_shape += [jax.ShapeDtypeStruct((B, T, 128), F32)] * 2
        out_specs += [pl.BlockSpec((None, tm, 128), im)] * 2
    return pl.pallas_call(
        functools.partial(_inproj_kernel, rope=rope, emit_kv=emit_kv),
        grid=(B, T // tm),
        in_specs=in_specs, out_specs=out_specs, out_shape=out_shape,
        compiler_params=_params("parallel", "parallel"),
        name="inproj",
    )(*args)


def _conv_window(main, prev, nxt, at_start, at_end):
    prev = jnp.where(at_start, 0.0, prev)
    nxt = jnp.where(at_end, 0.0, nxt)
    return jnp.concatenate([prev, main, nxt], axis=0)


def _conv4(win, w, n):
    return (w[0:1, :] * win[6:6 + n] + w[1:2, :] * win[7:7 + n]) + (w[2:3, :] * win[8:8 + n] + w[3:4, :] * win[9:9 + n])


def _dnchunk_kernel(q_ref, qp_ref, qn_ref, k_ref, kp_ref, kn_ref, v_ref, vp_ref, vn_ref, cwq_ref, cwk_ref, cwv_ref,
                    ba_ref, prm_ref, p_ref, qm_ref, o1_ref, o2_ref, gs_ref, *, G):
    C = DN_CHUNK
    W = 2 * C
    r, c, same = _seg_masks(W, C)
    i_loc = r % C
    j_loc = c % C
    m01 = lambda mask: jnp.where(mask, 1.0, 0.0).astype(BF16)
    cum_f = m01(same & (i_loc <= j_loc))
    cum_b = m01(same & (i_loc >= j_loc))
    ones_bd = m01(same)
    eye = m01(r == c)
    eye_f = jnp.where(r == c, 1.0, 0.0)
    blk = tuple((r // s) == (c // s) for s in (8, 16, 32, 64))
    incl = (same & (i_loc >= j_loc), same & (i_loc <= j_loc))
    strict = (same & (i_loc > j_loc), same & (i_loc < j_loc))
    head0 = lax.broadcasted_iota(jnp.int32, (C, W), 1) < C
    row2 = lax.broadcasted_iota(jnp.int32, (2, W), 0)
    alog = prm_ref[0:2, :]
    dtb = prm_ref[2:4, :]

    jblk = pl.program_id(2)
    at_start = jblk == 0
    at_end = jblk == pl.num_programs(2) - 1

    def conv_silu(main_ref, prev_ref, next_ref, cw_ref):
        win = _conv_window(main_ref[...], prev_ref[...], next_ref[...], at_start, at_end)
        return _silu(_conv4(win, cw_ref[...], G * C))

    def l2norm(x):
        return x * lax.rsqrt(_mm01(x * x, ones_bd) + EPS)

    q_all = l2norm(conv_silu(q_ref, qp_ref, qn_ref, cwq_ref)) * (DN_DK ** -0.5)
    k_all = l2norm(conv_silu(k_ref, kp_ref, kn_ref, cwk_ref))
    v_all = conv_silu(v_ref, vp_ref, vn_ref, cwv_ref)

    def stack(x):
        return jnp.concatenate([jnp.where(head0, x, 0.0), jnp.where(head0, 0.0, x)], axis=0)

    chunks = range(G)
    bg = [ba_ref[gi] for gi in chunks]
    beta = [jax.nn.sigmoid(b[0:2, :]) for b in bg]
    g = [-jnp.exp(alog) * _softplus(b[2:4, :] + dtb) for b in bg]
    gcf = [_mm01(x, cum_f) for x in g]
    gcb = [_mm01(x, cum_b) for x in g]
    tot = [_mm01(x, ones_bd) for x in g]
    rows = [jnp.concatenate([beta[i], jnp.where(row2 == 0, gcf[i], gcb[i]), tot[i], jnp.zeros((2, W), F32)], axis=0)
            for i in chunks]
    cols = [_mm01_nt(eye, x) for x in rows]
    Kst = [stack(k_all[gi * C:(gi + 1) * C, :]) for gi in chunks]
    Qst = [stack(q_all[gi * C:(gi + 1) * C, :]) for gi in chunks]
    Vst = [stack(v_all[gi * C:(gi + 1) * C, :]) for gi in chunks]
    kq = [_mm_nt(jnp.concatenate([Kst[i], Qst[i]], axis=0), Kst[i]) for i in chunks]

    chains = [(gi, d) for gi in chunks for d in range(2)]
    col = lambda gi, k: cols[gi][:, k:k + 1]
    dec = [jnp.exp(jnp.where(incl[d], col(gi, 2 + d) - rows[gi][2 + d:3 + d, :], -jnp.inf)) for gi, d in chains]
    L = [(col(gi, d) * kq[gi][0:W]) * jnp.where(strict[d], dec[i], 0.0) for i, (gi, d) in enumerate(chains)]
    QK = [kq[gi][W:2 * W] * dec[i] for i, (gi, d) in enumerate(chains)]
    dot = functools.partial(jnp.dot, preferred_element_type=F32)
    D8f = [jnp.where(blk[0], x, 0.0) for x in L]
    D8 = [x.astype(BF16) for x in D8f]
    M = [dot(d8, d8) for d8 in D8]
    Mb = [m.astype(BF16) for m in M]
    DM = [dot(d8, mb) for d8, mb in zip(D8, Mb)]
    R = [(m - d8) - dm for m, d8, dm in zip(M, D8f, DM)]
    M = [dot(mb, mb) for mb in Mb]
    RM = [dot(r_.astype(BF16), m.astype(BF16)) for r_, m in zip(R, M)]
    Tm = [eye_f + ((r_ + m) + rm) for r_, m, rm in zip(R, M, RM)]
    for lvl in range(3):
        off = blk[lvl + 1] & jnp.logical_not(blk[lvl])
        Th = [t.astype(BF16) for t in Tm]
        Bm = [jnp.where(off, x, 0.0) for x in L]
        Bh = [b.astype(BF16) for b in Bm]
        Bl = [(b - h.astype(F32)).astype(BF16) for b, h in zip(Bm, Bh)]
        TB2 = [dot(th, jnp.concatenate([bh, bl], axis=1)) for th, bh, bl in zip(Th, Bh, Bl)]
        TB = [x[:, 0:W] + x[:, W:2 * W] for x in TB2]
        TBh = [x.astype(BF16) for x in TB]
        TBl = [(x - h.astype(F32)).astype(BF16) for x, h in zip(TB, TBh)]
        TBT2 = [dot(jnp.concatenate([h, lo], axis=0), th) for h, lo, th in zip(TBh, TBl, Th)]
        Tm = [t - (x[0:W] + x[W:2 * W]) for t, x in zip(Tm, TBT2)]
    rhs = [jnp.concatenate([(col(gi, d) * jnp.exp(col(gi, 2 + d))) * Kst[gi], col(gi, d) * Vst[gi]], axis=1)
           for gi, d in chains]
    X = _mm3_many(Tm, rhs)
    kd = [Kst[gi] * jnp.exp(col(gi, 4 + d) - col(gi, 2 + d)) for gi, d in chains]
    PQ = [_mm_tn(a, x) for a, x in zip(kd, X)]
    OO = [_mm(a, x) for a, x in zip(QK, X)]
    for i, (gi, d) in enumerate(chains):
        o1 = Qst[gi] * jnp.exp(col(gi, 2 + d)) - OO[i][:, 0:W]
        o2 = OO[i][:, W:2 * W]
        p_ref[d, gi] = PQ[i][:, 0:W].astype(BF16)
        qm_ref[d, gi] = PQ[i][:, W:2 * W]
        o1_ref[d, gi] = (o1[0:C] + o1[C:W]).astype(BF16)
        o2_ref[d, gi] = o2[0:C] + o2[C:W]
        gs_ref[d, gi] = jnp.exp(rows[gi][4 + d:5 + d, :])


def _dnchunk_call(qkv, conv_w, ba, prm, *, layer, G=4):
    B, T, _ = qkv.shape
    C = DN_CHUNK
    W = 2 * C
    n = T // C
    G = min(G, n)
    rb = G * C // 8
    nb8 = T // 8
    sds = jax.ShapeDtypeStruct
    mat = lambda rows, dt: (sds((B, 2, 2, n, rows, W), dt),
                            pl.BlockSpec((None, None, 2, G, rows, W), lambda b, p, j: (b, p, 0, j, 0, 0)))
    outs = [mat(W, BF16), mat(W, F32), mat(C, BF16), mat(C, F32), mat(1, F32)]
    in_specs = []
    for part in range(3):
        in_specs += [pl.BlockSpec((None, G * C, W), lambda b, p, j, part=part: (b, j, 2 * part + p)),
                     pl.BlockSpec((None, 8, W), lambda b, p, j, part=part: (b, jnp.maximum(j * rb - 1, 0), 2 * part + p)),
                     pl.BlockSpec((None, 8, W),
                                  lambda b, p, j, part=part: (b, jnp.minimum((j + 1) * rb, nb8 - 1), 2 * part + p))]
    in_specs += [pl.BlockSpec((None, 4, W), lambda b, p, j, part=part: (layer, 0, 2 * part + p)) for part in range(3)]
    in_specs += [pl.BlockSpec((None, None, G, 4, W), lambda b, p, j: (b, p, j, 0, 0)),
                 pl.BlockSpec((None, None, 4, W), lambda b, p, j: (layer, p, 0, 0))]
    return pl.pallas_call(
        functools.partial(_dnchunk_kernel, G=G),
        grid=(B, 2, n // G),
        in_specs=in_specs,
        out_specs=[o[1] for o in outs],
        out_shape=[o[0] for o in outs],
        compiler_params=_params("parallel", "parallel", "parallel"),
        name="dnchunk",
    )(*([qkv] * 9), conv_w, conv_w, conv_w, ba, prm)


def _dnscan_kernel(pf_ref, qf_ref, o1f_ref, o2f_ref, gf_ref, pb_ref, qb_ref, o1b_ref, o2b_ref, gb_ref, s0_ref,
                   of_ref, ob_ref, sfin_ref, s_sc, *, Gs):
    C = DN_CHUNK
    W = 2 * C
    j = pl.program_id(1)

    @pl.when(j == 0)
    def _():
        s_sc[...] = s0_ref[...]

    S = [[s_sc[p, d] for d in range(2)] for p in range(2)]
    fwd = (pf_ref, qf_ref, o1f_ref, o2f_ref, gf_ref, of_ref)
    bwd = (pb_ref, qb_ref, o1b_ref, o2b_ref, gb_ref, ob_ref)
    for i in range(Gs):
        for d, (P, Qm, O1, O2, GS, out) in enumerate((fwd, bwd)):
            ci = i if d == 0 else Gs - 1 - i
            for p in range(2):
                Sb = S[p][d].astype(BF16)
                out[ci * C:(ci + 1) * C, p * W:(p + 1) * W] = (
                    jnp.dot(O1[p, ci], Sb, preferred_element_type=F32) + O2[p, ci])
                S[p][d] = (GS[p, ci] * S[p][d] - jnp.dot(P[p, ci], Sb, preferred_element_type=F32)) + Qm[p, ci]
    for p in range(2):
        for d in range(2):
            s_sc[p, d] = S[p][d]

    @pl.when(j == pl.num_programs(1) - 1)
    def _():
        sfin_ref[...] = s_sc[...]


def _dnscan_call(ops, s0, *, T, Gs=8):
    B = s0.shape[0]
    C = DN_CHUNK
    W = 2 * C
    n = T // C
    Gs = min(Gs, n)
    nb = n // Gs
    specs = []
    for d in range(2):
        for a in ops:
            rows = a.shape[4]
            if d == 0:
                specs.append(pl.BlockSpec((None, 2, None, Gs, rows, W), lambda b, j: (b, 0, 0, j, 0, 0)))
            else:
                specs.append(pl.BlockSpec((None, 2, None, Gs, rows, W), lambda b, j: (b, 0, 1, nb - 1 - j, 0, 0)))
    st_spec = pl.BlockSpec((None, 2, 2, W, W), lambda b, j: (b, 0, 0, 0, 0))
    return pl.pallas_call(
        functools.partial(_dnscan_kernel, Gs=Gs),
        grid=(B, nb),
        in_specs=specs + [st_spec],
        out_specs=[pl.BlockSpec((None, Gs * C, 2 * W), lambda b, j: (b, j, 0)),
                   pl.BlockSpec((None, Gs * C, 2 * W), lambda b, j: (b, nb - 1 - j, 0)),
                   st_spec],
        out_shape=[jax.ShapeDtypeStruct((B, T, 2 * W), F32), jax.ShapeDtypeStruct((B, T, 2 * W), F32),
                   jax.ShapeDtypeStruct((B, 2, 2, W, W), F32)],
        scratch_shapes=[pltpu.VMEM((2, 2, W, W), F32)],
        compiler_params=_params("parallel", "arbitrary"),
        name="dnscan",
    )(*ops, *ops, s0)


def _attn_kernel(*refs, seg_blocks, tq, rt):
    q_ref = refs[0]
    nseg = len(seg_blocks)
    e_ref, o_ref = refs[1 + 2 * nseg:3 + 2 * nseg]
    G = ATT_Q_HEADS // ATT_KV_HEADS
    rows = G * tq
    q = q_ref[...].reshape(rows, HEAD_DIM)
    rt = min(rt, rows)
    qs = [q[r * rt:(r + 1) * rt] for r in range(rows // rt)]
    m = None
    acc = None
    for si, (nblk, kb) in enumerate(seg_blocks):
        for j in range(nblk):
            kT = refs[1 + 2 * si][:, j * kb:(j + 1) * kb]
            v = refs[2 + 2 * si][j * kb:(j + 1) * kb, :]
            s_all = jnp.dot(q, kT, preferred_element_type=F32)
            S = [s_all[r * rt:(r + 1) * rt] for r in range(rows // rt)]
            smax = [jnp.max(s, axis=-1, keepdims=True) for s in S]
            if m is None:
                m_new = smax
            else:
                m_new = [jnp.maximum(a, b) for a, b in zip(m, smax)]
                alpha = [jnp.exp2(a - b) for a, b in zip(m, m_new)]
            P = [jnp.exp2((s - mn).astype(BF16)) for s, mn in zip(S, m_new)]
            pv = jnp.dot(jnp.concatenate(P, axis=0), v, preferred_element_type=F32)
            acc = pv if acc is None else jnp.concatenate(alpha, axis=0) * acc + pv
            m = m_new
    o = (acc / acc[:, HEAD_DIM:HEAD_DIM + 1]).astype(BF16)
    out = jnp.dot(o[0:tq], e_ref[0], preferred_element_type=F32)
    for hh in range(1, G):
        out = out + jnp.dot(o[hh * tq:(hh + 1) * tq], e_ref[hh], preferred_element_type=F32)
    o_ref[...] = out.astype(BF16)


def _attn_call(q, segs, place, *, grid, q_map, seg_maps, out_map, out_rows, tq, kb, rt=512):
    G = ATT_Q_HEADS // ATT_KV_HEADS
    in_specs = [pl.BlockSpec((None, G, tq, HEAD_DIM), q_map)]
    args = [q]
    seg_blocks = []
    for (kT, v, S), (k_map, v_map) in zip(segs, seg_maps):
        blk = min(kb, S)
        seg_blocks.append((S // blk, blk))
        in_specs.append(pl.BlockSpec((None,) * (kT.ndim - 2) + (HEAD_DIM, S), k_map))
        in_specs.append(pl.BlockSpec((None,) * (v.ndim - 2) + (S, 2 * HEAD_DIM), v_map))
        args += [kT, v]
    in_specs.append(pl.BlockSpec((G, 2 * HEAD_DIM, G * HEAD_DIM), lambda b, g, t: (0, 0, 0)))
    args.append(place)
    return pl.pallas_call(
        functools.partial(_attn_kernel, seg_blocks=tuple(seg_blocks), tq=tq, rt=rt),
        grid=grid,
        in_specs=in_specs,
        out_specs=pl.BlockSpec((None, tq, G * HEAD_DIM), out_map),
        out_shape=jax.ShapeDtypeStruct(out_rows + (ATT_Q_HEADS * HEAD_DIM,), BF16),
        compiler_params=_params("parallel", "parallel", "parallel"),
        name="attn",
    )(*args)


def _lru_kernel(xy_ref, cw_ref, cb_ref, wr_ref, br_ref, wi_ref, bi_ref, lam_ref, h0_ref, o_ref, hfin_ref,
                hf_sc, hb_sc, a_sc, u_sc, *, T, tt):
    W = LRU_WIDTH
    nt = T // tt
    cw = cw_ref[...]
    cb = cb_ref[...]
    rid = lax.broadcasted_iota(jnp.int32, (8, W), 0)

    def gates(i, d):
        st = pl.multiple_of(i * tt, tt)
        main = xy_ref[pl.ds(st, tt), 0:W]
        prev = xy_ref[pl.ds(pl.multiple_of(jnp.maximum(st - 8, 0), 8), 8), 0:W]
        nxt = xy_ref[pl.ds(pl.multiple_of(jnp.minimum(st + tt, T - 8), 8), 8), 0:W]
        x = _conv4(_conv_window(main, prev, nxt, i == 0, i == nt - 1), cw, tt) + cb
        rg = jax.nn.sigmoid(_mm(x, wr_ref[d]) + br_ref[d:d + 1, :])
        ig = jax.nn.sigmoid(_mm(x, wi_ref[d]) + bi_ref[d:d + 1, :])
        log_a = (-LRU_C * rg) * _softplus(-lam_ref[d:d + 1, :])
        a = jnp.exp(log_a)
        a_sc[...] = a
        u_sc[...] = jnp.sqrt(-jnp.tanh(log_a) * (a * a + 1.0)) * (ig * x)
        return st

    def scan8(j8, h, dst_ref, base, reverse):
        r0 = pl.multiple_of(j8 * 8, 8)
        a8 = a_sc[pl.ds(r0, 8), :]
        u8 = u_sc[pl.ds(r0, 8), :]
        out = jnp.zeros((8, W), F32)
        order = range(7, -1, -1) if reverse else range(8)
        for j in order:
            h = jnp.broadcast_to(a8[j:j + 1, :], (8, W)) * h + jnp.broadcast_to(u8[j:j + 1, :], (8, W))
            out = jnp.where(rid == j, h, out)
        dst_ref[pl.ds(pl.multiple_of(base + r0, 8), 8), :] = out
        return h

    def fwd_tile(i, h):
        st = gates(i, 0)
        return lax.fori_loop(0, tt // 8, lambda j8, hh: scan8(j8, hh, hf_sc, st, False), h)

    h = lax.fori_loop(0, nt, fwd_tile, jnp.broadcast_to(h0_ref[0:1, :], (8, W)))
    hfin_ref[0:1, :] = h[0:1, :]

    def bwd_tile(ii, h):
        i = nt - 1 - ii
        st = gates(i, 1)
        h = lax.fori_loop(0, tt // 8, lambda jj, hh: scan8(tt // 8 - 1 - jj, hh, hb_sc, 0, True), h)
        y = xy_ref[pl.ds(st, tt), W:2 * W]
        o_ref[pl.ds(st, tt), :] = _gelu_tanh(y) * (hf_sc[pl.ds(st, tt), :] + hb_sc[...])
        return h

    h = lax.fori_loop(0, nt, bwd_tile, jnp.broadcast_to(h0_ref[1:2, :], (8, W)))
    hfin_ref[1:2, :] = h[0:1, :]


def _lru_call(xy, conv_w, conv_b, wr, br, wi, bi, lam, h0, *, layer, h0_layer):
    B, T, _ = xy.shape
    W = LRU_WIDTH
    tt = min(256, T)
    lmap = lambda b: (layer, 0, 0)
    h0_spec = (pl.BlockSpec((None, 2, W), lambda b: (b, 0, 0)) if h0_layer is None
               else pl.BlockSpec((None, None, 2, W), lambda b: (b, h0_layer, 0, 0)))
    return pl.pallas_call(
        functools.partial(_lru_kernel, T=T, tt=tt),
        grid=(B,),
        in_specs=[pl.BlockSpec((None, T, 2 * W), lambda b: (b, 0, 0)),
                  pl.BlockSpec((None, 4, W), lmap),
                  pl.BlockSpec((None, 1, W), lmap),
                  pl.BlockSpec((None, 2, W, W), lambda b: (layer, 0, 0, 0)),
                  pl.BlockSpec((None, 2, W), lmap),
                  pl.BlockSpec((None, 2, W, W), lambda b: (layer, 0, 0, 0)),
                  pl.BlockSpec((None, 2, W), lmap),
                  pl.BlockSpec((None, 2, W), lmap),
                  h0_spec],
        out_specs=[pl.BlockSpec((None, T, W), lambda b: (b, 0, 0)),
                   pl.BlockSpec((None, 2, W), lambda b: (b, 0, 0))],
        out_shape=[jax.ShapeDtypeStruct((B, T, W), F32), jax.ShapeDtypeStruct((B, 2, W), F32)],
        scratch_shapes=[pltpu.VMEM((T, W), F32), pltpu.VMEM((tt, W), F32),
                        pltpu.VMEM((tt, W), F32), pltpu.VMEM((tt, W), F32)],
        compiler_params=_params("parallel"),
        name="lru",
    )(xy, conv_w, conv_b, wr, br, wi, bi, lam, h0)


def _merge_kernel(x_ref, mod_ref, nw_ref, of_ref, ob_ref, onw_ref, bd_ref, z_ref, oatt_ref, olru_ref,
                  wg_ref, wpa_ref, wpb_ref, wpc_ref, wo_ref, o_ref):
    D = x_ref.shape[-1]
    x = x_ref[...]
    h = _norm_mod(x, nw_ref[1:2, :], mod_ref[4:5, :], mod_ref[3:4, :]).astype(BF16)
    gates = jax.nn.sigmoid(jnp.dot(h, wg_ref[...], preferred_element_type=F32))
    odn = of_ref[...] + ob_ref[...]
    ms = _mm01(odn * odn, bd_ref[...]) * (1.0 / DN_DK)
    odn = (odn * lax.rsqrt(ms + EPS)) * onw_ref[...]
    a = _mm(odn * _silu(z_ref[...]), wpa_ref[...])
    b = jnp.dot(oatt_ref[...], wpb_ref[...], preferred_element_type=F32)
    c = _mm(olru_ref[...], wpc_ref[...])
    merged = (gates[:, 0:D] * a + gates[:, D:2 * D] * b) + gates[:, 2 * D:3 * D] * c
    o_ref[...] = x + mod_ref[5:6, :] * _mm(merged, wo_ref[...])


def _merge_call(x, mod, norm_w, o_f, o_b, onw, bd256, z, o_att, o_lru, w_gate, w_pa, w_pb, w_pc, w_o, *,
                layer, mod_off, tm=256):
    B, T, D = x.shape
    tm = min(tm, T)
    im = lambda b, t: (b, t, 0)
    wspec = lambda w: pl.BlockSpec((None,) + w.shape[1:], lambda b, t: (layer, 0, 0))
    return pl.pallas_call(
        _merge_kernel,
        grid=(B, T // tm),
        in_specs=[pl.BlockSpec((None, tm, D), im),
                  pl.BlockSpec((None, None, N_MOD, D), lambda b, t: (layer, b + mod_off, 0, 0)),
                  pl.BlockSpec((None, 3, D), lambda b, t: (layer, 0, 0)),
                  pl.BlockSpec((None, tm, 256), im), pl.BlockSpec((None, tm, 256), im),
                  pl.BlockSpec((None, 1, 256), lambda b, t: (layer, 0, 0)),
                  pl.BlockSpec((256, 256), lambda b, t: (0, 0)),
                  pl.BlockSpec((None, tm, 256), im),
                  pl.BlockSpec((None, tm, 512), im), pl.BlockSpec((None, tm, 256), im),
                  wspec(w_gate), wspec(w_pa), wspec(w_pb), wspec(w_pc), wspec(w_o)],
        out_specs=pl.BlockSpec((None, tm, D), im),
        out_shape=jax.ShapeDtypeStruct((B, T, D), F32),
        compiler_params=_params("parallel", "parallel"),
        name="merge",
    )(x, mod, norm_w, o_f, o_b, onw, bd256, z, o_att, o_lru, w_gate, w_pa, w_pb, w_pc, w_o)


def _blockdiag_ones(n, seg):
    i = np.arange(n)
    return jnp.asarray((i[:, None] // seg) == (i[None, :] // seg), BF16)


def _placement():
    G = ATT_Q_HEADS // ATT_KV_HEADS
    e = np.zeros((G, 2 * HEAD_DIM, G * HEAD_DIM), np.float32)
    d = np.arange(HEAD_DIM)
    for hh in range(G):
        e[hh, d, hh * HEAD_DIM + d] = 1.0
    return jnp.asarray(e, BF16)


def _with_ones_column(v):
    one = jnp.ones(v.shape[:-1] + (1,), v.dtype)
    zero = jnp.zeros(v.shape[:-1] + (HEAD_DIM - 1,), v.dtype)
    return jnp.concatenate([v, one, zero], axis=-1).astype(BF16)


def _rope_tables(n_tokens):
    rows = n_tokens // GRID_W
    row = jnp.broadcast_to(jnp.arange(rows, dtype=F32)[:, None], (rows, GRID_W)).reshape(-1)
    col = jnp.broadcast_to(jnp.arange(GRID_W, dtype=F32)[None, :], (rows, GRID_W)).reshape(-1)
    freqs = ROPE_BASE ** (-jnp.arange(ROPE_PAIRS, dtype=F32) / ROPE_PAIRS)
    ang = jnp.stack([row[:, None] * freqs, col[:, None] * freqs], axis=1)
    cos = jnp.cos(ang)[:, :, None, :]
    sin = jnp.sin(ang)[:, :, None, :]
    c = jnp.broadcast_to(cos, (n_tokens, 2, 2, ROPE_PAIRS)).reshape(n_tokens, HEAD_DIM)
    s = jnp.concatenate([-sin, sin], axis=2).reshape(n_tokens, HEAD_DIM)
    return jnp.concatenate([c, c], axis=1), jnp.concatenate([s, s], axis=1)


def _ba_layout(ba, n):
    B = ba.shape[0]
    x = ba.reshape(B, 2, 2, 2, 2, n, DN_CHUNK)
    x = x.transpose(0, 3, 5, 1, 2, 4, 6)
    return x.reshape(B, 2, n, 4, 2 * DN_CHUNK)


def _state_to_blockdiag(s):
    B = s.shape[0]
    x = s.reshape(B, 2, 2, 2, DN_DK, DN_DK)
    z = jnp.zeros_like(x[:, :, :, 0])
    top = jnp.concatenate([x[:, :, :, 0], z], axis=-1)
    bot = jnp.concatenate([z, x[:, :, :, 1]], axis=-1)
    return jnp.concatenate([top, bot], axis=-2).transpose(0, 2, 1, 3, 4)


def _blockdiag_to_state(sb):
    B = sb.shape[0]
    x = sb.transpose(0, 2, 1, 3, 4)
    h0 = x[..., 0:DN_DK, 0:DN_DK]
    h1 = x[..., DN_DK:, DN_DK:]
    return jnp.stack([h0, h1], axis=3).reshape(B, 2, DN_HEADS, DN_DK, DN_DK)


def _lru_blockdiag(w):
    L = w.shape[0]
    bw = LRU_WIDTH // LRU_BLOCKS
    out = jnp.zeros((L, 2, LRU_WIDTH, LRU_WIDTH), w.dtype)
    for n in range(LRU_BLOCKS):
        out = out.at[:, :, n * bw:(n + 1) * bw, n * bw:(n + 1) * bw].set(w[:, :, n])
    return out


def kernel(x_prompt, x_sample, cache_k, cache_v, state_delta, state_lru, c, c_ctx, w_mod, b_mod, norm_w, ffn1_wgu,
           ffn1_wd, ffn2_wgu, ffn2_wd, w_in, dn_conv_w, dn_a_log, dn_dt_bias, dn_onorm_w, att_qnorm_w, att_knorm_w,
           lru_conv_w, lru_conv_b, lru_wr, lru_br, lru_wi, lru_bi, lru_lam, w_pa, w_pb, w_pc, w_o):
    NB, SEQ, D = x_prompt.shape
    DB, DSEQ, _ = x_sample.shape
    L = w_mod.shape[0]
    PAST = cache_k.shape[2]
    TC = NB * SEQ

    cond = jnp.zeros((16, D), F32).at[0].set(c_ctx).at[1:1 + DB].set(c)
    mod = _mod_call(cond, w_mod, b_mod).reshape(L, 16, N_MOD, D)

    bf = lambda w: w.astype(BF16)
    ffn1_wgu, ffn1_wd, ffn2_wgu, ffn2_wd = bf(ffn1_wgu), bf(ffn1_wd), bf(ffn2_wgu), bf(ffn2_wd)
    w_main = bf(jnp.concatenate([w_in[:, :, 0:1024], w_in[:, :, 1040:2320]], axis=-1))
    w_ba = bf(jnp.swapaxes(w_in[:, :, 1024:1040], 1, 2))
    w_gate = bf(w_in[:, :, 2320:])
    w_pa, w_pb, w_pc, w_o = bf(w_pa), bf(w_pb), bf(w_pc), bf(w_o)
    qw = jnp.tile(att_qnorm_w, (1, ATT_Q_HEADS)).reshape(L, 1, ATT_Q_HEADS * HEAD_DIM)
    kw = jnp.tile(att_knorm_w, (1, ATT_KV_HEADS)).reshape(L, 1, ATT_KV_HEADS * HEAD_DIM)
    onw = jnp.tile(dn_onorm_w, (1, DN_HEADS)).reshape(L, 1, DN_HEADS * DN_DK)
    pr = lambda p: jnp.repeat(p.reshape(L, 2, 2, 2), DN_CHUNK, axis=-1).reshape(L, 2, 2, 2 * DN_CHUNK).transpose(0, 2, 1, 3)
    dn_prm = jnp.concatenate([pr(dn_a_log), pr(dn_dt_bias)], axis=2)
    wr_bd, wi_bd = bf(_lru_blockdiag(lru_wr)), bf(_lru_blockdiag(lru_wi))
    lru_cb = lru_conv_b.reshape(L, 1, LRU_WIDTH)
    bd512 = _blockdiag_ones(512, HEAD_DIM)
    bd256 = _blockdiag_ones(256, DN_DK)
    place = _placement()
    rope_tabs = _rope_tables(DSEQ)
    cache_kT = bf(cache_k.transpose(0, 1, 3, 4, 2))
    cache_v1 = _with_ones_column(cache_v.transpose(0, 1, 3, 2, 4))
    s0_lat = _state_to_blockdiag(state_delta.transpose(1, 0, 2, 3, 4, 5).reshape(L * DB, 2, DN_HEADS, DN_DK, DN_DK))
    s0_lat = s0_lat.reshape(L, DB, 2, 2, 2 * DN_DK, 2 * DN_DK)
    s0_ctx = jnp.zeros((NB, 2, 2, 2 * DN_DK, 2 * DN_DK), F32)
    h0_ctx = jnp.zeros((NB, 2, LRU_WIDTH), F32)

    xp = x_prompt.reshape(1, TC, D)
    xs = x_sample
    new_k, new_v, new_sd, new_sl = [], [], [], []
    for l in range(L):
        for ctx in (True, False):
            x = xp if ctx else xs
            mod_off = 0 if ctx else 1
            seq = SEQ if ctx else DSEQ
            nseq = NB if ctx else DB
            shp = x.shape[:2]
            x = _ffn_call(x, mod, norm_w, ffn1_wgu, ffn1_wd, layer=l, sub=0, mod_off=mod_off)
            outs = _inproj_call(x, mod, norm_w, w_main, w_ba, qw, kw, bd512, None if ctx else rope_tabs,
                                layer=l, mod_off=mod_off, emit_kv=ctx)
            qkv, z, xy, q_hm, kT, v_bf, ba = outs[:7]
            n = seq // DN_CHUNK
            ba_l = _ba_layout(ba.reshape(shp[0], 16, -1, seq).transpose(0, 2, 1, 3).reshape(nseq, 16, seq), n)
            ops = _dnchunk_call(qkv.reshape(nseq, seq, 768), dn_conv_w, ba_l, dn_prm, layer=l)
            o_f, o_b, s_fin = _dnscan_call(ops, s0_ctx if ctx else s0_lat[l], T=seq)
            if ctx:
                o_att = _attn_call(
                    q_hm, [(kT, v_bf, SEQ)], place, grid=(NB, ATT_KV_HEADS, 1),
                    q_map=lambda s, g, t: (0, g, s, 0),
                    seg_maps=[(lambda s, g, t: (0, g, 0, s), lambda s, g, t: (0, g, s, 0))],
                    out_map=lambda s, g, t: (0, s, g), out_rows=(1, TC), tq=SEQ, kb=512)
            else:
                tq = 128
                o_att = _attn_call(
                    q_hm, [(cache_kT, cache_v1, PAST), (kT, v_bf, DSEQ)], place, grid=(DB, ATT_KV_HEADS, DSEQ // tq),
                    q_map=lambda b, g, t: (b, g, t, 0),
                    seg_maps=[(lambda b, g, t: (b, l, g, 0, 0), lambda b, g, t: (b, l, g, 0, 0)),
                              (lambda b, g, t: (b, g, 0, 0), lambda b, g, t: (b, g, 0, 0))],
                    out_map=lambda b, g, t: (b, t, g), out_rows=(DB, DSEQ), tq=tq, kb=512)
            if ctx:
                o_lru, h_fin = _lru_call(xy.reshape(nseq, seq, 512), lru_conv_w, lru_cb, wr_bd, lru_br, wi_bd, lru_bi,
                                         lru_lam, h0_ctx, layer=l, h0_layer=None)
            else:
                o_lru, h_fin = _lru_call(xy, lru_conv_w, lru_cb, wr_bd, lru_br, wi_bd, lru_bi, lru_lam, state_lru,
                                         layer=l, h0_layer=l)
            x = _merge_call(x, mod, norm_w, o_f.reshape(shp + (256,)), o_b.reshape(shp + (256,)), onw, bd256, z, o_att,
                            o_lru.reshape(shp + (256,)), w_gate, w_pa, w_pb, w_pc, w_o, layer=l, mod_off=mod_off)
            x = _ffn_call(x, mod, norm_w, ffn2_wgu, ffn2_wd, layer=l, sub=2, mod_off=mod_off)
            if ctx:
                xp = x
                kf, vf = outs[7:9]
                new_k.append(kf.reshape(NB, SEQ, ATT_KV_HEADS, HEAD_DIM))
                new_v.append(vf.reshape(NB, SEQ, ATT_KV_HEADS, HEAD_DIM))
                new_sd.append(_blockdiag_to_state(s_fin))
                new_sl.append(h_fin)
            else:
                xs = x
    return (xp.reshape(NB, SEQ, D), xs, jnp.stack(new_k, axis=1), jnp.stack(new_v, axis=1),
            jnp.stack(new_sd, axis=1), jnp.stack(new_sl, axis=1))
```

```python
import functools
import math

import numpy as np
import jax
import jax.numpy as jnp
from jax import lax
from jax.experimental import pallas as pl
from jax.experimental.pallas import tpu as pltpu

F32 = jnp.float32
BF16 = jnp.bfloat16

EPS = 1e-6
GRID_W = 64
HEAD_DIM = 64
ATT_Q_HEADS = 8
ATT_KV_HEADS = 2
ROPE_BASE = 10000.0
ROPE_PAIRS = HEAD_DIM // 4
DN_HEADS = 4
DN_DK = 64
DN_CHUNK = 64
LRU_WIDTH = 256
LRU_BLOCKS = 4
LRU_C = 8.0
N_MOD = 9
LANES = 128
V7X_VMEM_LIMIT_BYTES = 56 * 1024 * 1024


def _params(*sem):
    return pltpu.CompilerParams(dimension_semantics=sem, vmem_limit_bytes=V7X_VMEM_LIMIT_BYTES)


def _mm(a, b):
    return jnp.dot(a.astype(BF16), b.astype(BF16), preferred_element_type=F32)


def _mm_nt(a, b):
    return lax.dot_general(a.astype(BF16), b.astype(BF16), (((1,), (1,)), ((), ())), preferred_element_type=F32)


def _mm_tn(a, b):
    return lax.dot_general(a.astype(BF16), b.astype(BF16), (((0,), (0,)), ((), ())), preferred_element_type=F32)


def _split3(x):
    hi = x.astype(BF16)
    r = x - hi.astype(F32)
    mid = r.astype(BF16)
    lo = (r - mid.astype(F32)).astype(BF16)
    return hi, mid, lo


def _mm01(x, m01):
    hi, mid, lo = _split3(x)
    d = functools.partial(jnp.dot, preferred_element_type=F32)
    return d(hi, m01) + d(mid, m01) + d(lo, m01)


def _mm01x2(x, m01):
    hi = x.astype(BF16)
    lo = (x - hi.astype(F32)).astype(BF16)
    return jnp.dot(hi, m01, preferred_element_type=F32) + jnp.dot(lo, m01, preferred_element_type=F32)


def _mm01_nt(m01, x):
    hi, mid, lo = _split3(x)
    d = functools.partial(lax.dot_general, dimension_numbers=(((1,), (1,)), ((), ())), preferred_element_type=F32)
    return d(m01, hi) + d(m01, mid) + d(m01, lo)


def _mm3(a, b):
    n = a.shape[0]
    ah = a.astype(BF16)
    al = (a - ah.astype(F32)).astype(BF16)
    bh = b.astype(BF16)
    bl = (b - bh.astype(F32)).astype(BF16)
    top = jnp.dot(jnp.concatenate([ah, al], axis=0), bh, preferred_element_type=F32)
    return (top[0:n] + top[n:2 * n]) + jnp.dot(ah, bl, preferred_element_type=F32)


def _mm3_many(As, Bs):
    n = As[0].shape[0]
    ah = [a.astype(BF16) for a in As]
    bh = [b.astype(BF16) for b in Bs]
    al = [(a - h.astype(F32)).astype(BF16) for a, h in zip(As, ah)]
    bl = [(b - h.astype(F32)).astype(BF16) for b, h in zip(Bs, bh)]
    top = [jnp.dot(jnp.concatenate([h, l], axis=0), b, preferred_element_type=F32) for h, l, b in zip(ah, al, bh)]
    low = [jnp.dot(h, b, preferred_element_type=F32) for h, b in zip(ah, bl)]
    return [(t[0:n] + t[n:2 * n]) + w for t, w in zip(top, low)]


def _silu(x):
    return x * jax.nn.sigmoid(x)


def _softplus(x):
    return jnp.maximum(x, 0.0) + jnp.log1p(jnp.exp(-jnp.abs(x)))


def _gelu_tanh(x):
    return x * (0.5 * (1.0 + jnp.tanh(0.7978845608028654 * (x + 0.044715 * (x * x * x)))))


def _norm_mod(x, nw, scale, shift):
    ms = jnp.mean(x * x, axis=-1, keepdims=True)
    y = (x * lax.rsqrt(ms + EPS)) * nw
    return y * (1.0 + scale) + shift


def _seg_masks(n, seg):
    r = lax.broadcasted_iota(jnp.int32, (n, n), 0)
    c = lax.broadcasted_iota(jnp.int32, (n, n), 1)
    return r, c, (r // seg) == (c // seg)


def _swap16(x):
    w = x.shape[1]
    lane = lax.broadcasted_iota(jnp.int32, x.shape, 1)
    return jnp.where((lane & 16) == 0, pltpu.roll(x, w - 16, axis=1), pltpu.roll(x, 16, axis=1))


def _mod_kernel(c_ref, w_ref, b_ref, o_ref):
    o_ref[...] = _mm(_silu(c_ref[...]), w_ref[...]) + b_ref[...]


def _mod_call(cond, w_mod, b_mod):
    L, D, N = w_mod.shape
    R = cond.shape[0]
    tn = D
    return pl.pallas_call(
        _mod_kernel,
        grid=(L, N // tn),
        in_specs=[pl.BlockSpec((R, D), lambda l, j: (0, 0)),
                  pl.BlockSpec((None, D, tn), lambda l, j: (l, 0, j)),
                  pl.BlockSpec((None, 1, tn), lambda l, j: (l, 0, j))],
        out_specs=pl.BlockSpec((None, R, tn), lambda l, j: (l, 0, j)),
        out_shape=jax.ShapeDtypeStruct((L, R, N), F32),
        compiler_params=_params("parallel", "parallel"),
        name="mod",
    )(cond, w_mod, b_mod.reshape(L, 1, N))


def _ffn_kernel(x_ref, mod_ref, nw_ref, wgu_ref, wd_ref, o_ref, *, sub, F, nc):
    x = x_ref[...]
    h = _norm_mod(x, nw_ref[sub:sub + 1, :], mod_ref[3 * sub + 1:3 * sub + 2, :],
                  mod_ref[3 * sub:3 * sub + 1, :]).astype(BF16)
    dot = functools.partial(jnp.dot, preferred_element_type=F32)
    cf = F // nc
    gu = [(dot(h, wgu_ref[:, c * cf:(c + 1) * cf]), dot(h, wgu_ref[:, F + c * cf:F + (c + 1) * cf]))
          for c in range(nc)]
    acc = None
    for c, (g, u) in enumerate(gu):
        part = dot((_silu(g) * u).astype(BF16), wd_ref[c * cf:(c + 1) * cf, :])
        acc = part if acc is None else acc + part
    o_ref[...] = x + (0.5 * mod_ref[3 * sub + 2:3 * sub + 3, :]) * acc


def _ffn_call(x, mod, norm_w, w_gu, w_d, *, layer, sub, mod_off, tm=512):
    B, T, D = x.shape
    F = w_d.shape[1]
    tm = min(tm, T)
    nc = 2 if F % (2 * LANES) == 0 else 1
    return pl.pallas_call(
        functools.partial(_ffn_kernel, sub=sub, F=F, nc=nc),
        grid=(B, T // tm),
        in_specs=[pl.BlockSpec((None, tm, D), lambda b, t: (b, t, 0)),
                  pl.BlockSpec((None, None, N_MOD, D), lambda b, t: (layer, b + mod_off, 0, 0)),
                  pl.BlockSpec((None, 3, D), lambda b, t: (layer, 0, 0)),
                  pl.BlockSpec((None, D, 2 * F), lambda b, t: (layer, 0, 0), pipeline_mode=pl.Buffered(1)),
                  pl.BlockSpec((None, F, D), lambda b, t: (layer, 0, 0), pipeline_mode=pl.Buffered(1))],
        out_specs=pl.BlockSpec((None, tm, D), lambda b, t: (b, t, 0)),
        out_shape=jax.ShapeDtypeStruct((B, T, D), F32),
        compiler_params=_params("parallel", "parallel"),
        name="ffn",
    )(x, mod, norm_w, w_gu, w_d)


def _inproj_kernel(*refs, rope, emit_kv):
    (x_ref, mod_ref, nw_ref, w_ref, wba_ref, qw_ref, kw_ref, bd_ref) = refs[:8]
    refs = refs[8:]
    if rope:
        rc_ref, rs_ref = refs[:2]
        refs = refs[2:]
    qkv_ref, z_ref, xy_ref, q_ref, kT_ref, v_ref, ba_ref = refs[:7]
    refs = refs[7:]

    h = _norm_mod(x_ref[...], nw_ref[1:2, :], mod_ref[4:5, :], mod_ref[3:4, :]).astype(BF16)
    y = jnp.dot(h, w_ref[...], preferred_element_type=F32)
    qkv_ref[...] = y[:, 0:768]
    z_ref[...] = y[:, 768:1024]
    xy_ref[...] = y[:, 1792:2304]
    ba_ref[...] = lax.dot_general(wba_ref[...], h, (((1,), (1,)), ((), ())), preferred_element_type=F32)

    aq = y[:, 1024:1536]
    ak = y[:, 1536:1664]
    v = y[:, 1664:1792]
    bd = bd_ref[...]
    inv_hd = 1.0 / HEAD_DIM
    qn = (aq * lax.rsqrt(_mm01x2(aq * aq, bd) * inv_hd + EPS)) * qw_ref[...]
    kn = (ak * lax.rsqrt(_mm01x2(ak * ak, bd[0:128, 0:128]) * inv_hd + EPS)) * kw_ref[...]
    if emit_kv:
        kf_ref, vf_ref = refs
        kf_ref[...] = kn
        vf_ref[...] = v
    if rope:
        rc = rc_ref[...]
        rs = rs_ref[...]
        kn = kn * rc + _swap16(kn) * rs
        qn = qn * jnp.concatenate([rc] * 4, axis=1) + _swap16(qn) * jnp.concatenate([rs] * 4, axis=1)
    qs = qn * (HEAD_DIM ** -0.5 * math.log2(math.e))
    for hh in range(ATT_Q_HEADS):
        q_ref[hh] = qs[:, hh * HEAD_DIM:(hh + 1) * HEAD_DIM].astype(BF16)
    kT = kn.T
    kT_ref[0] = kT[0:HEAD_DIM, :].astype(BF16)
    kT_ref[1] = kT[HEAD_DIM:2 * HEAD_DIM, :].astype(BF16)
    lane = lax.broadcasted_iota(jnp.int32, v.shape, 1)
    tail = jnp.where(lane == HEAD_DIM, 1.0, 0.0)
    v_ref[0] = jnp.where(lane < HEAD_DIM, v, tail).astype(BF16)
    v_ref[1] = jnp.where(lane < HEAD_DIM, pltpu.roll(v, HEAD_DIM, axis=1), tail).astype(BF16)


def _inproj_call(x, mod, norm_w, w_main, w_ba, qw, kw, bd512, rope_tabs, *, layer, mod_off, emit_kv, tm=512):
    B, T, D = x.shape
    tm = min(tm, T)
    NW = w_main.shape[2]
    rope = rope_tabs is not None
    im = lambda b, t: (b, t, 0)
    in_specs = [pl.BlockSpec((None, tm, D), im),
                pl.BlockSpec((None, None, N_MOD, D), lambda b, t: (layer, b + mod_off, 0, 0)),
                pl.BlockSpec((None, 3, D), lambda b, t: (layer, 0, 0)),
                pl.BlockSpec((None, D, NW), lambda b, t: (layer, 0, 0)),
                pl.BlockSpec((None, 16, D), lambda b, t: (layer, 0, 0)),
                pl.BlockSpec((None, 1, 512), lambda b, t: (layer, 0, 0)),
                pl.BlockSpec((None, 1, 128), lambda b, t: (layer, 0, 0)),
                pl.BlockSpec((512, 512), lambda b, t: (0, 0))]
    args = [x, mod, norm_w, w_main, w_ba, qw, kw, bd512]
    if rope:
        in_specs += [pl.BlockSpec((tm, 128), lambda b, t: (t, 0))] * 2
        args += list(rope_tabs)
    out_shape = [jax.ShapeDtypeStruct((B, T, 768), F32), jax.ShapeDtypeStruct((B, T, 256), F32),
                 jax.ShapeDtypeStruct((B, T, 512), F32), jax.ShapeDtypeStruct((B, ATT_Q_HEADS, T, HEAD_DIM), BF16),
                 jax.ShapeDtypeStruct((B, ATT_KV_HEADS, HEAD_DIM, T), BF16),
                 jax.ShapeDtypeStruct((B, ATT_KV_HEADS, T, 128), BF16),
                 jax.ShapeDtypeStruct((B, 16, T), F32)]
    out_specs = [pl.BlockSpec((None, tm, 768), im), pl.BlockSpec((None, tm, 256), im), pl.BlockSpec((None, tm, 512), im),
                 pl.BlockSpec((None, ATT_Q_HEADS, tm, HEAD_DIM), lambda b, t: (b, 0, t, 0)),
                 pl.BlockSpec((None, ATT_KV_HEADS, HEAD_DIM, tm), lambda b, t: (b, 0, 0, t)),
                 pl.BlockSpec((None, ATT_KV_HEADS, tm, 128), lambda b, t: (b, 0, t, 0)),
                 pl.BlockSpec((None, 16, tm), lambda b, t: (b, 0, t))]
    if emit_kv:
        out_shape += [jax.ShapeDtypeStruct((B, T, 128), F32)] * 2
        out_specs += [pl.BlockSpec((None, tm, 128), im)] * 2
    return pl.pallas_call(
        functools.partial(_inproj_kernel, rope=rope, emit_kv=emit_kv),
        grid=(B, T // tm),
        in_specs=in_specs, out_specs=out_specs, out_shape=out_shape,
        compiler_params=_params("parallel", "parallel"),
        name="inproj",
    )(*args)


def _conv_window(main, prev, nxt, at_start, at_end):
    prev = jnp.where(at_start, 0.0, prev)
    nxt = jnp.where(at_end, 0.0, nxt)
    return jnp.concatenate([prev, main, nxt], axis=0)


def _conv4(win, w, n):
    return (w[0:1, :] * win[6:6 + n] + w[1:2, :] * win[7:7 + n]) + (w[2:3, :] * win[8:8 + n] + w[3:4, :] * win[9:9 + n])


def _dnchunk_kernel(q_ref, qp_ref, qn_ref, k_ref, kp_ref, kn_ref, v_ref, vp_ref, vn_ref, cwq_ref, cwk_ref, cwv_ref,
                    ba_ref, prm_ref, p_ref, qm_ref, o1_ref, o2_ref, gs_ref, *, G):
    C = DN_CHUNK
    W = 2 * C
    r, c, same = _seg_masks(W, C)
    i_loc = r % C
    j_loc = c % C
    m01 = lambda mask: jnp.where(mask, 1.0, 0.0).astype(BF16)
    cum_f = m01(same & (i_loc <= j_loc))
    cum_b = m01(same & (i_loc >= j_loc))
    ones_bd = m01(same)
    eye = m01(r == c)
    eye_f = jnp.where(r == c, 1.0, 0.0)
    blk = tuple((r // s) == (c // s) for s in (8, 16, 32, 64))
    incl = (same & (i_loc >= j_loc), same & (i_loc <= j_loc))
    strict = (same & (i_loc > j_loc), same & (i_loc < j_loc))
    head0 = lax.broadcasted_iota(jnp.int32, (C, W), 1) < C
    row2 = lax.broadcasted_iota(jnp.int32, (2, W), 0)
    alog = prm_ref[0:2, :]
    dtb = prm_ref[2:4, :]

    jblk = pl.program_id(2)
    at_start = jblk == 0
    at_end = jblk == pl.num_programs(2) - 1

    def conv_silu(main_ref, prev_ref, next_ref, cw_ref):
        win = _conv_window(main_ref[...], prev_ref[...], next_ref[...], at_start, at_end)
        return _silu(_conv4(win, cw_ref[...], G * C))

    def l2norm(x):
        return x * lax.rsqrt(_mm01(x * x, ones_bd) + EPS)

    q_all = l2norm(conv_silu(q_ref, qp_ref, qn_ref, cwq_ref)) * (DN_DK ** -0.5)
    k_all = l2norm(conv_silu(k_ref, kp_ref, kn_ref, cwk_ref))
    v_all = conv_silu(v_ref, vp_ref, vn_ref, cwv_ref)

    def stack(x):
        return jnp.concatenate([jnp.where(head0, x, 0.0), jnp.where(head0, 0.0, x)], axis=0)

    chunks = range(G)
    bg = [ba_ref[gi] for gi in chunks]
    beta = [jax.nn.sigmoid(b[0:2, :]) for b in bg]
    g = [-jnp.exp(alog) * _softplus(b[2:4, :] + dtb) for b in bg]
    gcf = [_mm01(x, cum_f) for x in g]
    gcb = [_mm01(x, cum_b) for x in g]
    tot = [_mm01(x, ones_bd) for x in g]
    rows = [jnp.concatenate([beta[i], jnp.where(row2 == 0, gcf[i], gcb[i]), tot[i], jnp.zeros((2, W), F32)], axis=0)
            for i in chunks]
    cols = [_mm01_nt(eye, x) for x in rows]
    Kst = [stack(k_all[gi * C:(gi + 1) * C, :]) for gi in chunks]
    Qst = [stack(q_all[gi * C:(gi + 1) * C, :]) for gi in chunks]
    Vst = [stack(v_all[gi * C:(gi + 1) * C, :]) for gi in chunks]
    kq = [_mm_nt(jnp.concatenate([Kst[i], Qst[i]], axis=0), Kst[i]) for i in chunks]

    chains = [(gi, d) for gi in chunks for d in range(2)]
    col = lambda gi, k: cols[gi][:, k:k + 1]
    dec = [jnp.exp(jnp.where(incl[d], col(gi, 2 + d) - rows[gi][2 + d:3 + d, :], -jnp.inf)) for gi, d in chains]
    L = [(col(gi, d) * kq[gi][0:W]) * jnp.where(strict[d], dec[i], 0.0) for i, (gi, d) in enumerate(chains)]
    QK = [kq[gi][W:2 * W] * dec[i] for i, (gi, d) in enumerate(chains)]
    dot = functools.partial(jnp.dot, preferred_element_type=F32)
    D8f = [jnp.where(blk[0], x, 0.0) for x in L]
    D8 = [x.astype(BF16) for x in D8f]
    M = [dot(d8, d8) for d8 in D8]
    Mb = [m.astype(BF16) for m in M]
    DM = [dot(d8, mb) for d8, mb in zip(D8, Mb)]
    R = [(m - d8) - dm for m, d8, dm in zip(M, D8f, DM)]
    M = [dot(mb, mb) for mb in Mb]
    RM = [dot(r_.astype(BF16), m.astype(BF16)) for r_, m in zip(R, M)]
    Tm = [eye_f + ((r_ + m) + rm) for r_, m, rm in zip(R, M, RM)]
    for lvl in range(3):
        off = blk[lvl + 1] & jnp.logical_not(blk[lvl])
        Th = [t.astype(BF16) for t in Tm]
        Bm = [jnp.where(off, x, 0.0) for x in L]
        Bh = [b.astype(BF16) for b in Bm]
        Bl = [(b - h.astype(F32)).astype(BF16) for b, h in zip(Bm, Bh)]
        TB2 = [dot(th, jnp.concatenate([bh, bl], axis=1)) for th, bh, bl in zip(Th, Bh, Bl)]
        TB = [x[:, 0:W] + x[:, W:2 * W] for x in TB2]
        TBh = [x.astype(BF16) for x in TB]
        TBl = [(x - h.astype(F32)).astype(BF16) for x, h in zip(TB, TBh)]
        TBT2 = [dot(jnp.concatenate([h, lo], axis=0), th) for h, lo, th in zip(TBh, TBl, Th)]
        Tm = [t - (x[0:W] + x[W:2 * W]) for t, x in zip(Tm, TBT2)]
    rhs = [jnp.concatenate([(col(gi, d) * jnp.exp(col(gi, 2 + d))) * Kst[gi], col(gi, d) * Vst[gi]], axis=1)
           for gi, d in chains]
    X = _mm3_many(Tm, rhs)
    kd = [Kst[gi] * jnp.exp(col(gi, 4 + d) - col(gi, 2 + d)) for gi, d in chains]
    PQ = [_mm_tn(a, x) for a, x in zip(kd, X)]
    OO = [_mm(a, x) for a, x in zip(QK, X)]
    for i, (gi, d) in enumerate(chains):
        o1 = Qst[gi] * jnp.exp(col(gi, 2 + d)) - OO[i][:, 0:W]
        o2 = OO[i][:, W:2 * W]
        p_ref[d, gi] = PQ[i][:, 0:W].astype(BF16)
        qm_ref[d, gi] = PQ[i][:, W:2 * W]
        o1_ref[d, gi] = (o1[0:C] + o1[C:W]).astype(BF16)
        o2_ref[d, gi] = o2[0:C] + o2[C:W]
        gs_ref[d, gi] = jnp.exp(rows[gi][4 + d:5 + d, :])


def _dnchunk_call(qkv, conv_w, ba, prm, *, layer, G=8):
    B, T, _ = qkv.shape
    C = DN_CHUNK
    W = 2 * C
    n = T // C
    G = min(G, n)
    rb = G * C // 8
    nb8 = T // 8
    sds = jax.ShapeDtypeStruct
    mat = lambda rows, dt: (sds((B, 2, 2, n, rows, W), dt),
                            pl.BlockSpec((None, None, 2, G, rows, W), lambda b, p, j: (b, p, 0, j, 0, 0)))
    outs = [mat(W, BF16), mat(W, F32), mat(C, BF16), mat(C, F32), mat(1, F32)]
    in_specs = []
    for part in range(3):
        in_specs += [pl.BlockSpec((None, G * C, W), lambda b, p, j, part=part: (b, j, 2 * part + p)),
                     pl.BlockSpec((None, 8, W), lambda b, p, j, part=part: (b, jnp.maximum(j * rb - 1, 0), 2 * part + p)),
                     pl.BlockSpec((None, 8, W),
                                  lambda b, p, j, part=part: (b, jnp.minimum((j + 1) * rb, nb8 - 1), 2 * part + p))]
    in_specs += [pl.BlockSpec((None, 4, W), lambda b, p, j, part=part: (layer, 0, 2 * part + p)) for part in range(3)]
    in_specs += [pl.BlockSpec((None, None, G, 4, W), lambda b, p, j: (b, p, j, 0, 0)),
                 pl.BlockSpec((None, None, 4, W), lambda b, p, j: (layer, p, 0, 0))]
    return pl.pallas_call(
        functools.partial(_dnchunk_kernel, G=G),
        grid=(B, 2, n // G),
        in_specs=in_specs,
        out_specs=[o[1] for o in outs],
        out_shape=[o[0] for o in outs],
        compiler_params=_params("parallel", "parallel", "parallel"),
        name="dnchunk",
    )(*([qkv] * 9), conv_w, conv_w, conv_w, ba, prm)


def _dnscan_kernel(pf_ref, qf_ref, o1f_ref, o2f_ref, gf_ref, pb_ref, qb_ref, o1b_ref, o2b_ref, gb_ref, s0_ref,
                   of_ref, ob_ref, sfin_ref, s_sc, *, Gs):
    C = DN_CHUNK
    W = 2 * C
    j = pl.program_id(1)

    @pl.when(j == 0)
    def _():
        s_sc[...] = s0_ref[...]

    S = [[s_sc[p, d] for d in range(2)] for p in range(2)]
    fwd = (pf_ref, qf_ref, o1f_ref, o2f_ref, gf_ref, of_ref)
    bwd = (pb_ref, qb_ref, o1b_ref, o2b_ref, gb_ref, ob_ref)
    for i in range(Gs):
        for d, (P, Qm, O1, O2, GS, out) in enumerate((fwd, bwd)):
            ci = i if d == 0 else Gs - 1 - i
            for p in range(2):
                Sb = S[p][d].astype(BF16)
                out[ci * C:(ci + 1) * C, p * W:(p + 1) * W] = (
                    jnp.dot(O1[p, ci], Sb, preferred_element_type=F32) + O2[p, ci])
                S[p][d] = (GS[p, ci] * S[p][d] - jnp.dot(P[p, ci], Sb, preferred_element_type=F32)) + Qm[p, ci]
    for p in range(2):
        for d in range(2):
            s_sc[p, d] = S[p][d]

    @pl.when(j == pl.num_programs(1) - 1)
    def _():
        sfin_ref[...] = s_sc[...]


def _dnscan_call(ops, s0, *, T, Gs=8):
    B = s0.shape[0]
    C = DN_CHUNK
    W = 2 * C
    n = T // C
    Gs = min(Gs, n)
    nb = n // Gs
    specs = []
    for d in range(2):
        for a in ops:
            rows = a.shape[4]
            if d == 0:
                specs.append(pl.BlockSpec((None, 2, None, Gs, rows, W), lambda b, j: (b, 0, 0, j, 0, 0)))
            else:
                specs.append(pl.BlockSpec((None, 2, None, Gs, rows, W), lambda b, j: (b, 0, 1, nb - 1 - j, 0, 0)))
    st_spec = pl.BlockSpec((None, 2, 2, W, W), lambda b, j: (b, 0, 0, 0, 0))
    return pl.pallas_call(
        functools.partial(_dnscan_kernel, Gs=Gs),
        grid=(B, nb),
        in_specs=specs + [st_spec],
        out_specs=[pl.BlockSpec((None, Gs * C, 2 * W), lambda b, j: (b, j, 0)),
                   pl.BlockSpec((None, Gs * C, 2 * W), lambda b, j: (b, nb - 1 - j, 0)),
                   st_spec],
        out_shape=[jax.ShapeDtypeStruct((B, T, 2 * W), F32), jax.ShapeDtypeStruct((B, T, 2 * W), F32),
                   jax.ShapeDtypeStruct((B, 2, 2, W, W), F32)],
        scratch_shapes=[pltpu.VMEM((2, 2, W, W), F32)],
        compiler_params=_params("parallel", "arbitrary"),
        name="dnscan",
    )(*ops, *ops, s0)


def _attn_kernel(*refs, seg_blocks, tq, rt):
    q_ref = refs[0]
    nseg = len(seg_blocks)
    e_ref, o_ref = refs[1 + 2 * nseg:3 + 2 * nseg]
    G = ATT_Q_HEADS // ATT_KV_HEADS
    rows = G * tq
    q = q_ref[...].reshape(rows, HEAD_DIM)
    rt = min(rt, rows)
    qs = [q[r * rt:(r + 1) * rt] for r in range(rows // rt)]
    m = None
    acc = None
    for si, (nblk, kb) in enumerate(seg_blocks):
        for j in range(nblk):
            kT = refs[1 + 2 * si][:, j * kb:(j + 1) * kb]
            v = refs[2 + 2 * si][j * kb:(j + 1) * kb, :]
            s_all = jnp.dot(q, kT, preferred_element_type=F32)
            S = [s_all[r * rt:(r + 1) * rt] for r in range(rows // rt)]
            smax = [jnp.max(s, axis=-1, keepdims=True) for s in S]
            if m is None:
                m_new = smax
            else:
                m_new = [jnp.maximum(a, b) for a, b in zip(m, smax)]
                alpha = [jnp.exp2(a - b) for a, b in zip(m, m_new)]
            P = [jnp.exp2((s - mn).astype(BF16)) for s, mn in zip(S, m_new)]
            pv = jnp.dot(jnp.concatenate(P, axis=0), v, preferred_element_type=F32)
            acc = pv if acc is None else jnp.concatenate(alpha, axis=0) * acc + pv
            m = m_new
    o = (acc / acc[:, HEAD_DIM:HEAD_DIM + 1]).astype(BF16)
    out = jnp.dot(o[0:tq], e_ref[0], preferred_element_type=F32)
    for hh in range(1, G):
        out = out + jnp.dot(o[hh * tq:(hh + 1) * tq], e_ref[hh], preferred_element_type=F32)
    o_ref[...] = out.astype(BF16)


def _attn_call(q, segs, place, *, grid, q_map, seg_maps, out_map, out_rows, tq, kb, rt=512):
    G = ATT_Q_HEADS // ATT_KV_HEADS
    in_specs = [pl.BlockSpec((None, G, tq, HEAD_DIM), q_map)]
    args = [q]
    seg_blocks = []
    for (kT, v, S), (k_map, v_map) in zip(segs, seg_maps):
        blk = min(kb, S)
        seg_blocks.append((S // blk, blk))
        in_specs.append(pl.BlockSpec((None,) * (kT.ndim - 2) + (HEAD_DIM, S), k_map))
        in_specs.append(pl.BlockSpec((None,) * (v.ndim - 2) + (S, 2 * HEAD_DIM), v_map))
        args += [kT, v]
    in_specs.append(pl.BlockSpec((G, 2 * HEAD_DIM, G * HEAD_DIM), lambda b, g, t: (0, 0, 0)))
    args.append(place)
    return pl.pallas_call(
        functools.partial(_attn_kernel, seg_blocks=tuple(seg_blocks), tq=tq, rt=rt),
        grid=grid,
        in_specs=in_specs,
        out_specs=pl.BlockSpec((None, tq, G * HEAD_DIM), out_map),
        out_shape=jax.ShapeDtypeStruct(out_rows + (ATT_Q_HEADS * HEAD_DIM,), BF16),
        compiler_params=_params("parallel", "parallel", "parallel"),
        name="attn",
    )(*args)


def _lru_kernel(xf_ref, xfp_ref, xfn_ref, xb_ref, xbp_ref, xbn_ref, cw_ref, cb_ref, wr_ref, br_ref, wi_ref, bi_ref,
                lam_ref, h0_ref, hf_ref, hb_ref, hfin_ref, h_sc, a_sc, u_sc, *, SB, tt):
    W = LRU_WIDTH
    j = pl.program_id(1)
    last = pl.num_programs(1) - 1
    cw = cw_ref[...]
    cb = cb_ref[...]
    rid = lax.broadcasted_iota(jnp.int32, (8, W), 0)

    @pl.when(j == 0)
    def _():
        for s in range(SB):
            for d in range(2):
                h_sc[2 * s + d] = jnp.broadcast_to(h0_ref[s, d:d + 1, :], (8, W))

    def gates(x_ref, p_ref, n_ref, at_start, at_end, d):
        x = jnp.concatenate(
            [_conv4(_conv_window(x_ref[s, :, 0:W], p_ref[s, :, 0:W], n_ref[s, :, 0:W], at_start, at_end), cw, tt)
             for s in range(SB)], axis=0) + cb
        rg = jax.nn.sigmoid(_mm(x, wr_ref[d]) + br_ref[d:d + 1, :])
        ig = jax.nn.sigmoid(_mm(x, wi_ref[d]) + bi_ref[d:d + 1, :])
        log_a = (-LRU_C * rg) * _softplus(-lam_ref[d:d + 1, :])
        a = jnp.exp(log_a)
        a_sc[d] = a
        u_sc[d] = jnp.sqrt(-jnp.tanh(log_a) * (a * a + 1.0)) * (ig * x)

    gates(xf_ref, xfp_ref, xfn_ref, j == 0, j == last, 0)
    gates(xb_ref, xbp_ref, xbn_ref, j == last, j == 0, 1)

    chains = [(s, d) for s in range(SB) for d in range(2)]
    nblk = tt // 8

    def scan8(j8, hs):
        blk = [j8 if d == 0 else nblk - 1 - j8 for _, d in chains]
        row0 = [pl.multiple_of(s * tt + b * 8, 8) for (s, _), b in zip(chains, blk)]
        a8 = [a_sc[d, pl.ds(r0, 8), :] for (_, d), r0 in zip(chains, row0)]
        u8 = [u_sc[d, pl.ds(r0, 8), :] for (_, d), r0 in zip(chains, row0)]
        hs = list(hs)
        out = [jnp.zeros((8, W), F32)] * len(chains)
        for step in range(8):
            for i, (_, d) in enumerate(chains):
                r = step if d == 0 else 7 - step
                hs[i] = (jnp.broadcast_to(a8[i][r:r + 1, :], (8, W)) * hs[i]
                         + jnp.broadcast_to(u8[i][r:r + 1, :], (8, W)))
                out[i] = jnp.where(rid == r, hs[i], out[i])
        for i, (s, d) in enumerate(chains):
            dst = hf_ref if d == 0 else hb_ref
            dst[s, pl.ds(pl.multiple_of(blk[i] * 8, 8), 8), :] = out[i]
        return tuple(hs)

    hs = lax.fori_loop(0, nblk, scan8, tuple(h_sc[i] for i in range(len(chains))))
    for i in range(len(chains)):
        h_sc[i] = hs[i]

    @pl.when(j == last)
    def _():
        for i, (s, d) in enumerate(chains):
            hfin_ref[s, d:d + 1, :] = hs[i][0:1, :]


def _lru_call(xy, conv_w, conv_b, wr, br, wi, bi, lam, h0, *, layer, h0_layer, SB=4):
    B, T, _ = xy.shape
    W = LRU_WIDTH
    tt = min(256, T)
    nt = T // tt
    SB = min(SB, B)
    rb = tt // 8
    nb8 = T // 8
    lmap = lambda b, j: (layer, 0, 0)
    h0_spec = (pl.BlockSpec((SB, 2, W), lambda b, j: (b, 0, 0)) if h0_layer is None
               else pl.BlockSpec((SB, None, 2, W), lambda b, j: (b, h0_layer, 0, 0)))
    fwd = lambda b, j: j
    bwd = lambda b, j: nt - 1 - j
    tiles = []
    for tile in (fwd, bwd):
        tiles += [pl.BlockSpec((SB, tt, 2 * W), lambda b, j, tile=tile: (b, tile(b, j), 0)),
                  pl.BlockSpec((SB, 8, 2 * W), lambda b, j, tile=tile: (b, jnp.maximum(tile(b, j) * rb - 1, 0), 0)),
                  pl.BlockSpec((SB, 8, 2 * W),
                               lambda b, j, tile=tile: (b, jnp.minimum((tile(b, j) + 1) * rb, nb8 - 1), 0))]
    return pl.pallas_call(
        functools.partial(_lru_kernel, SB=SB, tt=tt),
        grid=(B // SB, nt),
        in_specs=tiles + [pl.BlockSpec((None, 4, W), lmap),
                          pl.BlockSpec((None, 1, W), lmap),
                          pl.BlockSpec((None, 2, W, W), lambda b, j: (layer, 0, 0, 0)),
                          pl.BlockSpec((None, 2, W), lmap),
                          pl.BlockSpec((None, 2, W, W), lambda b, j: (layer, 0, 0, 0)),
                          pl.BlockSpec((None, 2, W), lmap),
                          pl.BlockSpec((None, 2, W), lmap),
                          h0_spec],
        out_specs=[pl.BlockSpec((SB, tt, W), lambda b, j: (b, j, 0)),
                   pl.BlockSpec((SB, tt, W), lambda b, j: (b, nt - 1 - j, 0)),
                   pl.BlockSpec((SB, 2, W), lambda b, j: (b, 0, 0))],
        out_shape=[jax.ShapeDtypeStruct((B, T, W), F32), jax.ShapeDtypeStruct((B, T, W), F32),
                   jax.ShapeDtypeStruct((B, 2, W), F32)],
        scratch_shapes=[pltpu.VMEM((2 * SB, 8, W), F32), pltpu.VMEM((2, SB * tt, W), F32),
                        pltpu.VMEM((2, SB * tt, W), F32)],
        compiler_params=_params("parallel", "arbitrary"),
        name="lru",
    )(*([xy] * 6), conv_w, conv_b, wr, br, wi, bi, lam, h0)


def _merge_kernel(x_ref, mod_ref, nw_ref, of_ref, ob_ref, onw_ref, bd_ref, z_ref, oatt_ref, hf_ref, hb_ref, xy_ref,
                  wg_ref, wpa_ref, wpb_ref, wpc_ref, wo_ref, o_ref):
    D = x_ref.shape[-1]
    x = x_ref[...]
    h = _norm_mod(x, nw_ref[1:2, :], mod_ref[4:5, :], mod_ref[3:4, :]).astype(BF16)
    gates = jax.nn.sigmoid(jnp.dot(h, wg_ref[...], preferred_element_type=F32))
    odn = of_ref[...] + ob_ref[...]
    ms = _mm01(odn * odn, bd_ref[...]) * (1.0 / DN_DK)
    odn = (odn * lax.rsqrt(ms + EPS)) * onw_ref[...]
    a = _mm(odn * _silu(z_ref[...]), wpa_ref[...])
    b = jnp.dot(oatt_ref[...], wpb_ref[...], preferred_element_type=F32)
    c = _mm(_gelu_tanh(xy_ref[:, LRU_WIDTH:2 * LRU_WIDTH]) * (hf_ref[...] + hb_ref[...]), wpc_ref[...])
    merged = (gates[:, 0:D] * a + gates[:, D:2 * D] * b) + gates[:, 2 * D:3 * D] * c
    o_ref[...] = x + mod_ref[5:6, :] * _mm(merged, wo_ref[...])


def _merge_call(x, mod, norm_w, o_f, o_b, onw, bd256, z, o_att, h_f, h_b, xy, w_gate, w_pa, w_pb, w_pc, w_o, *,
                layer, mod_off, tm=512):
    B, T, D = x.shape
    tm = min(tm, T)
    im = lambda b, t: (b, t, 0)
    wspec = lambda w: pl.BlockSpec((None,) + w.shape[1:], lambda b, t: (layer, 0, 0), pipeline_mode=pl.Buffered(1))
    return pl.pallas_call(
        _merge_kernel,
        grid=(B, T // tm),
        in_specs=[pl.BlockSpec((None, tm, D), im),
                  pl.BlockSpec((None, None, N_MOD, D), lambda b, t: (layer, b + mod_off, 0, 0)),
                  pl.BlockSpec((None, 3, D), lambda b, t: (layer, 0, 0)),
                  pl.BlockSpec((None, tm, 256), im), pl.BlockSpec((None, tm, 256), im),
                  pl.BlockSpec((None, 1, 256), lambda b, t: (layer, 0, 0)),
                  pl.BlockSpec((256, 256), lambda b, t: (0, 0)),
                  pl.BlockSpec((None, tm, 256), im),
                  pl.BlockSpec((None, tm, 512), im),
                  pl.BlockSpec((None, tm, 256), im), pl.BlockSpec((None, tm, 256), im), pl.BlockSpec((None, tm, 512), im),
                  wspec(w_gate), wspec(w_pa), wspec(w_pb), wspec(w_pc), wspec(w_o)],
        out_specs=pl.BlockSpec((None, tm, D), im),
        out_shape=jax.ShapeDtypeStruct((B, T, D), F32),
        compiler_params=_params("parallel", "parallel"),
        name="merge",
    )(x, mod, norm_w, o_f, o_b, onw, bd256, z, o_att, h_f, h_b, xy, w_gate, w_pa, w_pb, w_pc, w_o)


def _blockdiag_ones(n, seg):
    i = np.arange(n)
    return jnp.asarray((i[:, None] // seg) == (i[None, :] // seg), BF16)


def _placement():
    G = ATT_Q_HEADS // ATT_KV_HEADS
    e = np.zeros((G, 2 * HEAD_DIM, G * HEAD_DIM), np.float32)
    d = np.arange(HEAD_DIM)
    for hh in range(G):
        e[hh, d, hh * HEAD_DIM + d] = 1.0
    return jnp.asarray(e, BF16)


def _with_ones_column(v):
    one = jnp.ones(v.shape[:-1] + (1,), v.dtype)
    zero = jnp.zeros(v.shape[:-1] + (HEAD_DIM - 1,), v.dtype)
    return jnp.concatenate([v, one, zero], axis=-1).astype(BF16)


def _rope_tables(n_tokens):
    rows = n_tokens // GRID_W
    row = jnp.broadcast_to(jnp.arange(rows, dtype=F32)[:, None], (rows, GRID_W)).reshape(-1)
    col = jnp.broadcast_to(jnp.arange(GRID_W, dtype=F32)[None, :], (rows, GRID_W)).reshape(-1)
    freqs = ROPE_BASE ** (-jnp.arange(ROPE_PAIRS, dtype=F32) / ROPE_PAIRS)
    ang = jnp.stack([row[:, None] * freqs, col[:, None] * freqs], axis=1)
    cos = jnp.cos(ang)[:, :, None, :]
    sin = jnp.sin(ang)[:, :, None, :]
    c = jnp.broadcast_to(cos, (n_tokens, 2, 2, ROPE_PAIRS)).reshape(n_tokens, HEAD_DIM)
    s = jnp.concatenate([-sin, sin], axis=2).reshape(n_tokens, HEAD_DIM)
    return jnp.concatenate([c, c], axis=1), jnp.concatenate([s, s], axis=1)


def _ba_layout(ba, n):
    B = ba.shape[0]
    x = ba.reshape(B, 2, 2, 2, 2, n, DN_CHUNK)
    x = x.transpose(0, 3, 5, 1, 2, 4, 6)
    return x.reshape(B, 2, n, 4, 2 * DN_CHUNK)


def _state_to_blockdiag(s):
    B = s.shape[0]
    x = s.reshape(B, 2, 2, 2, DN_DK, DN_DK)
    z = jnp.zeros_like(x[:, :, :, 0])
    top = jnp.concatenate([x[:, :, :, 0], z], axis=-1)
    bot = jnp.concatenate([z, x[:, :, :, 1]], axis=-1)
    return jnp.concatenate([top, bot], axis=-2).transpose(0, 2, 1, 3, 4)


def _blockdiag_to_state(sb):
    B = sb.shape[0]
    x = sb.transpose(0, 2, 1, 3, 4)
    h0 = x[..., 0:DN_DK, 0:DN_DK]
    h1 = x[..., DN_DK:, DN_DK:]
    return jnp.stack([h0, h1], axis=3).reshape(B, 2, DN_HEADS, DN_DK, DN_DK)


def _lru_blockdiag(w):
    L = w.shape[0]
    bw = LRU_WIDTH // LRU_BLOCKS
    out = jnp.zeros((L, 2, LRU_WIDTH, LRU_WIDTH), w.dtype)
    for n in range(LRU_BLOCKS):
        out = out.at[:, :, n * bw:(n + 1) * bw, n * bw:(n + 1) * bw].set(w[:, :, n])
    return out


def kernel(x_prompt, x_sample, cache_k, cache_v, state_delta, state_lru, c, c_ctx, w_mod, b_mod, norm_w, ffn1_wgu,
           ffn1_wd, ffn2_wgu, ffn2_wd, w_in, dn_conv_w, dn_a_log, dn_dt_bias, dn_onorm_w, att_qnorm_w, att_knorm_w,
           lru_conv_w, lru_conv_b, lru_wr, lru_br, lru_wi, lru_bi, lru_lam, w_pa, w_pb, w_pc, w_o):
    NB, SEQ, D = x_prompt.shape
    DB, DSEQ, _ = x_sample.shape
    L = w_mod.shape[0]
    PAST = cache_k.shape[2]
    TC = NB * SEQ

    cond = jnp.zeros((16, D), F32).at[0].set(c_ctx).at[1:1 + DB].set(c)
    mod = _mod_call(cond, w_mod, b_mod).reshape(L, 16, N_MOD, D)

    bf = lambda w: w.astype(BF16)
    ffn1_wgu, ffn1_wd, ffn2_wgu, ffn2_wd = bf(ffn1_wgu), bf(ffn1_wd), bf(ffn2_wgu), bf(ffn2_wd)
    w_main = bf(jnp.concatenate([w_in[:, :, 0:1024], w_in[:, :, 1040:2320]], axis=-1))
    w_ba = bf(jnp.swapaxes(w_in[:, :, 1024:1040], 1, 2))
    w_gate = bf(w_in[:, :, 2320:])
    w_pa, w_pb, w_pc, w_o = bf(w_pa), bf(w_pb), bf(w_pc), bf(w_o)
    qw = jnp.tile(att_qnorm_w, (1, ATT_Q_HEADS)).reshape(L, 1, ATT_Q_HEADS * HEAD_DIM)
    kw = jnp.tile(att_knorm_w, (1, ATT_KV_HEADS)).reshape(L, 1, ATT_KV_HEADS * HEAD_DIM)
    onw = jnp.tile(dn_onorm_w, (1, DN_HEADS)).reshape(L, 1, DN_HEADS * DN_DK)
    pr = lambda p: jnp.repeat(p.reshape(L, 2, 2, 2), DN_CHUNK, axis=-1).reshape(L, 2, 2, 2 * DN_CHUNK).transpose(0, 2, 1, 3)
    dn_prm = jnp.concatenate([pr(dn_a_log), pr(dn_dt_bias)], axis=2)
    wr_bd, wi_bd = bf(_lru_blockdiag(lru_wr)), bf(_lru_blockdiag(lru_wi))
    lru_cb = lru_conv_b.reshape(L, 1, LRU_WIDTH)
    bd512 = _blockdiag_ones(512, HEAD_DIM)
    bd256 = _blockdiag_ones(256, DN_DK)
    place = _placement()
    rope_tabs = _rope_tables(DSEQ)
    cache_kT = bf(cache_k.transpose(0, 1, 3, 4, 2))
    cache_v1 = _with_ones_column(cache_v.transpose(0, 1, 3, 2, 4))
    s0_lat = _state_to_blockdiag(state_delta.transpose(1, 0, 2, 3, 4, 5).reshape(L * DB, 2, DN_HEADS, DN_DK, DN_DK))
    s0_lat = s0_lat.reshape(L, DB, 2, 2, 2 * DN_DK, 2 * DN_DK)
    s0_ctx = jnp.zeros((NB, 2, 2, 2 * DN_DK, 2 * DN_DK), F32)
    h0_ctx = jnp.zeros((NB, 2, LRU_WIDTH), F32)

    xp = x_prompt.reshape(1, TC, D)
    xs = x_sample
    new_k, new_v, new_sd, new_sl = [], [], [], []
    for l in range(L):
        for ctx in (True, False):
            x = xp if ctx else xs
            mod_off = 0 if ctx else 1
            seq = SEQ if ctx else DSEQ
            nseq = NB if ctx else DB
            shp = x.shape[:2]
            x = _ffn_call(x, mod, norm_w, ffn1_wgu, ffn1_wd, layer=l, sub=0, mod_off=mod_off)
            outs = _inproj_call(x, mod, norm_w, w_main, w_ba, qw, kw, bd512, None if ctx else rope_tabs,
                                layer=l, mod_off=mod_off, emit_kv=ctx)
            qkv, z, xy, q_hm, kT, v_bf, ba = outs[:7]
            n = seq // DN_CHUNK
            ba_l = _ba_layout(ba.reshape(shp[0], 16, -1, seq).transpose(0, 2, 1, 3).reshape(nseq, 16, seq), n)
            ops = _dnchunk_call(qkv.reshape(nseq, seq, 768), dn_conv_w, ba_l, dn_prm, layer=l)
            o_f, o_b, s_fin = _dnscan_call(ops, s0_ctx if ctx else s0_lat[l], T=seq)
            if ctx:
                o_att = _attn_call(
                    q_hm, [(kT, v_bf, SEQ)], place, grid=(NB, ATT_KV_HEADS, 1),
                    q_map=lambda s, g, t: (0, g, s, 0),
                    seg_maps=[(lambda s, g, t: (0, g, 0, s), lambda s, g, t: (0, g, s, 0))],
                    out_map=lambda s, g, t: (0, s, g), out_rows=(1, TC), tq=SEQ, kb=512)
            else:
                tq = 128
                o_att = _attn_call(
                    q_hm, [(cache_kT, cache_v1, PAST), (kT, v_bf, DSEQ)], place, grid=(DB, ATT_KV_HEADS, DSEQ // tq),
                    q_map=lambda b, g, t: (b, g, t, 0),
                    seg_maps=[(lambda b, g, t: (b, l, g, 0, 0), lambda b, g, t: (b, l, g, 0, 0)),
                              (lambda b, g, t: (b, g, 0, 0), lambda b, g, t: (b, g, 0, 0))],
                    out_map=lambda b, g, t: (b, t, g), out_rows=(DB, DSEQ), tq=tq, kb=512)
            if ctx:
                h_f, h_b, h_fin = _lru_call(xy.reshape(nseq, seq, 512), lru_conv_w, lru_cb, wr_bd, lru_br, wi_bd,
                                            lru_bi, lru_lam, h0_ctx, layer=l, h0_layer=None)
            else:
                h_f, h_b, h_fin = _lru_call(xy, lru_conv_w, lru_cb, wr_bd, lru_br, wi_bd, lru_bi, lru_lam, state_lru,
                                            layer=l, h0_layer=l)
            x = _merge_call(x, mod, norm_w, o_f.reshape(shp + (256,)), o_b.reshape(shp + (256,)), onw, bd256, z, o_att,
                            h_f.reshape(shp + (256,)), h_b.reshape(shp + (256,)), xy, w_gate, w_pa, w_pb, w_pc, w_o,
                            layer=l, mod_off=mod_off)
            x = _ffn_call(x, mod, norm_w, ffn2_wgu, ffn2_wd, layer=l, sub=2, mod_off=mod_off)
            if ctx:
                xp = x
                kf, vf = outs[7:9]
                new_k.append(kf.reshape(NB, SEQ, ATT_KV_HEADS, HEAD_DIM))
                new_v.append(vf.reshape(NB, SEQ, ATT_KV_HEADS, HEAD_DIM))
                new_sd.append(_blockdiag_to_state(s_fin))
                new_sl.append(h_fin)
            else:
                xs = x
    return (xp.reshape(NB, SEQ, D), xs, jnp.stack(new_k, axis=1), jnp.stack(new_v, axis=1),
            jnp.stack(new_sd, axis=1), jnp.stack(new_sl, axis=1))
```

```python
import functools
import math

import numpy as np
import jax
import jax.numpy as jnp
from jax import lax
from jax.experimental import pallas as pl
from jax.experimental.pallas import tpu as pltpu

F32 = jnp.float32
BF16 = jnp.bfloat16

EPS = 1e-6
GRID_W = 64
HEAD_DIM = 64
ATT_Q_HEADS = 8
ATT_KV_HEADS = 2
ROPE_BASE = 10000.0
ROPE_PAIRS = HEAD_DIM // 4
DN_HEADS = 4
DN_DK = 64
DN_CHUNK = 64
LRU_WIDTH = 256
LRU_BLOCKS = 4
LRU_C = 8.0
N_MOD = 9
LANES = 128
V7X_VMEM_LIMIT_BYTES = 56 * 1024 * 1024


def _params(*sem):
    return pltpu.CompilerParams(dimension_semantics=sem, vmem_limit_bytes=V7X_VMEM_LIMIT_BYTES)


def _mm(a, b):
    return jnp.dot(a.astype(BF16), b.astype(BF16), preferred_element_type=F32)


def _mm_nt(a, b):
    return lax.dot_general(a.astype(BF16), b.astype(BF16), (((1,), (1,)), ((), ())), preferred_element_type=F32)


def _mm_tn(a, b):
    return lax.dot_general(a.astype(BF16), b.astype(BF16), (((0,), (0,)), ((), ())), preferred_element_type=F32)


def _split3(x):
    hi = x.astype(BF16)
    r = x - hi.astype(F32)
    mid = r.astype(BF16)
    lo = (r - mid.astype(F32)).astype(BF16)
    return hi, mid, lo


def _mm01(x, m01):
    hi, mid, lo = _split3(x)
    d = functools.partial(jnp.dot, preferred_element_type=F32)
    return d(hi, m01) + d(mid, m01) + d(lo, m01)


def _mm01x2(x, m01):
    hi = x.astype(BF16)
    lo = (x - hi.astype(F32)).astype(BF16)
    return jnp.dot(hi, m01, preferred_element_type=F32) + jnp.dot(lo, m01, preferred_element_type=F32)


def _mm01_nt(m01, x):
    hi, mid, lo = _split3(x)
    d = functools.partial(lax.dot_general, dimension_numbers=(((1,), (1,)), ((), ())), preferred_element_type=F32)
    return d(m01, hi) + d(m01, mid) + d(m01, lo)


def _mm3(a, b):
    n = a.shape[0]
    ah = a.astype(BF16)
    al = (a - ah.astype(F32)).astype(BF16)
    bh = b.astype(BF16)
    bl = (b - bh.astype(F32)).astype(BF16)
    top = jnp.dot(jnp.concatenate([ah, al], axis=0), bh, preferred_element_type=F32)
    return (top[0:n] + top[n:2 * n]) + jnp.dot(ah, bl, preferred_element_type=F32)


def _mm3_many(As, Bs):
    n = As[0].shape[0]
    ah = [a.astype(BF16) for a in As]
    bh = [b.astype(BF16) for b in Bs]
    al = [(a - h.astype(F32)).astype(BF16) for a, h in zip(As, ah)]
    bl = [(b - h.astype(F32)).astype(BF16) for b, h in zip(Bs, bh)]
    top = [jnp.dot(jnp.concatenate([h, l], axis=0), b, preferred_element_type=F32) for h, l, b in zip(ah, al, bh)]
    low = [jnp.dot(h, b, preferred_element_type=F32) for h, b in zip(ah, bl)]
    return [(t[0:n] + t[n:2 * n]) + w for t, w in zip(top, low)]


def _silu(x):
    return x * jax.nn.sigmoid(x)


def _softplus(x):
    return jnp.maximum(x, 0.0) + jnp.log1p(jnp.exp(-jnp.abs(x)))


def _gelu_tanh(x):
    return x * (0.5 * (1.0 + jnp.tanh(0.7978845608028654 * (x + 0.044715 * (x * x * x)))))


def _norm_mod(x, nw, scale, shift):
    ms = jnp.mean(x * x, axis=-1, keepdims=True)
    y = (x * lax.rsqrt(ms + EPS)) * nw
    return y * (1.0 + scale) + shift


def _seg_masks(n, seg):
    r = lax.broadcasted_iota(jnp.int32, (n, n), 0)
    c = lax.broadcasted_iota(jnp.int32, (n, n), 1)
    return r, c, (r // seg) == (c // seg)


def _swap16(x):
    w = x.shape[1]
    lane = lax.broadcasted_iota(jnp.int32, x.shape, 1)
    return jnp.where((lane & 16) == 0, pltpu.roll(x, w - 16, axis=1), pltpu.roll(x, 16, axis=1))


def _mod_kernel(c_ref, w_ref, b_ref, o_ref):
    o_ref[...] = _mm(_silu(c_ref[...]), w_ref[...]) + b_ref[...]


def _mod_call(cond, w_mod, b_mod):
    L, D, N = w_mod.shape
    R = cond.shape[0]
    tn = D
    return pl.pallas_call(
        _mod_kernel,
        grid=(L, N // tn),
        in_specs=[pl.BlockSpec((R, D), lambda l, j: (0, 0)),
                  pl.BlockSpec((None, D, tn), lambda l, j: (l, 0, j)),
                  pl.BlockSpec((None, 1, tn), lambda l, j: (l, 0, j))],
        out_specs=pl.BlockSpec((None, R, tn), lambda l, j: (l, 0, j)),
        out_shape=jax.ShapeDtypeStruct((L, R, N), F32),
        compiler_params=_params("parallel", "parallel"),
        name="mod",
    )(cond, w_mod, b_mod.reshape(L, 1, N))


def _ffn_kernel(x_ref, mod_ref, nw_ref, wgu_ref, wd_ref, o_ref, *, sub, F, nc):
    x = x_ref[...]
    h = _norm_mod(x, nw_ref[sub:sub + 1, :], mod_ref[3 * sub + 1:3 * sub + 2, :],
                  mod_ref[3 * sub:3 * sub + 1, :]).astype(BF16)
    dot = functools.partial(jnp.dot, preferred_element_type=F32)
    cf = F // nc
    gu = [(dot(h, wgu_ref[:, c * cf:(c + 1) * cf]), dot(h, wgu_ref[:, F + c * cf:F + (c + 1) * cf]))
          for c in range(nc)]
    acc = None
    for c, (g, u) in enumerate(gu):
        part = dot((_silu(g) * u).astype(BF16), wd_ref[c * cf:(c + 1) * cf, :])
        acc = part if acc is None else acc + part
    o_ref[...] = x + (0.5 * mod_ref[3 * sub + 2:3 * sub + 3, :]) * acc


def _ffn_call(x, mod, norm_w, w_gu, w_d, *, layer, sub, mod_off, tm=512):
    B, T, D = x.shape
    F = w_d.shape[1]
    tm = min(tm, T)
    nc = 2 if F % (2 * LANES) == 0 else 1
    return pl.pallas_call(
        functools.partial(_ffn_kernel, sub=sub, F=F, nc=nc),
        grid=(B, T // tm),
        in_specs=[pl.BlockSpec((None, tm, D), lambda b, t: (b, t, 0)),
                  pl.BlockSpec((None, None, N_MOD, D), lambda b, t: (layer, b + mod_off, 0, 0)),
                  pl.BlockSpec((None, 3, D), lambda b, t: (layer, 0, 0)),
                  pl.BlockSpec((None, D, 2 * F), lambda b, t: (layer, 0, 0), pipeline_mode=pl.Buffered(1)),
                  pl.BlockSpec((None, F, D), lambda b, t: (layer, 0, 0), pipeline_mode=pl.Buffered(1))],
        out_specs=pl.BlockSpec((None, tm, D), lambda b, t: (b, t, 0)),
        out_shape=jax.ShapeDtypeStruct((B, T, D), F32),
        compiler_params=_params("parallel", "parallel"),
        name="ffn",
    )(x, mod, norm_w, w_gu, w_d)


def _inproj_kernel(*refs, rope, emit_kv):
    (x_ref, mod_ref, nw_ref, w_ref, wba_ref, qw_ref, kw_ref, bd_ref) = refs[:8]
    refs = refs[8:]
    if rope:
        rc_ref, rs_ref = refs[:2]
        refs = refs[2:]
    qkv_ref, z_ref, xy_ref, q_ref, kT_ref, v_ref, ba_ref = refs[:7]
    refs = refs[7:]

    h = _norm_mod(x_ref[...], nw_ref[1:2, :], mod_ref[4:5, :], mod_ref[3:4, :]).astype(BF16)
    ya = jnp.dot(h, w_ref[:, 0:768], preferred_element_type=F32)
    aq = ya[:, 0:512]
    ak = ya[:, 512:640]
    v = ya[:, 640:768]
    bd = bd_ref[...]
    inv_hd = 1.0 / HEAD_DIM
    ssq = _mm01x2(aq * aq, bd)
    ssk = _mm01x2(ak * ak, bd[0:128, 0:128])
    ba_ref[...] = lax.dot_general(wba_ref[...], h, (((1,), (1,)), ((), ())), preferred_element_type=F32)
    y = jnp.dot(h, w_ref[:, 768:2304], preferred_element_type=F32)
    qkv_ref[...] = y[:, 0:768]
    z_ref[...] = y[:, 768:1024]
    xy_ref[...] = y[:, 1024:1536]
    qn = (aq * lax.rsqrt(ssq * inv_hd + EPS)) * qw_ref[...]
    kn = (ak * lax.rsqrt(ssk * inv_hd + EPS)) * kw_ref[...]
    if emit_kv:
        kf_ref, vf_ref = refs
        kf_ref[...] = kn
        vf_ref[...] = v
    if rope:
        rc = rc_ref[...]
        rs = rs_ref[...]
        kn = kn * rc + _swap16(kn) * rs
        qn = qn * jnp.concatenate([rc] * 4, axis=1) + _swap16(qn) * jnp.concatenate([rs] * 4, axis=1)
    qs = qn * (HEAD_DIM ** -0.5 * math.log2(math.e))
    for hh in range(ATT_Q_HEADS):
        q_ref[hh] = qs[:, hh * HEAD_DIM:(hh + 1) * HEAD_DIM].astype(BF16)
    kT = kn.T
    kT_ref[0] = kT[0:HEAD_DIM, :].astype(BF16)
    kT_ref[1] = kT[HEAD_DIM:2 * HEAD_DIM, :].astype(BF16)
    lane = lax.broadcasted_iota(jnp.int32, v.shape, 1)
    tail = jnp.where(lane == HEAD_DIM, 1.0, 0.0)
    v_ref[0] = jnp.where(lane < HEAD_DIM, v, tail).astype(BF16)
    v_ref[1] = jnp.where(lane < HEAD_DIM, pltpu.roll(v, HEAD_DIM, axis=1), tail).astype(BF16)


def _inproj_call(x, mod, norm_w, w_main, w_ba, qw, kw, bd512, rope_tabs, *, layer, mod_off, emit_kv, tm=512):
    B, T, D = x.shape
    tm = min(tm, T)
    NW = w_main.shape[2]
    rope = rope_tabs is not None
    im = lambda b, t: (b, t, 0)
    in_specs = [pl.BlockSpec((None, tm, D), im),
                pl.BlockSpec((None, None, N_MOD, D), lambda b, t: (layer, b + mod_off, 0, 0)),
                pl.BlockSpec((None, 3, D), lambda b, t: (layer, 0, 0)),
                pl.BlockSpec((None, D, NW), lambda b, t: (layer, 0, 0)),
                pl.BlockSpec((None, 16, D), lambda b, t: (layer, 0, 0)),
                pl.BlockSpec((None, 1, 512), lambda b, t: (layer, 0, 0)),
                pl.BlockSpec((None, 1, 128), lambda b, t: (layer, 0, 0)),
                pl.BlockSpec((512, 512), lambda b, t: (0, 0))]
    args = [x, mod, norm_w, w_main, w_ba, qw, kw, bd512]
    if rope:
        in_specs += [pl.BlockSpec((tm, 128), lambda b, t: (t, 0))] * 2
        args += list(rope_tabs)
    out_shape = [jax.ShapeDtypeStruct((B, T, 768), F32), jax.ShapeDtypeStruct((B, T, 256), F32),
                 jax.ShapeDtypeStruct((B, T, 512), F32), jax.ShapeDtypeStruct((B, ATT_Q_HEADS, T, HEAD_DIM), BF16),
                 jax.ShapeDtypeStruct((B, ATT_KV_HEADS, HEAD_DIM, T), BF16),
                 jax.ShapeDtypeStruct((B, ATT_KV_HEADS, T, 128), BF16),
                 jax.ShapeDtypeStruct((B, 16, T), F32)]
    out_specs = [pl.BlockSpec((None, tm, 768), im), pl.BlockSpec((None, tm, 256), im), pl.BlockSpec((None, tm, 512), im),
                 pl.BlockSpec((None, ATT_Q_HEADS, tm, HEAD_DIM), lambda b, t: (b, 0, t, 0)),
                 pl.BlockSpec((None, ATT_KV_HEADS, HEAD_DIM, tm), lambda b, t: (b, 0, 0, t)),
                 pl.BlockSpec((None, ATT_KV_HEADS, tm, 128), lambda b, t: (b, 0, t, 0)),
                 pl.BlockSpec((None, 16, tm), lambda b, t: (b, 0, t))]
    if emit_kv:
        out_shape += [jax.ShapeDtypeStruct((B, T, 128), F32)] * 2
        out_specs += [pl.BlockSpec((None, tm, 128), im)] * 2
    return pl.pallas_call(
        functools.partial(_inproj_kernel, rope=rope, emit_kv=emit_kv),
        grid=(B, T // tm),
        in_specs=in_specs, out_specs=out_specs, out_shape=out_shape,
        compiler_params=_params("parallel", "parallel"),
        name="inproj",
    )(*args)


def _conv_window(main, prev, nxt, at_start, at_end):
    prev = jnp.where(at_start, 0.0, prev)
    nxt = jnp.where(at_end, 0.0, nxt)
    return jnp.concatenate([prev, main, nxt], axis=0)


def _conv4(win, w, n):
    return (w[0:1, :] * win[6:6 + n] + w[1:2, :] * win[7:7 + n]) + (w[2:3, :] * win[8:8 + n] + w[3:4, :] * win[9:9 + n])


def _dnchunk_kernel(q_ref, qp_ref, qn_ref, k_ref, kp_ref, kn_ref, v_ref, vp_ref, vn_ref, cwq_ref, cwk_ref, cwv_ref,
                    ba_ref, prm_ref, p_ref, qm_ref, o1_ref, o2_ref, gs_ref, *, G):
    C = DN_CHUNK
    W = 2 * C
    r, c, same = _seg_masks(W, C)
    i_loc = r % C
    j_loc = c % C
    m01 = lambda mask: jnp.where(mask, 1.0, 0.0).astype(BF16)
    cum_f = m01(same & (i_loc <= j_loc))
    cum_b = m01(same & (i_loc >= j_loc))
    ones_bd = m01(same)
    eye = m01(r == c)
    eye_f = jnp.where(r == c, 1.0, 0.0)
    blk = tuple((r // s) == (c // s) for s in (8, 16, 32, 64))
    incl = (same & (i_loc >= j_loc), same & (i_loc <= j_loc))
    strict = (same & (i_loc > j_loc), same & (i_loc < j_loc))
    head0 = lax.broadcasted_iota(jnp.int32, (C, W), 1) < C
    row2 = lax.broadcasted_iota(jnp.int32, (2, W), 0)
    alog = prm_ref[0:2, :]
    dtb = prm_ref[2:4, :]

    jblk = pl.program_id(2)
    at_start = jblk == 0
    at_end = jblk == pl.num_programs(2) - 1

    def conv_silu(main_ref, prev_ref, next_ref, cw_ref):
        win = _conv_window(main_ref[...], prev_ref[...], next_ref[...], at_start, at_end)
        return _silu(_conv4(win, cw_ref[...], G * C))

    def l2norm(x):
        return x * lax.rsqrt(_mm01(x * x, ones_bd) + EPS)

    q_all = l2norm(conv_silu(q_ref, qp_ref, qn_ref, cwq_ref)) * (DN_DK ** -0.5)
    k_all = l2norm(conv_silu(k_ref, kp_ref, kn_ref, cwk_ref))
    v_all = conv_silu(v_ref, vp_ref, vn_ref, cwv_ref)

    def stack(x):
        return jnp.concatenate([jnp.where(head0, x, 0.0), jnp.where(head0, 0.0, x)], axis=0)

    chunks = range(G)
    bg = [ba_ref[gi] for gi in chunks]
    beta = [jax.nn.sigmoid(b[0:2, :]) for b in bg]
    g = [-jnp.exp(alog) * _softplus(b[2:4, :] + dtb) for b in bg]
    gcf = [_mm01(x, cum_f) for x in g]
    gcb = [_mm01(x, cum_b) for x in g]
    tot = [_mm01(x, ones_bd) for x in g]
    rows = [jnp.concatenate([beta[i], jnp.where(row2 == 0, gcf[i], gcb[i]), tot[i], jnp.zeros((2, W), F32)], axis=0)
            for i in chunks]
    cols = [_mm01_nt(eye, x) for x in rows]
    Kst = [stack(k_all[gi * C:(gi + 1) * C, :]) for gi in chunks]
    Qst = [stack(q_all[gi * C:(gi + 1) * C, :]) for gi in chunks]
    Vst = [stack(v_all[gi * C:(gi + 1) * C, :]) for gi in chunks]
    kq = [_mm_nt(jnp.concatenate([Kst[i], Qst[i]], axis=0), Kst[i]) for i in chunks]

    chains = [(gi, d) for gi in chunks for d in range(2)]
    col = lambda gi, k: cols[gi][:, k:k + 1]
    dec = [jnp.exp(jnp.where(incl[d], col(gi, 2 + d) - rows[gi][2 + d:3 + d, :], -jnp.inf)) for gi, d in chains]
    L = [(col(gi, d) * kq[gi][0:W]) * jnp.where(strict[d], dec[i], 0.0) for i, (gi, d) in enumerate(chains)]
    QK = [kq[gi][W:2 * W] * dec[i] for i, (gi, d) in enumerate(chains)]
    dot = functools.partial(jnp.dot, preferred_element_type=F32)
    D8f = [jnp.where(blk[0], x, 0.0) for x in L]
    D8 = [x.astype(BF16) for x in D8f]
    M = [dot(d8, d8) for d8 in D8]
    Mb = [m.astype(BF16) for m in M]
    DM = [dot(d8, mb) for d8, mb in zip(D8, Mb)]
    R = [(m - d8) - dm for m, d8, dm in zip(M, D8f, DM)]
    M = [dot(mb, mb) for mb in Mb]
    RM = [dot(r_.astype(BF16), m.astype(BF16)) for r_, m in zip(R, M)]
    Tm = [eye_f + ((r_ + m) + rm) for r_, m, rm in zip(R, M, RM)]
    for lvl in range(3):
        off = blk[lvl + 1] & jnp.logical_not(blk[lvl])
        Th = [t.astype(BF16) for t in Tm]
        Bm = [jnp.where(off, x, 0.0) for x in L]
        Bh = [b.astype(BF16) for b in Bm]
        Bl = [(b - h.astype(F32)).astype(BF16) for b, h in zip(Bm, Bh)]
        TB2 = [dot(th, jnp.concatenate([bh, bl], axis=1)) for th, bh, bl in zip(Th, Bh, Bl)]
        TB = [x[:, 0:W] + x[:, W:2 * W] for x in TB2]
        TBh = [x.astype(BF16) for x in TB]
        TBl = [(x - h.astype(F32)).astype(BF16) for x, h in zip(TB, TBh)]
        TBT2 = [dot(jnp.concatenate([h, lo], axis=0), th) for h, lo, th in zip(TBh, TBl, Th)]
        Tm = [t - (x[0:W] + x[W:2 * W]) for t, x in zip(Tm, TBT2)]
    rhs = [jnp.concatenate([(col(gi, d) * jnp.exp(col(gi, 2 + d))) * Kst[gi], col(gi, d) * Vst[gi]], axis=1)
           for gi, d in chains]
    X = _mm3_many(Tm, rhs)
    kd = [Kst[gi] * jnp.exp(col(gi, 4 + d) - col(gi, 2 + d)) for gi, d in chains]
    PQ = [_mm_tn(a, x) for a, x in zip(kd, X)]
    OO = [_mm(a, x) for a, x in zip(QK, X)]
    for i, (gi, d) in enumerate(chains):
        o1 = Qst[gi] * jnp.exp(col(gi, 2 + d)) - OO[i][:, 0:W]
        o2 = OO[i][:, W:2 * W]
        p_ref[d, gi] = PQ[i][:, 0:W].astype(BF16)
        qm_ref[d, gi] = PQ[i][:, W:2 * W]
        o1_ref[d, gi] = (o1[0:C] + o1[C:W]).astype(BF16)
        o2_ref[d, gi] = o2[0:C] + o2[C:W]
        gs_ref[d, gi] = jnp.exp(rows[gi][4 + d:5 + d, :])


def _dnchunk_call(qkv, conv_w, ba, prm, *, layer, G=8):
    B, T, _ = qkv.shape
    C = DN_CHUNK
    W = 2 * C
    n = T // C
    G = min(G, n)
    rb = G * C // 8
    nb8 = T // 8
    sds = jax.ShapeDtypeStruct
    mat = lambda rows, dt: (sds((B, 2, 2, n, rows, W), dt),
                            pl.BlockSpec((None, None, 2, G, rows, W), lambda b, p, j: (b, p, 0, j, 0, 0)))
    outs = [mat(W, BF16), mat(W, F32), mat(C, BF16), mat(C, F32), mat(1, F32)]
    in_specs = []
    for part in range(3):
        in_specs += [pl.BlockSpec((None, G * C, W), lambda b, p, j, part=part: (b, j, 2 * part + p)),
                     pl.BlockSpec((None, 8, W), lambda b, p, j, part=part: (b, jnp.maximum(j * rb - 1, 0), 2 * part + p)),
                     pl.BlockSpec((None, 8, W),
                                  lambda b, p, j, part=part: (b, jnp.minimum((j + 1) * rb, nb8 - 1), 2 * part + p))]
    in_specs += [pl.BlockSpec((None, 4, W), lambda b, p, j, part=part: (layer, 0, 2 * part + p)) for part in range(3)]
    in_specs += [pl.BlockSpec((None, None, G, 4, W), lambda b, p, j: (b, p, j, 0, 0)),
                 pl.BlockSpec((None, None, 4, W), lambda b, p, j: (layer, p, 0, 0))]
    return pl.pallas_call(
        functools.partial(_dnchunk_kernel, G=G),
        grid=(B, 2, n // G),
        in_specs=in_specs,
        out_specs=[o[1] for o in outs],
        out_shape=[o[0] for o in outs],
        compiler_params=_params("parallel", "parallel", "parallel"),
        name="dnchunk",
    )(*([qkv] * 9), conv_w, conv_w, conv_w, ba, prm)


def _dnscan_kernel(pf_ref, qf_ref, o1f_ref, o2f_ref, gf_ref, pb_ref, qb_ref, o1b_ref, o2b_ref, gb_ref, s0_ref,
                   of_ref, ob_ref, sfin_ref, s_sc, *, Gs):
    C = DN_CHUNK
    W = 2 * C
    j = pl.program_id(1)

    @pl.when(j == 0)
    def _():
        s_sc[...] = s0_ref[...]

    S = [[s_sc[p, d] for d in range(2)] for p in range(2)]
    fwd = (pf_ref, qf_ref, o1f_ref, o2f_ref, gf_ref, of_ref)
    bwd = (pb_ref, qb_ref, o1b_ref, o2b_ref, gb_ref, ob_ref)
    for i in range(Gs):
        for d, (P, Qm, O1, O2, GS, out) in enumerate((fwd, bwd)):
            ci = i if d == 0 else Gs - 1 - i
            for p in range(2):
                Sb = S[p][d].astype(BF16)
                out[ci * C:(ci + 1) * C, p * W:(p + 1) * W] = (
                    jnp.dot(O1[p, ci], Sb, preferred_element_type=F32) + O2[p, ci])
                S[p][d] = (GS[p, ci] * S[p][d] - jnp.dot(P[p, ci], Sb, preferred_element_type=F32)) + Qm[p, ci]
    for p in range(2):
        for d in range(2):
            s_sc[p, d] = S[p][d]

    @pl.when(j == pl.num_programs(1) - 1)
    def _():
        sfin_ref[...] = s_sc[...]


def _dnscan_call(ops, s0, *, T, Gs=8):
    B = s0.shape[0]
    C = DN_CHUNK
    W = 2 * C
    n = T // C
    Gs = min(Gs, n)
    nb = n // Gs
    specs = []
    for d in range(2):
        for a in ops:
            rows = a.shape[4]
            if d == 0:
                specs.append(pl.BlockSpec((None, 2, None, Gs, rows, W), lambda b, j: (b, 0, 0, j, 0, 0)))
            else:
                specs.append(pl.BlockSpec((None, 2, None, Gs, rows, W), lambda b, j: (b, 0, 1, nb - 1 - j, 0, 0)))
    st_spec = pl.BlockSpec((None, 2, 2, W, W), lambda b, j: (b, 0, 0, 0, 0))
    return pl.pallas_call(
        functools.partial(_dnscan_kernel, Gs=Gs),
        grid=(B, nb),
        in_specs=specs + [st_spec],
        out_specs=[pl.BlockSpec((None, Gs * C, 2 * W), lambda b, j: (b, j, 0)),
                   pl.BlockSpec((None, Gs * C, 2 * W), lambda b, j: (b, nb - 1 - j, 0)),
                   st_spec],
        out_shape=[jax.ShapeDtypeStruct((B, T, 2 * W), F32), jax.ShapeDtypeStruct((B, T, 2 * W), F32),
                   jax.ShapeDtypeStruct((B, 2, 2, W, W), F32)],
        scratch_shapes=[pltpu.VMEM((2, 2, W, W), F32)],
        compiler_params=_params("parallel", "arbitrary"),
        name="dnscan",
    )(*ops, *ops, s0)


def _attn_kernel(*refs, seg_blocks, tq, rt):
    q_ref = refs[0]
    nseg = len(seg_blocks)
    e_ref, o_ref = refs[1 + 2 * nseg:3 + 2 * nseg]
    G = ATT_Q_HEADS // ATT_KV_HEADS
    rows = G * tq
    rt = min(rt, rows)
    for g in range(ATT_KV_HEADS):
        q = q_ref[g * G:(g + 1) * G].reshape(rows, HEAD_DIM)
        m = None
        acc = None
        for si, (nblk, kb) in enumerate(seg_blocks):
            for j in range(nblk):
                kT = refs[1 + 2 * si][g, :, j * kb:(j + 1) * kb]
                v = refs[2 + 2 * si][g, j * kb:(j + 1) * kb, :]
                s_all = jnp.dot(q, kT, preferred_element_type=F32)
                S = [s_all[r * rt:(r + 1) * rt] for r in range(rows // rt)]
                smax = [jnp.max(s, axis=-1, keepdims=True) for s in S]
                if m is None:
                    m_new = smax
                else:
                    m_new = [jnp.maximum(a, b) for a, b in zip(m, smax)]
                    alpha = [jnp.exp2(a - b) for a, b in zip(m, m_new)]
                P = [jnp.exp2((s - mn).astype(BF16)) for s, mn in zip(S, m_new)]
                pv = jnp.dot(jnp.concatenate(P, axis=0), v, preferred_element_type=F32)
                acc = pv if acc is None else jnp.concatenate(alpha, axis=0) * acc + pv
                m = m_new
        o = (acc / acc[:, HEAD_DIM:HEAD_DIM + 1]).astype(BF16)
        out = jnp.dot(o[0:tq], e_ref[0], preferred_element_type=F32)
        for hh in range(1, G):
            out = out + jnp.dot(o[hh * tq:(hh + 1) * tq], e_ref[hh], preferred_element_type=F32)
        o_ref[:, g * G * HEAD_DIM:(g + 1) * G * HEAD_DIM] = out.astype(BF16)


def _attn_call(q, segs, place, *, grid, q_map, seg_maps, out_map, out_rows, tq, kb, rt=512):
    G = ATT_Q_HEADS // ATT_KV_HEADS
    in_specs = [pl.BlockSpec((None, ATT_Q_HEADS, tq, HEAD_DIM), q_map)]
    args = [q]
    seg_blocks = []
    for (kT, v, S), (k_map, v_map) in zip(segs, seg_maps):
        blk = min(kb, S)
        seg_blocks.append((S // blk, blk))
        in_specs.append(pl.BlockSpec((None,) * (kT.ndim - 3) + (ATT_KV_HEADS, HEAD_DIM, S), k_map))
        in_specs.append(pl.BlockSpec((None,) * (v.ndim - 3) + (ATT_KV_HEADS, S, 2 * HEAD_DIM), v_map))
        args += [kT, v]
    in_specs.append(pl.BlockSpec((G, 2 * HEAD_DIM, G * HEAD_DIM), lambda b, t: (0, 0, 0)))
    args.append(place)
    return pl.pallas_call(
        functools.partial(_attn_kernel, seg_blocks=tuple(seg_blocks), tq=tq, rt=rt),
        grid=grid,
        in_specs=in_specs,
        out_specs=pl.BlockSpec((None, tq, ATT_Q_HEADS * HEAD_DIM), out_map),
        out_shape=jax.ShapeDtypeStruct(out_rows + (ATT_Q_HEADS * HEAD_DIM,), BF16),
        compiler_params=_params("parallel", "parallel"),
        name="attn",
    )(*args)


def _lru_kernel(xf_ref, xfp_ref, xfn_ref, xb_ref, xbp_ref, xbn_ref, cw_ref, cb_ref, wr_ref, br_ref, wi_ref, bi_ref,
                lam_ref, h0_ref, hf_ref, hb_ref, hfin_ref, h_sc, a_sc, u_sc, *, SB, tt):
    W = LRU_WIDTH
    j = pl.program_id(1)
    last = pl.num_programs(1) - 1
    cw = cw_ref[...]
    cb = cb_ref[...]
    rid = lax.broadcasted_iota(jnp.int32, (8, W), 0)

    @pl.when(j == 0)
    def _():
        for s in range(SB):
            for d in range(2):
                h_sc[2 * s + d] = jnp.broadcast_to(h0_ref[s, d:d + 1, :], (8, W))

    def gates(x_ref, p_ref, n_ref, at_start, at_end, d):
        x = jnp.concatenate(
            [_conv4(_conv_window(x_ref[s, :, 0:W], p_ref[s, :, 0:W], n_ref[s, :, 0:W], at_start, at_end), cw, tt)
             for s in range(SB)], axis=0) + cb
        rg = jax.nn.sigmoid(_mm(x, wr_ref[d]) + br_ref[d:d + 1, :])
        ig = jax.nn.sigmoid(_mm(x, wi_ref[d]) + bi_ref[d:d + 1, :])
        log_a = (-LRU_C * rg) * _softplus(-lam_ref[d:d + 1, :])
        a = jnp.exp(log_a)
        a_sc[d] = a
        u_sc[d] = jnp.sqrt(-jnp.tanh(log_a) * (a * a + 1.0)) * (ig * x)

    gates(xf_ref, xfp_ref, xfn_ref, j == 0, j == last, 0)
    gates(xb_ref, xbp_ref, xbn_ref, j == last, j == 0, 1)

    chains = [(s, d) for s in range(SB) for d in range(2)]
    nblk = tt // 8

    def scan8(j8, hs):
        blk = [j8 if d == 0 else nblk - 1 - j8 for _, d in chains]
        row0 = [pl.multiple_of(s * tt + b * 8, 8) for (s, _), b in zip(chains, blk)]
        a8 = [a_sc[d, pl.ds(r0, 8), :] for (_, d), r0 in zip(chains, row0)]
        u8 = [u_sc[d, pl.ds(r0, 8), :] for (_, d), r0 in zip(chains, row0)]
        hs = list(hs)
        out = [jnp.zeros((8, W), F32)] * len(chains)
        for step in range(8):
            for i, (_, d) in enumerate(chains):
                r = step if d == 0 else 7 - step
                hs[i] = (jnp.broadcast_to(a8[i][r:r + 1, :], (8, W)) * hs[i]
                         + jnp.broadcast_to(u8[i][r:r + 1, :], (8, W)))
                out[i] = jnp.where(rid == r, hs[i], out[i])
        for i, (s, d) in enumerate(chains):
            dst = hf_ref if d == 0 else hb_ref
            dst[s, pl.ds(pl.multiple_of(blk[i] * 8, 8), 8), :] = out[i]
        return tuple(hs)

    hs = lax.fori_loop(0, nblk, scan8, tuple(h_sc[i] for i in range(len(chains))))
    for i in range(len(chains)):
        h_sc[i] = hs[i]

    @pl.when(j == last)
    def _():
        for i, (s, d) in enumerate(chains):
            hfin_ref[s, d:d + 1, :] = hs[i][0:1, :]


def _lru_call(xy, conv_w, conv_b, wr, br, wi, bi, lam, h0, *, layer, h0_layer, SB=4):
    B, T, _ = xy.shape
    W = LRU_WIDTH
    tt = min(256, T)
    nt = T // tt
    SB = min(SB, B)
    rb = tt // 8
    nb8 = T // 8
    lmap = lambda b, j: (layer, 0, 0)
    h0_spec = (pl.BlockSpec((SB, 2, W), lambda b, j: (b, 0, 0)) if h0_layer is None
               else pl.BlockSpec((SB, None, 2, W), lambda b, j: (b, h0_layer, 0, 0)))
    fwd = lambda b, j: j
    bwd = lambda b, j: nt - 1 - j
    tiles = []
    for tile in (fwd, bwd):
        tiles += [pl.BlockSpec((SB, tt, 2 * W), lambda b, j, tile=tile: (b, tile(b, j), 0)),
                  pl.BlockSpec((SB, 8, 2 * W), lambda b, j, tile=tile: (b, jnp.maximum(tile(b, j) * rb - 1, 0), 0)),
                  pl.BlockSpec((SB, 8, 2 * W),
                               lambda b, j, tile=tile: (b, jnp.minimum((tile(b, j) + 1) * rb, nb8 - 1), 0))]
    return pl.pallas_call(
        functools.partial(_lru_kernel, SB=SB, tt=tt),
        grid=(B // SB, nt),
        in_specs=tiles + [pl.BlockSpec((None, 4, W), lmap),
                          pl.BlockSpec((None, 1, W), lmap),
                          pl.BlockSpec((None, 2, W, W), lambda b, j: (layer, 0, 0, 0)),
                          pl.BlockSpec((None, 2, W), lmap),
                          pl.BlockSpec((None, 2, W, W), lambda b, j: (layer, 0, 0, 0)),
                          pl.BlockSpec((None, 2, W), lmap),
                          pl.BlockSpec((None, 2, W), lmap),
                          h0_spec],
        out_specs=[pl.BlockSpec((SB, tt, W), lambda b, j: (b, j, 0)),
                   pl.BlockSpec((SB, tt, W), lambda b, j: (b, nt - 1 - j, 0)),
                   pl.BlockSpec((SB, 2, W), lambda b, j: (b, 0, 0))],
        out_shape=[jax.ShapeDtypeStruct((B, T, W), F32), jax.ShapeDtypeStruct((B, T, W), F32),
                   jax.ShapeDtypeStruct((B, 2, W), F32)],
        scratch_shapes=[pltpu.VMEM((2 * SB, 8, W), F32), pltpu.VMEM((2, SB * tt, W), F32),
                        pltpu.VMEM((2, SB * tt, W), F32)],
        compiler_params=_params("parallel", "arbitrary"),
        name="lru",
    )(*([xy] * 6), conv_w, conv_b, wr, br, wi, bi, lam, h0)


def _merge_kernel(x_ref, mod_ref, nw_ref, of_ref, ob_ref, onw_ref, bd_ref, z_ref, oatt_ref, hf_ref, hb_ref, xy_ref,
                  wg_ref, wpa_ref, wpb_ref, wpc_ref, wo_ref, o_ref):
    D = x_ref.shape[-1]
    x = x_ref[...]
    h = _norm_mod(x, nw_ref[1:2, :], mod_ref[4:5, :], mod_ref[3:4, :]).astype(BF16)
    gates = jax.nn.sigmoid(jnp.dot(h, wg_ref[...], preferred_element_type=F32))
    odn = of_ref[...] + ob_ref[...]
    ms = _mm01(odn * odn, bd_ref[...]) * (1.0 / DN_DK)
    odn = (odn * lax.rsqrt(ms + EPS)) * onw_ref[...]
    a = _mm(odn * _silu(z_ref[...]), wpa_ref[...])
    b = jnp.dot(oatt_ref[...], wpb_ref[...], preferred_element_type=F32)
    c = _mm(_gelu_tanh(xy_ref[:, LRU_WIDTH:2 * LRU_WIDTH]) * (hf_ref[...] + hb_ref[...]), wpc_ref[...])
    merged = (gates[:, 0:D] * a + gates[:, D:2 * D] * b) + gates[:, 2 * D:3 * D] * c
    o_ref[...] = x + mod_ref[5:6, :] * _mm(merged, wo_ref[...])


def _merge_call(x, mod, norm_w, o_f, o_b, onw, bd256, z, o_att, h_f, h_b, xy, w_gate, w_pa, w_pb, w_pc, w_o, *,
                layer, mod_off, tm=512):
    B, T, D = x.shape
    tm = min(tm, T)
    im = lambda b, t: (b, t, 0)
    wspec = lambda w: pl.BlockSpec((None,) + w.shape[1:], lambda b, t: (layer, 0, 0), pipeline_mode=pl.Buffered(1))
    return pl.pallas_call(
        _merge_kernel,
        grid=(B, T // tm),
        in_specs=[pl.BlockSpec((None, tm, D), im),
                  pl.BlockSpec((None, None, N_MOD, D), lambda b, t: (layer, b + mod_off, 0, 0)),
                  pl.BlockSpec((None, 3, D), lambda b, t: (layer, 0, 0)),
                  pl.BlockSpec((None, tm, 256), im), pl.BlockSpec((None, tm, 256), im),
                  pl.BlockSpec((None, 1, 256), lambda b, t: (layer, 0, 0)),
                  pl.BlockSpec((256, 256), lambda b, t: (0, 0)),
                  pl.BlockSpec((None, tm, 256), im),
                  pl.BlockSpec((None, tm, 512), im),
                  pl.BlockSpec((None, tm, 256), im), pl.BlockSpec((None, tm, 256), im), pl.BlockSpec((None, tm, 512), im),
                  wspec(w_gate), wspec(w_pa), wspec(w_pb), wspec(w_pc), wspec(w_o)],
        out_specs=pl.BlockSpec((None, tm, D), im),
        out_shape=jax.ShapeDtypeStruct((B, T, D), F32),
        compiler_params=_params("parallel", "parallel"),
        name="merge",
    )(x, mod, norm_w, o_f, o_b, onw, bd256, z, o_att, h_f, h_b, xy, w_gate, w_pa, w_pb, w_pc, w_o)


def _blockdiag_ones(n, seg):
    i = np.arange(n)
    return jnp.asarray((i[:, None] // seg) == (i[None, :] // seg), BF16)


def _placement():
    G = ATT_Q_HEADS // ATT_KV_HEADS
    e = np.zeros((G, 2 * HEAD_DIM, G * HEAD_DIM), np.float32)
    d = np.arange(HEAD_DIM)
    for hh in range(G):
        e[hh, d, hh * HEAD_DIM + d] = 1.0
    return jnp.asarray(e, BF16)


def _with_ones_column(v):
    one = jnp.ones(v.shape[:-1] + (1,), v.dtype)
    zero = jnp.zeros(v.shape[:-1] + (HEAD_DIM - 1,), v.dtype)
    return jnp.concatenate([v, one, zero], axis=-1).astype(BF16)


def _rope_tables(n_tokens):
    rows = n_tokens // GRID_W
    row = jnp.broadcast_to(jnp.arange(rows, dtype=F32)[:, None], (rows, GRID_W)).reshape(-1)
    col = jnp.broadcast_to(jnp.arange(GRID_W, dtype=F32)[None, :], (rows, GRID_W)).reshape(-1)
    freqs = ROPE_BASE ** (-jnp.arange(ROPE_PAIRS, dtype=F32) / ROPE_PAIRS)
    ang = jnp.stack([row[:, None] * freqs, col[:, None] * freqs], axis=1)
    cos = jnp.cos(ang)[:, :, None, :]
    sin = jnp.sin(ang)[:, :, None, :]
    c = jnp.broadcast_to(cos, (n_tokens, 2, 2, ROPE_PAIRS)).reshape(n_tokens, HEAD_DIM)
    s = jnp.concatenate([-sin, sin], axis=2).reshape(n_tokens, HEAD_DIM)
    return jnp.concatenate([c, c], axis=1), jnp.concatenate([s, s], axis=1)


def _ba_layout(ba, n):
    B = ba.shape[0]
    x = ba.reshape(B, 2, 2, 2, 2, n, DN_CHUNK)
    x = x.transpose(0, 3, 5, 1, 2, 4, 6)
    return x.reshape(B, 2, n, 4, 2 * DN_CHUNK)


def _state_to_blockdiag(s):
    B = s.shape[0]
    x = s.reshape(B, 2, 2, 2, DN_DK, DN_DK)
    z = jnp.zeros_like(x[:, :, :, 0])
    top = jnp.concatenate([x[:, :, :, 0], z], axis=-1)
    bot = jnp.concatenate([z, x[:, :, :, 1]], axis=-1)
    return jnp.concatenate([top, bot], axis=-2).transpose(0, 2, 1, 3, 4)


def _blockdiag_to_state(sb):
    B = sb.shape[0]
    x = sb.transpose(0, 2, 1, 3, 4)
    h0 = x[..., 0:DN_DK, 0:DN_DK]
    h1 = x[..., DN_DK:, DN_DK:]
    return jnp.stack([h0, h1], axis=3).reshape(B, 2, DN_HEADS, DN_DK, DN_DK)


def _lru_blockdiag(w):
    L = w.shape[0]
    bw = LRU_WIDTH // LRU_BLOCKS
    out = jnp.zeros((L, 2, LRU_WIDTH, LRU_WIDTH), w.dtype)
    for n in range(LRU_BLOCKS):
        out = out.at[:, :, n * bw:(n + 1) * bw, n * bw:(n + 1) * bw].set(w[:, :, n])
    return out


def kernel(x_prompt, x_sample, cache_k, cache_v, state_delta, state_lru, c, c_ctx, w_mod, b_mod, norm_w, ffn1_wgu,
           ffn1_wd, ffn2_wgu, ffn2_wd, w_in, dn_conv_w, dn_a_log, dn_dt_bias, dn_onorm_w, att_qnorm_w, att_knorm_w,
           lru_conv_w, lru_conv_b, lru_wr, lru_br, lru_wi, lru_bi, lru_lam, w_pa, w_pb, w_pc, w_o):
    NB, SEQ, D = x_prompt.shape
    DB, DSEQ, _ = x_sample.shape
    L = w_mod.shape[0]
    PAST = cache_k.shape[2]
    TC = NB * SEQ

    cond = jnp.zeros((16, D), F32).at[0].set(c_ctx).at[1:1 + DB].set(c)
    mod = _mod_call(cond, w_mod, b_mod).reshape(L, 16, N_MOD, D)

    bf = lambda w: w.astype(BF16)
    ffn1_wgu, ffn1_wd, ffn2_wgu, ffn2_wd = bf(ffn1_wgu), bf(ffn1_wd), bf(ffn2_wgu), bf(ffn2_wd)
    w_main = bf(jnp.concatenate([w_in[:, :, 1040:1808], w_in[:, :, 0:1024], w_in[:, :, 1808:2320]], axis=-1))
    w_ba = bf(jnp.swapaxes(w_in[:, :, 1024:1040], 1, 2))
    w_gate = bf(w_in[:, :, 2320:])
    w_pa, w_pb, w_pc, w_o = bf(w_pa), bf(w_pb), bf(w_pc), bf(w_o)
    qw = jnp.tile(att_qnorm_w, (1, ATT_Q_HEADS)).reshape(L, 1, ATT_Q_HEADS * HEAD_DIM)
    kw = jnp.tile(att_knorm_w, (1, ATT_KV_HEADS)).reshape(L, 1, ATT_KV_HEADS * HEAD_DIM)
    onw = jnp.tile(dn_onorm_w, (1, DN_HEADS)).reshape(L, 1, DN_HEADS * DN_DK)
    pr = lambda p: jnp.repeat(p.reshape(L, 2, 2, 2), DN_CHUNK, axis=-1).reshape(L, 2, 2, 2 * DN_CHUNK).transpose(0, 2, 1, 3)
    dn_prm = jnp.concatenate([pr(dn_a_log), pr(dn_dt_bias)], axis=2)
    wr_bd, wi_bd = bf(_lru_blockdiag(lru_wr)), bf(_lru_blockdiag(lru_wi))
    lru_cb = lru_conv_b.reshape(L, 1, LRU_WIDTH)
    bd512 = _blockdiag_ones(512, HEAD_DIM)
    bd256 = _blockdiag_ones(256, DN_DK)
    place = _placement()
    rope_tabs = _rope_tables(DSEQ)
    cache_kT = bf(cache_k.transpose(0, 1, 3, 4, 2))
    cache_v1 = _with_ones_column(cache_v.transpose(0, 1, 3, 2, 4))
    s0_lat = _state_to_blockdiag(state_delta.transpose(1, 0, 2, 3, 4, 5).reshape(L * DB, 2, DN_HEADS, DN_DK, DN_DK))
    s0_lat = s0_lat.reshape(L, DB, 2, 2, 2 * DN_DK, 2 * DN_DK)
    s0_ctx = jnp.zeros((NB, 2, 2, 2 * DN_DK, 2 * DN_DK), F32)
    h0_ctx = jnp.zeros((NB, 2, LRU_WIDTH), F32)

    xp = x_prompt.reshape(1, TC, D)
    xs = x_sample
    new_k, new_v, new_sd, new_sl = [], [], [], []
    for l in range(L):
        for ctx in (True, False):
            x = xp if ctx else xs
            mod_off = 0 if ctx else 1
            seq = SEQ if ctx else DSEQ
            nseq = NB if ctx else DB
            shp = x.shape[:2]
            x = _ffn_call(x, mod, norm_w, ffn1_wgu, ffn1_wd, layer=l, sub=0, mod_off=mod_off)
            outs = _inproj_call(x, mod, norm_w, w_main, w_ba, qw, kw, bd512, None if ctx else rope_tabs,
                                layer=l, mod_off=mod_off, emit_kv=ctx)
            qkv, z, xy, q_hm, kT, v_bf, ba = outs[:7]
            n = seq // DN_CHUNK
            ba_l = _ba_layout(ba.reshape(shp[0], 16, -1, seq).transpose(0, 2, 1, 3).reshape(nseq, 16, seq), n)
            ops = _dnchunk_call(qkv.reshape(nseq, seq, 768), dn_conv_w, ba_l, dn_prm, layer=l)
            o_f, o_b, s_fin = _dnscan_call(ops, s0_ctx if ctx else s0_lat[l], T=seq)
            if ctx:
                o_att = _attn_call(
                    q_hm, [(kT, v_bf, SEQ)], place, grid=(NB, 1),
                    q_map=lambda s, t: (0, 0, s, 0),
                    seg_maps=[(lambda s, t: (0, 0, 0, s), lambda s, t: (0, 0, s, 0))],
                    out_map=lambda s, t: (0, s, 0), out_rows=(1, TC), tq=SEQ, kb=512)
            else:
                tq = 128
                o_att = _attn_call(
                    q_hm, [(cache_kT, cache_v1, PAST), (kT, v_bf, DSEQ)], place, grid=(DB, DSEQ // tq),
                    q_map=lambda b, t: (b, 0, t, 0),
                    seg_maps=[(lambda b, t: (b, l, 0, 0, 0), lambda b, t: (b, l, 0, 0, 0)),
                              (lambda b, t: (b, 0, 0, 0), lambda b, t: (b, 0, 0, 0))],
                    out_map=lambda b, t: (b, t, 0), out_rows=(DB, DSEQ), tq=tq, kb=512)
            if ctx:
                h_f, h_b, h_fin = _lru_call(xy.reshape(nseq, seq, 512), lru_conv_w, lru_cb, wr_bd, lru_br, wi_bd,
                                            lru_bi, lru_lam, h0_ctx, layer=l, h0_layer=None)
            else:
                h_f, h_b, h_fin = _lru_call(xy, lru_conv_w, lru_cb, wr_bd, lru_br, wi_bd, lru_bi, lru_lam, state_lru,
                                            layer=l, h0_layer=l)
            x = _merge_call(x, mod, norm_w, o_f.reshape(shp + (256,)), o_b.reshape(shp + (256,)), onw, bd256, z, o_att,
                            h_f.reshape(shp + (256,)), h_b.reshape(shp + (256,)), xy, w_gate, w_pa, w_pb, w_pc, w_o,
                            layer=l, mod_off=mod_off)
            x = _ffn_call(x, mod, norm_w, ffn2_wgu, ffn2_wd, layer=l, sub=2, mod_off=mod_off)
            if ctx:
                xp = x
                kf, vf = outs[7:9]
                new_k.append(kf.reshape(NB, SEQ, ATT_KV_HEADS, HEAD_DIM))
                new_v.append(vf.reshape(NB, SEQ, ATT_KV_HEADS, HEAD_DIM))
                new_sd.append(_blockdiag_to_state(s_fin))
                new_sl.append(h_fin)
            else:
                xs = x
    return (xp.reshape(NB, SEQ, D), xs, jnp.stack(new_k, axis=1), jnp.stack(new_v, axis=1),
            jnp.stack(new_sd, axis=1), jnp.stack(new_sl, axis=1))
```

```python
import functools
import math

import numpy as np
import jax
import jax.numpy as jnp
from jax import lax
from jax.experimental import pallas as pl
from jax.experimental.pallas import tpu as pltpu

F32 = jnp.float32
BF16 = jnp.bfloat16

EPS = 1e-6
GRID_W = 64
HEAD_DIM = 64
ATT_Q_HEADS = 8
ATT_KV_HEADS = 2
ROPE_BASE = 10000.0
ROPE_PAIRS = HEAD_DIM // 4
DN_HEADS = 4
DN_DK = 64
DN_CHUNK = 64
LRU_WIDTH = 256
LRU_BLOCKS = 4
LRU_C = 8.0
N_MOD = 9
LANES = 128
V7X_VMEM_LIMIT_BYTES = 56 * 1024 * 1024


def _params(*sem):
    return pltpu.CompilerParams(dimension_semantics=sem, vmem_limit_bytes=V7X_VMEM_LIMIT_BYTES)


def _mm(a, b):
    return jnp.dot(a.astype(BF16), b.astype(BF16), preferred_element_type=F32)


def _mm_nt(a, b):
    return lax.dot_general(a.astype(BF16), b.astype(BF16), (((1,), (1,)), ((), ())), preferred_element_type=F32)


def _mm_tn(a, b):
    return lax.dot_general(a.astype(BF16), b.astype(BF16), (((0,), (0,)), ((), ())), preferred_element_type=F32)


def _split3(x):
    hi = x.astype(BF16)
    r = x - hi.astype(F32)
    mid = r.astype(BF16)
    lo = (r - mid.astype(F32)).astype(BF16)
    return hi, mid, lo


def _mm01(x, m01):
    hi, mid, lo = _split3(x)
    d = functools.partial(jnp.dot, preferred_element_type=F32)
    return d(hi, m01) + d(mid, m01) + d(lo, m01)


def _mm01x2(x, m01):
    hi = x.astype(BF16)
    lo = (x - hi.astype(F32)).astype(BF16)
    return jnp.dot(hi, m01, preferred_element_type=F32) + jnp.dot(lo, m01, preferred_element_type=F32)


def _mm01_nt(m01, x):
    hi, mid, lo = _split3(x)
    d = functools.partial(lax.dot_general, dimension_numbers=(((1,), (1,)), ((), ())), preferred_element_type=F32)
    return d(m01, hi) + d(m01, mid) + d(m01, lo)


def _mm3(a, b):
    n = a.shape[0]
    ah = a.astype(BF16)
    al = (a - ah.astype(F32)).astype(BF16)
    bh = b.astype(BF16)
    bl = (b - bh.astype(F32)).astype(BF16)
    top = jnp.dot(jnp.concatenate([ah, al], axis=0), bh, preferred_element_type=F32)
    return (top[0:n] + top[n:2 * n]) + jnp.dot(ah, bl, preferred_element_type=F32)


def _mm3_many(As, Bs):
    n = As[0].shape[0]
    ah = [a.astype(BF16) for a in As]
    bh = [b.astype(BF16) for b in Bs]
    al = [(a - h.astype(F32)).astype(BF16) for a, h in zip(As, ah)]
    bl = [(b - h.astype(F32)).astype(BF16) for b, h in zip(Bs, bh)]
    top = [jnp.dot(jnp.concatenate([h, l], axis=0), b, preferred_element_type=F32) for h, l, b in zip(ah, al, bh)]
    low = [jnp.dot(h, b, preferred_element_type=F32) for h, b in zip(ah, bl)]
    return [(t[0:n] + t[n:2 * n]) + w for t, w in zip(top, low)]


def _silu(x):
    return x * jax.nn.sigmoid(x)


def _softplus(x):
    return jnp.maximum(x, 0.0) + jnp.log1p(jnp.exp(-jnp.abs(x)))


def _gelu_tanh(x):
    return x * (0.5 * (1.0 + jnp.tanh(0.7978845608028654 * (x + 0.044715 * (x * x * x)))))


def _norm_mod(x, nw, scale, shift):
    ms = jnp.mean(x * x, axis=-1, keepdims=True)
    y = (x * lax.rsqrt(ms + EPS)) * nw
    return y * (1.0 + scale) + shift


def _seg_masks(n, seg):
    r = lax.broadcasted_iota(jnp.int32, (n, n), 0)
    c = lax.broadcasted_iota(jnp.int32, (n, n), 1)
    return r, c, (r // seg) == (c // seg)


def _swap16(x):
    w = x.shape[1]
    lane = lax.broadcasted_iota(jnp.int32, x.shape, 1)
    return jnp.where((lane & 16) == 0, pltpu.roll(x, w - 16, axis=1), pltpu.roll(x, 16, axis=1))


def _mod_kernel(c_ref, w_ref, b_ref, o_ref):
    o_ref[...] = _mm(_silu(c_ref[...]), w_ref[...]) + b_ref[...]


def _mod_call(cond, w_mod, b_mod):
    L, D, N = w_mod.shape
    R = cond.shape[0]
    tn = D
    return pl.pallas_call(
        _mod_kernel,
        grid=(L, N // tn),
        in_specs=[pl.BlockSpec((R, D), lambda l, j: (0, 0)),
                  pl.BlockSpec((None, D, tn), lambda l, j: (l, 0, j)),
                  pl.BlockSpec((None, 1, tn), lambda l, j: (l, 0, j))],
        out_specs=pl.BlockSpec((None, R, tn), lambda l, j: (l, 0, j)),
        out_shape=jax.ShapeDtypeStruct((L, R, N), F32),
        compiler_params=_params("parallel", "parallel"),
        name="mod",
    )(cond, w_mod, b_mod.reshape(L, 1, N))


def _ffn_kernel(x_ref, mod_ref, nw_ref, wgu_ref, wd_ref, o_ref, *, sub, F, nc):
    x = x_ref[...]
    h = _norm_mod(x, nw_ref[sub:sub + 1, :], mod_ref[3 * sub + 1:3 * sub + 2, :],
                  mod_ref[3 * sub:3 * sub + 1, :]).astype(BF16)
    dot = functools.partial(jnp.dot, preferred_element_type=F32)
    cf = F // nc

    def gate_up(c):
        return (dot(h, wgu_ref[:, c * cf:(c + 1) * cf]), dot(h, wgu_ref[:, F + c * cf:F + (c + 1) * cf]))

    nxt = gate_up(0)
    acc = None
    for c in range(nc):
        g, u = nxt
        if c + 1 < nc:
            nxt = gate_up(c + 1)
        part = dot((_silu(g) * u).astype(BF16), wd_ref[c * cf:(c + 1) * cf, :])
        acc = part if acc is None else acc + part
    o_ref[...] = x + (0.5 * mod_ref[3 * sub + 2:3 * sub + 3, :]) * acc


def _ffn_call(x, mod, norm_w, w_gu, w_d, *, layer, sub, mod_off, tm=1024):
    B, T, D = x.shape
    F = w_d.shape[1]
    tm = min(tm, T)
    nc = F // 256 if F % 256 == 0 else 1
    return pl.pallas_call(
        functools.partial(_ffn_kernel, sub=sub, F=F, nc=nc),
        grid=(B, T // tm),
        in_specs=[pl.BlockSpec((None, tm, D), lambda b, t: (b, t, 0)),
                  pl.BlockSpec((None, None, N_MOD, D), lambda b, t: (layer, b + mod_off, 0, 0)),
                  pl.BlockSpec((None, 3, D), lambda b, t: (layer, 0, 0)),
                  pl.BlockSpec((None, D, 2 * F), lambda b, t: (layer, 0, 0), pipeline_mode=pl.Buffered(1)),
                  pl.BlockSpec((None, F, D), lambda b, t: (layer, 0, 0), pipeline_mode=pl.Buffered(1))],
        out_specs=pl.BlockSpec((None, tm, D), lambda b, t: (b, t, 0)),
        out_shape=jax.ShapeDtypeStruct((B, T, D), F32),
        compiler_params=_params("parallel", "parallel"),
        name="ffn",
    )(x, mod, norm_w, w_gu, w_d)


def _inproj_kernel(*refs, rope, emit_kv):
    (x_ref, mod_ref, nw_ref, w_ref, wba_ref, qw_ref, kw_ref, bd_ref) = refs[:8]
    refs = refs[8:]
    if rope:
        rc_ref, rs_ref = refs[:2]
        refs = refs[2:]
    qkv_ref, z_ref, xy_ref, q_ref, kT_ref, v_ref, ba_ref = refs[:7]
    refs = refs[7:]

    h = _norm_mod(x_ref[...], nw_ref[1:2, :], mod_ref[4:5, :], mod_ref[3:4, :]).astype(BF16)
    ya = jnp.dot(h, w_ref[:, 0:768], preferred_element_type=F32)
    aq = ya[:, 0:512]
    ak = ya[:, 512:640]
    v = ya[:, 640:768]
    bd = bd_ref[...]
    inv_hd = 1.0 / HEAD_DIM
    ssq = _mm01x2(aq * aq, bd)
    ssk = _mm01x2(ak * ak, bd[0:128, 0:128])
    ba_ref[...] = lax.dot_general(wba_ref[...], h, (((1,), (1,)), ((), ())), preferred_element_type=F32)
    y = jnp.dot(h, w_ref[:, 768:2304], preferred_element_type=F32)
    qkv_ref[...] = y[:, 0:768]
    z_ref[...] = y[:, 768:1024]
    xy_ref[...] = y[:, 1024:1536]
    qn = (aq * lax.rsqrt(ssq * inv_hd + EPS)) * qw_ref[...]
    kn = (ak * lax.rsqrt(ssk * inv_hd + EPS)) * kw_ref[...]
    if emit_kv:
        kf_ref, vf_ref = refs
        kf_ref[...] = kn
        vf_ref[...] = v
    if rope:
        rc = rc_ref[...]
        rs = rs_ref[...]
        kn = kn * rc + _swap16(kn) * rs
        qn = qn * jnp.concatenate([rc] * 4, axis=1) + _swap16(qn) * jnp.concatenate([rs] * 4, axis=1)
    qs = qn * (HEAD_DIM ** -0.5 * math.log2(math.e))
    for hh in range(ATT_Q_HEADS):
        q_ref[hh] = qs[:, hh * HEAD_DIM:(hh + 1) * HEAD_DIM].astype(BF16)
    kT = kn.T
    kT_ref[0] = kT[0:HEAD_DIM, :].astype(BF16)
    kT_ref[1] = kT[HEAD_DIM:2 * HEAD_DIM, :].astype(BF16)
    lane = lax.broadcasted_iota(jnp.int32, v.shape, 1)
    tail = jnp.where(lane == HEAD_DIM, 1.0, 0.0)
    v_ref[0] = jnp.where(lane < HEAD_DIM, v, tail).astype(BF16)
    v_ref[1] = jnp.where(lane < HEAD_DIM, pltpu.roll(v, HEAD_DIM, axis=1), tail).astype(BF16)


def _inproj_call(x, mod, norm_w, w_main, w_ba, qw, kw, bd512, rope_tabs, *, layer, mod_off, emit_kv, tm=512):
    B, T, D = x.shape
    tm = min(tm, T)
    NW = w_main.shape[2]
    rope = rope_tabs is not None
    im = lambda b, t: (b, t, 0)
    in_specs = [pl.BlockSpec((None, tm, D), im),
                pl.BlockSpec((None, None, N_MOD, D), lambda b, t: (layer, b + mod_off, 0, 0)),
                pl.BlockSpec((None, 3, D), lambda b, t: (layer, 0, 0)),
                pl.BlockSpec((None, D, NW), lambda b, t: (layer, 0, 0)),
                pl.BlockSpec((None, 16, D), lambda b, t: (layer, 0, 0)),
                pl.BlockSpec((None, 1, 512), lambda b, t: (layer, 0, 0)),
                pl.BlockSpec((None, 1, 128), lambda b, t: (layer, 0, 0)),
                pl.BlockSpec((512, 512), lambda b, t: (0, 0))]
    args = [x, mod, norm_w, w_main, w_ba, qw, kw, bd512]
    if rope:
        in_specs += [pl.BlockSpec((tm, 128), lambda b, t: (t, 0))] * 2
        args += list(rope_tabs)
    out_shape = [jax.ShapeDtypeStruct((B, T, 768), F32), jax.ShapeDtypeStruct((B, T, 256), F32),
                 jax.ShapeDtypeStruct((B, T, 512), F32), jax.ShapeDtypeStruct((B, ATT_Q_HEADS, T, HEAD_DIM), BF16),
                 jax.ShapeDtypeStruct((B, ATT_KV_HEADS, HEAD_DIM, T), BF16),
                 jax.ShapeDtypeStruct((B, ATT_KV_HEADS, T, 128), BF16),
                 jax.ShapeDtypeStruct((B, 16, T), F32)]
    out_specs = [pl.BlockSpec((None, tm, 768), im), pl.BlockSpec((None, tm, 256), im), pl.BlockSpec((None, tm, 512), im),
                 pl.BlockSpec((None, ATT_Q_HEADS, tm, HEAD_DIM), lambda b, t: (b, 0, t, 0)),
                 pl.BlockSpec((None, ATT_KV_HEADS, HEAD_DIM, tm), lambda b, t: (b, 0, 0, t)),
                 pl.BlockSpec((None, ATT_KV_HEADS, tm, 128), lambda b, t: (b, 0, t, 0)),
                 pl.BlockSpec((None, 16, tm), lambda b, t: (b, 0, t))]
    if emit_kv:
        out_shape += [jax.ShapeDtypeStruct((B, T, 128), F32)] * 2
        out_specs += [pl.BlockSpec((None, tm, 128), im)] * 2
    return pl.pallas_call(
        functools.partial(_inproj_kernel, rope=rope, emit_kv=emit_kv),
        grid=(B, T // tm),
        in_specs=in_specs, out_specs=out_specs, out_shape=out_shape,
        compiler_params=_params("parallel", "parallel"),
        name="inproj",
    )(*args)


def _conv_window(main, prev, nxt, at_start, at_end):
    prev = jnp.where(at_start, 0.0, prev)
    nxt = jnp.where(at_end, 0.0, nxt)
    return jnp.concatenate([prev, main, nxt], axis=0)


def _conv4(win, w, n):
    return (w[0:1, :] * win[6:6 + n] + w[1:2, :] * win[7:7 + n]) + (w[2:3, :] * win[8:8 + n] + w[3:4, :] * win[9:9 + n])


def _dnchunk_kernel(q_ref, qp_ref, qn_ref, k_ref, kp_ref, kn_ref, v_ref, vp_ref, vn_ref, cwq_ref, cwk_ref, cwv_ref,
                    ba_ref, prm_ref, p_ref, qm_ref, o1_ref, o2_ref, gs_ref, *, G):
    C = DN_CHUNK
    W = 2 * C
    r, c, same = _seg_masks(W, C)
    i_loc = r % C
    j_loc = c % C
    m01 = lambda mask: jnp.where(mask, 1.0, 0.0).astype(BF16)
    cum_f = m01(same & (i_loc <= j_loc))
    cum_b = m01(same & (i_loc >= j_loc))
    ones_bd = m01(same)
    eye = m01(r == c)
    eye_f = jnp.where(r == c, 1.0, 0.0)
    blk = tuple((r // s) == (c // s) for s in (8, 16, 32, 64))
    incl = (same & (i_loc >= j_loc), same & (i_loc <= j_loc))
    strict = (same & (i_loc > j_loc), same & (i_loc < j_loc))
    head0 = lax.broadcasted_iota(jnp.int32, (C, W), 1) < C
    row2 = lax.broadcasted_iota(jnp.int32, (2, W), 0)
    alog = prm_ref[0:2, :]
    dtb = prm_ref[2:4, :]

    jblk = pl.program_id(2)
    at_start = jblk == 0
    at_end = jblk == pl.num_programs(2) - 1

    def conv_silu(main_ref, prev_ref, next_ref, cw_ref):
        win = _conv_window(main_ref[...], prev_ref[...], next_ref[...], at_start, at_end)
        return _silu(_conv4(win, cw_ref[...], G * C))

    def l2norm(x):
        return x * lax.rsqrt(_mm01(x * x, ones_bd) + EPS)

    q_all = l2norm(conv_silu(q_ref, qp_ref, qn_ref, cwq_ref)) * (DN_DK ** -0.5)
    k_all = l2norm(conv_silu(k_ref, kp_ref, kn_ref, cwk_ref))
    v_all = conv_silu(v_ref, vp_ref, vn_ref, cwv_ref)

    def stack(x):
        return jnp.concatenate([jnp.where(head0, x, 0.0), jnp.where(head0, 0.0, x)], axis=0)

    chunks = range(G)
    bg = [ba_ref[gi] for gi in chunks]
    beta = [jax.nn.sigmoid(b[0:2, :]) for b in bg]
    g = [-jnp.exp(alog) * _softplus(b[2:4, :] + dtb) for b in bg]
    gcf = [_mm01(x, cum_f) for x in g]
    gcb = [_mm01(x, cum_b) for x in g]
    tot = [_mm01(x, ones_bd) for x in g]
    rows = [jnp.concatenate([beta[i], jnp.where(row2 == 0, gcf[i], gcb[i]), tot[i], jnp.zeros((2, W), F32)], axis=0)
            for i in chunks]
    cols = [_mm01_nt(eye, x) for x in rows]
    Kst = [stack(k_all[gi * C:(gi + 1) * C, :]) for gi in chunks]
    Qst = [stack(q_all[gi * C:(gi + 1) * C, :]) for gi in chunks]
    Vst = [stack(v_all[gi * C:(gi + 1) * C, :]) for gi in chunks]
    kq = [_mm_nt(jnp.concatenate([Kst[i], Qst[i]], axis=0), Kst[i]) for i in chunks]

    chains = [(gi, d) for gi in chunks for d in range(2)]
    col = lambda gi, k: cols[gi][:, k:k + 1]
    dec = [jnp.exp(jnp.where(incl[d], col(gi, 2 + d) - rows[gi][2 + d:3 + d, :], -jnp.inf)) for gi, d in chains]
    L = [(col(gi, d) * kq[gi][0:W]) * jnp.where(strict[d], dec[i], 0.0) for i, (gi, d) in enumerate(chains)]
    QK = [kq[gi][W:2 * W] * dec[i] for i, (gi, d) in enumerate(chains)]
    dot = functools.partial(jnp.dot, preferred_element_type=F32)
    D8f = [jnp.where(blk[0], x, 0.0) for x in L]
    D8 = [x.astype(BF16) for x in D8f]
    M = [dot(d8, d8) for d8 in D8]
    Mb = [m.astype(BF16) for m in M]
    DM = [dot(d8, mb) for d8, mb in zip(D8, Mb)]
    R = [(m - d8) - dm for m, d8, dm in zip(M, D8f, DM)]
    M = [dot(mb, mb) for mb in Mb]
    RM = [dot(r_.astype(BF16), m.astype(BF16)) for r_, m in zip(R, M)]
    Tm = [eye_f + ((r_ + m) + rm) for r_, m, rm in zip(R, M, RM)]
    for lvl in range(3):
        off = blk[lvl + 1] & jnp.logical_not(blk[lvl])
        Th = [t.astype(BF16) for t in Tm]
        Bm = [jnp.where(off, x, 0.0) for x in L]
        Bh = [b.astype(BF16) for b in Bm]
        Bl = [(b - h.astype(F32)).astype(BF16) for b, h in zip(Bm, Bh)]
        TB2 = [dot(th, jnp.concatenate([bh, bl], axis=1)) for th, bh, bl in zip(Th, Bh, Bl)]
        TB = [x[:, 0:W] + x[:, W:2 * W] for x in TB2]
        TBh = [x.astype(BF16) for x in TB]
        TBl = [(x - h.astype(F32)).astype(BF16) for x, h in zip(TB, TBh)]
        TBT2 = [dot(jnp.concatenate([h, lo], axis=0), th) for h, lo, th in zip(TBh, TBl, Th)]
        Tm = [t - (x[0:W] + x[W:2 * W]) for t, x in zip(Tm, TBT2)]
    rhs = [jnp.concatenate([(col(gi, d) * jnp.exp(col(gi, 2 + d))) * Kst[gi], col(gi, d) * Vst[gi]], axis=1)
           for gi, d in chains]
    X = _mm3_many(Tm, rhs)
    kd = [Kst[gi] * jnp.exp(col(gi, 4 + d) - col(gi, 2 + d)) for gi, d in chains]
    PQ = [_mm_tn(a, x) for a, x in zip(kd, X)]
    OO = [_mm(a, x) for a, x in zip(QK, X)]
    for i, (gi, d) in enumerate(chains):
        o1 = Qst[gi] * jnp.exp(col(gi, 2 + d)) - OO[i][:, 0:W]
        o2 = OO[i][:, W:2 * W]
        p_ref[d, gi] = PQ[i][:, 0:W].astype(BF16)
        qm_ref[d, gi] = PQ[i][:, W:2 * W]
        o1_ref[d, gi] = (o1[0:C] + o1[C:W]).astype(BF16)
        o2_ref[d, gi] = o2[0:C] + o2[C:W]
        gs_ref[d, gi] = jnp.exp(rows[gi][4 + d:5 + d, :])


def _dnchunk_call(qkv, conv_w, ba, prm, *, layer, G=8):
    B, T, _ = qkv.shape
    C = DN_CHUNK
    W = 2 * C
    n = T // C
    G = min(G, n)
    rb = G * C // 8
    nb8 = T // 8
    sds = jax.ShapeDtypeStruct
    mat = lambda rows, dt: (sds((B, 2, 2, n, rows, W), dt),
                            pl.BlockSpec((None, None, 2, G, rows, W), lambda b, p, j: (b, p, 0, j, 0, 0)))
    outs = [mat(W, BF16), mat(W, F32), mat(C, BF16), mat(C, F32), mat(1, F32)]
    in_specs = []
    for part in range(3):
        in_specs += [pl.BlockSpec((None, G * C, W), lambda b, p, j, part=part: (b, j, 2 * part + p)),
                     pl.BlockSpec((None, 8, W), lambda b, p, j, part=part: (b, jnp.maximum(j * rb - 1, 0), 2 * part + p)),
                     pl.BlockSpec((None, 8, W),
                                  lambda b, p, j, part=part: (b, jnp.minimum((j + 1) * rb, nb8 - 1), 2 * part + p))]
    in_specs += [pl.BlockSpec((None, 4, W), lambda b, p, j, part=part: (layer, 0, 2 * part + p)) for part in range(3)]
    in_specs += [pl.BlockSpec((None, None, G, 4, W), lambda b, p, j: (b, p, j, 0, 0)),
                 pl.BlockSpec((None, None, 4, W), lambda b, p, j: (layer, p, 0, 0))]
    return pl.pallas_call(
        functools.partial(_dnchunk_kernel, G=G),
        grid=(B, 2, n // G),
        in_specs=in_specs,
        out_specs=[o[1] for o in outs],
        out_shape=[o[0] for o in outs],
        compiler_params=_params("parallel", "parallel", "parallel"),
        name="dnchunk",
    )(*([qkv] * 9), conv_w, conv_w, conv_w, ba, prm)


def _dnscan_kernel(pf_ref, qf_ref, o1f_ref, o2f_ref, gf_ref, pb_ref, qb_ref, o1b_ref, o2b_ref, gb_ref, s0_ref,
                   of_ref, ob_ref, sfin_ref, s_sc, *, Gs):
    C = DN_CHUNK
    W = 2 * C
    j = pl.program_id(1)

    @pl.when(j == 0)
    def _():
        s_sc[...] = s0_ref[...]

    S = [[s_sc[p, d] for d in range(2)] for p in range(2)]
    fwd = (pf_ref, qf_ref, o1f_ref, o2f_ref, gf_ref, of_ref)
    bwd = (pb_ref, qb_ref, o1b_ref, o2b_ref, gb_ref, ob_ref)
    for i in range(Gs):
        for d, (P, Qm, O1, O2, GS, out) in enumerate((fwd, bwd)):
            ci = i if d == 0 else Gs - 1 - i
            for p in range(2):
                Sb = S[p][d].astype(BF16)
                out[ci * C:(ci + 1) * C, p * W:(p + 1) * W] = (
                    jnp.dot(O1[p, ci], Sb, preferred_element_type=F32) + O2[p, ci])
                S[p][d] = (GS[p, ci] * S[p][d] - jnp.dot(P[p, ci], Sb, preferred_element_type=F32)) + Qm[p, ci]
    for p in range(2):
        for d in range(2):
            s_sc[p, d] = S[p][d]

    @pl.when(j == pl.num_programs(1) - 1)
    def _():
        sfin_ref[...] = s_sc[...]


def _dnscan_call(ops, s0, *, T, Gs=8):
    B = s0.shape[0]
    C = DN_CHUNK
    W = 2 * C
    n = T // C
    Gs = min(Gs, n)
    nb = n // Gs
    specs = []
    for d in range(2):
        for a in ops:
            rows = a.shape[4]
            if d == 0:
                specs.append(pl.BlockSpec((None, 2, None, Gs, rows, W), lambda b, j: (b, 0, 0, j, 0, 0)))
            else:
                specs.append(pl.BlockSpec((None, 2, None, Gs, rows, W), lambda b, j: (b, 0, 1, nb - 1 - j, 0, 0)))
    st_spec = pl.BlockSpec((None, 2, 2, W, W), lambda b, j: (b, 0, 0, 0, 0))
    return pl.pallas_call(
        functools.partial(_dnscan_kernel, Gs=Gs),
        grid=(B, nb),
        in_specs=specs + [st_spec],
        out_specs=[pl.BlockSpec((None, Gs * C, 2 * W), lambda b, j: (b, j, 0)),
                   pl.BlockSpec((None, Gs * C, 2 * W), lambda b, j: (b, nb - 1 - j, 0)),
                   st_spec],
        out_shape=[jax.ShapeDtypeStruct((B, T, 2 * W), F32), jax.ShapeDtypeStruct((B, T, 2 * W), F32),
                   jax.ShapeDtypeStruct((B, 2, 2, W, W), F32)],
        scratch_shapes=[pltpu.VMEM((2, 2, W, W), F32)],
        compiler_params=_params("parallel", "arbitrary"),
        name="dnscan",
    )(*ops, *ops, s0)


def _attn_kernel(*refs, seg_blocks, tq, rt):
    q_ref = refs[0]
    nseg = len(seg_blocks)
    e_ref, o_ref = refs[1 + 2 * nseg:3 + 2 * nseg]
    G = ATT_Q_HEADS // ATT_KV_HEADS
    rows = G * tq
    rt = min(rt, rows)
    for g in range(ATT_KV_HEADS):
        q = q_ref[g * G:(g + 1) * G].reshape(rows, HEAD_DIM)
        m = None
        acc = None
        for si, (nblk, kb) in enumerate(seg_blocks):
            for j in range(nblk):
                kT = refs[1 + 2 * si][g, :, j * kb:(j + 1) * kb]
                v = refs[2 + 2 * si][g, j * kb:(j + 1) * kb, :]
                s_all = jnp.dot(q, kT, preferred_element_type=F32)
                S = [s_all[r * rt:(r + 1) * rt] for r in range(rows // rt)]
                smax = [jnp.max(s, axis=-1, keepdims=True) for s in S]
                if m is None:
                    m_new = smax
                else:
                    m_new = [jnp.maximum(a, b) for a, b in zip(m, smax)]
                    alpha = [jnp.exp2(a - b) for a, b in zip(m, m_new)]
                P = [jnp.exp2((s - mn).astype(BF16)) for s, mn in zip(S, m_new)]
                pv = jnp.dot(jnp.concatenate(P, axis=0), v, preferred_element_type=F32)
                acc = pv if acc is None else jnp.concatenate(alpha, axis=0) * acc + pv
                m = m_new
        o = (acc / acc[:, HEAD_DIM:HEAD_DIM + 1]).astype(BF16)
        out = jnp.dot(o[0:tq], e_ref[0], preferred_element_type=F32)
        for hh in range(1, G):
            out = out + jnp.dot(o[hh * tq:(hh + 1) * tq], e_ref[hh], preferred_element_type=F32)
        o_ref[:, g * G * HEAD_DIM:(g + 1) * G * HEAD_DIM] = out.astype(BF16)


def _attn_call(q, segs, place, *, grid, q_map, seg_maps, out_map, out_rows, tq, kb, rt=512):
    G = ATT_Q_HEADS // ATT_KV_HEADS
    in_specs = [pl.BlockSpec((None, ATT_Q_HEADS, tq, HEAD_DIM), q_map)]
    args = [q]
    seg_blocks = []
    for (kT, v, S), (k_map, v_map) in zip(segs, seg_maps):
        blk = min(kb, S)
        seg_blocks.append((S // blk, blk))
        in_specs.append(pl.BlockSpec((None,) * (kT.ndim - 3) + (ATT_KV_HEADS, HEAD_DIM, S), k_map))
        in_specs.append(pl.BlockSpec((None,) * (v.ndim - 3) + (ATT_KV_HEADS, S, 2 * HEAD_DIM), v_map))
        args += [kT, v]
    in_specs.append(pl.BlockSpec((G, 2 * HEAD_DIM, G * HEAD_DIM), lambda b, t: (0, 0, 0)))
    args.append(place)
    return pl.pallas_call(
        functools.partial(_attn_kernel, seg_blocks=tuple(seg_blocks), tq=tq, rt=rt),
        grid=grid,
        in_specs=in_specs,
        out_specs=pl.BlockSpec((None, tq, ATT_Q_HEADS * HEAD_DIM), out_map),
        out_shape=jax.ShapeDtypeStruct(out_rows + (ATT_Q_HEADS * HEAD_DIM,), BF16),
        compiler_params=_params("parallel", "parallel"),
        name="attn",
    )(*args)


def _lru_kernel(xf_ref, xfp_ref, xfn_ref, xb_ref, xbp_ref, xbn_ref, cw_ref, cb_ref, wr_ref, br_ref, wi_ref, bi_ref,
                lam_ref, h0_ref, hf_ref, hb_ref, hfin_ref, h_sc, a_sc, u_sc, *, SB, tt):
    W = LRU_WIDTH
    j = pl.program_id(1)
    last = pl.num_programs(1) - 1
    cw = cw_ref[...]
    cb = cb_ref[...]
    rid = lax.broadcasted_iota(jnp.int32, (8, W), 0)

    @pl.when(j == 0)
    def _():
        for s in range(SB):
            for d in range(2):
                h_sc[2 * s + d] = jnp.broadcast_to(h0_ref[s, d:d + 1, :], (8, W))

    def gates(x_ref, p_ref, n_ref, at_start, at_end, d):
        x = jnp.concatenate(
            [_conv4(_conv_window(x_ref[s, :, 0:W], p_ref[s, :, 0:W], n_ref[s, :, 0:W], at_start, at_end), cw, tt)
             for s in range(SB)], axis=0) + cb
        rg = jax.nn.sigmoid(_mm(x, wr_ref[d]) + br_ref[d:d + 1, :])
        ig = jax.nn.sigmoid(_mm(x, wi_ref[d]) + bi_ref[d:d + 1, :])
        log_a = (-LRU_C * rg) * _softplus(-lam_ref[d:d + 1, :])
        a = jnp.exp(log_a)
        a_sc[d] = a
        u_sc[d] = jnp.sqrt(-jnp.tanh(log_a) * (a * a + 1.0)) * (ig * x)

    gates(xf_ref, xfp_ref, xfn_ref, j == 0, j == last, 0)
    gates(xb_ref, xbp_ref, xbn_ref, j == last, j == 0, 1)

    chains = [(s, d) for s in range(SB) for d in range(2)]
    nblk = tt // 8

    def scan8(j8, hs):
        blk = [j8 if d == 0 else nblk - 1 - j8 for _, d in chains]
        row0 = [pl.multiple_of(s * tt + b * 8, 8) for (s, _), b in zip(chains, blk)]
        a8 = [a_sc[d, pl.ds(r0, 8), :] for (_, d), r0 in zip(chains, row0)]
        u8 = [u_sc[d, pl.ds(r0, 8), :] for (_, d), r0 in zip(chains, row0)]
        hs = list(hs)
        out = [jnp.zeros((8, W), F32)] * len(chains)
        for step in range(8):
            for i, (_, d) in enumerate(chains):
                r = step if d == 0 else 7 - step
                hs[i] = (jnp.broadcast_to(a8[i][r:r + 1, :], (8, W)) * hs[i]
                         + jnp.broadcast_to(u8[i][r:r + 1, :], (8, W)))
                out[i] = jnp.where(rid == r, hs[i], out[i])
        for i, (s, d) in enumerate(chains):
            dst = hf_ref if d == 0 else hb_ref
            dst[s, pl.ds(pl.multiple_of(blk[i] * 8, 8), 8), :] = out[i]
        return tuple(hs)

    hs = lax.fori_loop(0, nblk, scan8, tuple(h_sc[i] for i in range(len(chains))))
    for i in range(len(chains)):
        h_sc[i] = hs[i]

    @pl.when(j == last)
    def _():
        for i, (s, d) in enumerate(chains):
            hfin_ref[s, d:d + 1, :] = hs[i][0:1, :]


def _lru_call(xy, conv_w, conv_b, wr, br, wi, bi, lam, h0, *, layer, h0_layer, SB=4):
    B, T, _ = xy.shape
    W = LRU_WIDTH
    tt = min(256, T)
    nt = T // tt
    SB = min(SB, B)
    rb = tt // 8
    nb8 = T // 8
    lmap = lambda b, j: (layer, 0, 0)
    h0_spec = (pl.BlockSpec((SB, 2, W), lambda b, j: (b, 0, 0)) if h0_layer is None
               else pl.BlockSpec((SB, None, 2, W), lambda b, j: (b, h0_layer, 0, 0)))
    fwd = lambda b, j: j
    bwd = lambda b, j: nt - 1 - j
    tiles = []
    for tile in (fwd, bwd):
        tiles += [pl.BlockSpec((SB, tt, 2 * W), lambda b, j, tile=tile: (b, tile(b, j), 0)),
                  pl.BlockSpec((SB, 8, 2 * W), lambda b, j, tile=tile: (b, jnp.maximum(tile(b, j) * rb - 1, 0), 0)),
                  pl.BlockSpec((SB, 8, 2 * W),
                               lambda b, j, tile=tile: (b, jnp.minimum((tile(b, j) + 1) * rb, nb8 - 1), 0))]
    return pl.pallas_call(
        functools.partial(_lru_kernel, SB=SB, tt=tt),
        grid=(B // SB, nt),
        in_specs=tiles + [pl.BlockSpec((None, 4, W), lmap),
                          pl.BlockSpec((None, 1, W), lmap),
                          pl.BlockSpec((None, 2, W, W), lambda b, j: (layer, 0, 0, 0)),
                          pl.BlockSpec((None, 2, W), lmap),
                          pl.BlockSpec((None, 2, W, W), lambda b, j: (layer, 0, 0, 0)),
                          pl.BlockSpec((None, 2, W), lmap),
                          pl.BlockSpec((None, 2, W), lmap),
                          h0_spec],
        out_specs=[pl.BlockSpec((SB, tt, W), lambda b, j: (b, j, 0)),
                   pl.BlockSpec((SB, tt, W), lambda b, j: (b, nt - 1 - j, 0)),
                   pl.BlockSpec((SB, 2, W), lambda b, j: (b, 0, 0))],
        out_shape=[jax.ShapeDtypeStruct((B, T, W), F32), jax.ShapeDtypeStruct((B, T, W), F32),
                   jax.ShapeDtypeStruct((B, 2, W), F32)],
        scratch_shapes=[pltpu.VMEM((2 * SB, 8, W), F32), pltpu.VMEM((2, SB * tt, W), F32),
                        pltpu.VMEM((2, SB * tt, W), F32)],
        compiler_params=_params("parallel", "arbitrary"),
        name="lru",
    )(*([xy] * 6), conv_w, conv_b, wr, br, wi, bi, lam, h0)


def _merge_kernel(x_ref, mod_ref, nw_ref, of_ref, ob_ref, onw_ref, bd_ref, z_ref, oatt_ref, hf_ref, hb_ref, xy_ref,
                  wg_ref, wpa_ref, wpb_ref, wpc_ref, wo_ref, o_ref):
    D = x_ref.shape[-1]
    x = x_ref[...]
    h = _norm_mod(x, nw_ref[1:2, :], mod_ref[4:5, :], mod_ref[3:4, :]).astype(BF16)
    gates = jax.nn.sigmoid(jnp.dot(h, wg_ref[...], preferred_element_type=F32))
    odn = of_ref[...] + ob_ref[...]
    ms = _mm01(odn * odn, bd_ref[...]) * (1.0 / DN_DK)
    odn = (odn * lax.rsqrt(ms + EPS)) * onw_ref[...]
    a = _mm(odn * _silu(z_ref[...]), wpa_ref[...])
    b = jnp.dot(oatt_ref[...], wpb_ref[...], preferred_element_type=F32)
    c = _mm(_gelu_tanh(xy_ref[:, LRU_WIDTH:2 * LRU_WIDTH]) * (hf_ref[...] + hb_ref[...]), wpc_ref[...])
    merged = (gates[:, 0:D] * a + gates[:, D:2 * D] * b) + gates[:, 2 * D:3 * D] * c
    o_ref[...] = x + mod_ref[5:6, :] * _mm(merged, wo_ref[...])


def _merge_call(x, mod, norm_w, o_f, o_b, onw, bd256, z, o_att, h_f, h_b, xy, w_gate, w_pa, w_pb, w_pc, w_o, *,
                layer, mod_off, tm=512):
    B, T, D = x.shape
    tm = min(tm, T)
    im = lambda b, t: (b, t, 0)
    wspec = lambda w: pl.BlockSpec((None,) + w.shape[1:], lambda b, t: (layer, 0, 0), pipeline_mode=pl.Buffered(1))
    return pl.pallas_call(
        _merge_kernel,
        grid=(B, T // tm),
        in_specs=[pl.BlockSpec((None, tm, D), im),
                  pl.BlockSpec((None, None, N_MOD, D), lambda b, t: (layer, b + mod_off, 0, 0)),
                  pl.BlockSpec((None, 3, D), lambda b, t: (layer, 0, 0)),
                  pl.BlockSpec((None, tm, 256), im), pl.BlockSpec((None, tm, 256), im),
                  pl.BlockSpec((None, 1, 256), lambda b, t: (layer, 0, 0)),
                  pl.BlockSpec((256, 256), lambda b, t: (0, 0)),
                  pl.BlockSpec((None, tm, 256), im),
                  pl.BlockSpec((None, tm, 512), im),
                  pl.BlockSpec((None, tm, 256), im), pl.BlockSpec((None, tm, 256), im), pl.BlockSpec((None, tm, 512), im),
                  wspec(w_gate), wspec(w_pa), wspec(w_pb), wspec(w_pc), wspec(w_o)],
        out_specs=pl.BlockSpec((None, tm, D), im),
        out_shape=jax.ShapeDtypeStruct((B, T, D), F32),
        compiler_params=_params("parallel", "parallel"),
        name="merge",
    )(x, mod, norm_w, o_f, o_b, onw, bd256, z, o_att, h_f, h_b, xy, w_gate, w_pa, w_pb, w_pc, w_o)


def _blockdiag_ones(n, seg):
    i = np.arange(n)
    return jnp.asarray((i[:, None] // seg) == (i[None, :] // seg), BF16)


def _placement():
    G = ATT_Q_HEADS // ATT_KV_HEADS
    e = np.zeros((G, 2 * HEAD_DIM, G * HEAD_DIM), np.float32)
    d = np.arange(HEAD_DIM)
    for hh in range(G):
        e[hh, d, hh * HEAD_DIM + d] = 1.0
    return jnp.asarray(e, BF16)


def _with_ones_column(v):
    one = jnp.ones(v.shape[:-1] + (1,), v.dtype)
    zero = jnp.zeros(v.shape[:-1] + (HEAD_DIM - 1,), v.dtype)
    return jnp.concatenate([v, one, zero], axis=-1).astype(BF16)


def _rope_tables(n_tokens):
    rows = n_tokens // GRID_W
    row = jnp.broadcast_to(jnp.arange(rows, dtype=F32)[:, None], (rows, GRID_W)).reshape(-1)
    col = jnp.broadcast_to(jnp.arange(GRID_W, dtype=F32)[None, :], (rows, GRID_W)).reshape(-1)
    freqs = ROPE_BASE ** (-jnp.arange(ROPE_PAIRS, dtype=F32) / ROPE_PAIRS)
    ang = jnp.stack([row[:, None] * freqs, col[:, None] * freqs], axis=1)
    cos = jnp.cos(ang)[:, :, None, :]
    sin = jnp.sin(ang)[:, :, None, :]
    c = jnp.broadcast_to(cos, (n_tokens, 2, 2, ROPE_PAIRS)).reshape(n_tokens, HEAD_DIM)
    s = jnp.concatenate([-sin, sin], axis=2).reshape(n_tokens, HEAD_DIM)
    return jnp.concatenate([c, c], axis=1), jnp.concatenate([s, s], axis=1)


def _ba_layout(ba, n):
    B = ba.shape[0]
    x = ba.reshape(B, 2, 2, 2, 2, n, DN_CHUNK)
    x = x.transpose(0, 3, 5, 1, 2, 4, 6)
    return x.reshape(B, 2, n, 4, 2 * DN_CHUNK)


def _state_to_blockdiag(s):
    B = s.shape[0]
    x = s.reshape(B, 2, 2, 2, DN_DK, DN_DK)
    z = jnp.zeros_like(x[:, :, :, 0])
    top = jnp.concatenate([x[:, :, :, 0], z], axis=-1)
    bot = jnp.concatenate([z, x[:, :, :, 1]], axis=-1)
    return jnp.concatenate([top, bot], axis=-2).transpose(0, 2, 1, 3, 4)


def _blockdiag_to_state(sb):
    B = sb.shape[0]
    x = sb.transpose(0, 2, 1, 3, 4)
    h0 = x[..., 0:DN_DK, 0:DN_DK]
    h1 = x[..., DN_DK:, DN_DK:]
    return jnp.stack([h0, h1], axis=3).reshape(B, 2, DN_HEADS, DN_DK, DN_DK)


def _lru_blockdiag(w):
    L = w.shape[0]
    bw = LRU_WIDTH // LRU_BLOCKS
    out = jnp.zeros((L, 2, LRU_WIDTH, LRU_WIDTH), w.dtype)
    for n in range(LRU_BLOCKS):
        out = out.at[:, :, n * bw:(n + 1) * bw, n * bw:(n + 1) * bw].set(w[:, :, n])
    return out


def kernel(x_prompt, x_sample, cache_k, cache_v, state_delta, state_lru, c, c_ctx, w_mod, b_mod, norm_w, ffn1_wgu,
           ffn1_wd, ffn2_wgu, ffn2_wd, w_in, dn_conv_w, dn_a_log, dn_dt_bias, dn_onorm_w, att_qnorm_w, att_knorm_w,
           lru_conv_w, lru_conv_b, lru_wr, lru_br, lru_wi, lru_bi, lru_lam, w_pa, w_pb, w_pc, w_o):
    NB, SEQ, D = x_prompt.shape
    DB, DSEQ, _ = x_sample.shape
    L = w_mod.shape[0]
    PAST = cache_k.shape[2]
    TC = NB * SEQ

    cond = jnp.zeros((16, D), F32).at[0].set(c_ctx).at[1:1 + DB].set(c)
    mod = _mod_call(cond, w_mod, b_mod).reshape(L, 16, N_MOD, D)

    bf = lambda w: w.astype(BF16)
    ffn1_wgu, ffn1_wd, ffn2_wgu, ffn2_wd = bf(ffn1_wgu), bf(ffn1_wd), bf(ffn2_wgu), bf(ffn2_wd)
    w_main = bf(jnp.concatenate([w_in[:, :, 1040:1808], w_in[:, :, 0:1024], w_in[:, :, 1808:2320]], axis=-1))
    w_ba = bf(jnp.swapaxes(w_in[:, :, 1024:1040], 1, 2))
    w_gate = bf(w_in[:, :, 2320:])
    w_pa, w_pb, w_pc, w_o = bf(w_pa), bf(w_pb), bf(w_pc), bf(w_o)
    qw = jnp.tile(att_qnorm_w, (1, ATT_Q_HEADS)).reshape(L, 1, ATT_Q_HEADS * HEAD_DIM)
    kw = jnp.tile(att_knorm_w, (1, ATT_KV_HEADS)).reshape(L, 1, ATT_KV_HEADS * HEAD_DIM)
    onw = jnp.tile(dn_onorm_w, (1, DN_HEADS)).reshape(L, 1, DN_HEADS * DN_DK)
    pr = lambda p: jnp.repeat(p.reshape(L, 2, 2, 2), DN_CHUNK, axis=-1).reshape(L, 2, 2, 2 * DN_CHUNK).transpose(0, 2, 1, 3)
    dn_prm = jnp.concatenate([pr(dn_a_log), pr(dn_dt_bias)], axis=2)
    wr_bd, wi_bd = bf(_lru_blockdiag(lru_wr)), bf(_lru_blockdiag(lru_wi))
    lru_cb = lru_conv_b.reshape(L, 1, LRU_WIDTH)
    bd512 = _blockdiag_ones(512, HEAD_DIM)
    bd256 = _blockdiag_ones(256, DN_DK)
    place = _placement()
    rope_tabs = _rope_tables(DSEQ)
    cache_kT = bf(cache_k.transpose(0, 1, 3, 4, 2))
    cache_v1 = _with_ones_column(cache_v.transpose(0, 1, 3, 2, 4))
    s0_lat = _state_to_blockdiag(state_delta.transpose(1, 0, 2, 3, 4, 5).reshape(L * DB, 2, DN_HEADS, DN_DK, DN_DK))
    s0_lat = s0_lat.reshape(L, DB, 2, 2, 2 * DN_DK, 2 * DN_DK)
    s0_ctx = jnp.zeros((NB, 2, 2, 2 * DN_DK, 2 * DN_DK), F32)
    h0_ctx = jnp.zeros((NB, 2, LRU_WIDTH), F32)

    xp = x_prompt.reshape(1, TC, D)
    xs = x_sample
    new_k, new_v, new_sd, new_sl = [], [], [], []
    for l in range(L):
        for ctx in (True, False):
            x = xp if ctx else xs
            mod_off = 0 if ctx else 1
            seq = SEQ if ctx else DSEQ
            nseq = NB if ctx else DB
            shp = x.shape[:2]
            x = _ffn_call(x, mod, norm_w, ffn1_wgu, ffn1_wd, layer=l, sub=0, mod_off=mod_off)
            outs = _inproj_call(x, mod, norm_w, w_main, w_ba, qw, kw, bd512, None if ctx else rope_tabs,
                                layer=l, mod_off=mod_off, emit_kv=ctx)
            qkv, z, xy, q_hm, kT, v_bf, ba = outs[:7]
            n = seq // DN_CHUNK
            ba_l = _ba_layout(ba.reshape(shp[0], 16, -1, seq).transpose(0, 2, 1, 3).reshape(nseq, 16, seq), n)
            ops = _dnchunk_call(qkv.reshape(nseq, seq, 768), dn_conv_w, ba_l, dn_prm, layer=l)
            o_f, o_b, s_fin = _dnscan_call(ops, s0_ctx if ctx else s0_lat[l], T=seq)
            if ctx:
                o_att = _attn_call(
                    q_hm, [(kT, v_bf, SEQ)], place, grid=(NB, 1),
                    q_map=lambda s, t: (0, 0, s, 0),
                    seg_maps=[(lambda s, t: (0, 0, 0, s), lambda s, t: (0, 0, s, 0))],
                    out_map=lambda s, t: (0, s, 0), out_rows=(1, TC), tq=SEQ, kb=512)
            else:
                tq = 128
                o_att = _attn_call(
                    q_hm, [(cache_kT, cache_v1, PAST), (kT, v_bf, DSEQ)], place, grid=(DB, DSEQ // tq),
                    q_map=lambda b, t: (b, 0, t, 0),
                    seg_maps=[(lambda b, t: (b, l, 0, 0, 0), lambda b, t: (b, l, 0, 0, 0)),
                              (lambda b, t: (b, 0, 0, 0), lambda b, t: (b, 0, 0, 0))],
                    out_map=lambda b, t: (b, t, 0), out_rows=(DB, DSEQ), tq=tq, kb=512)
            if ctx:
                h_f, h_b, h_fin = _lru_call(xy.reshape(nseq, seq, 512), lru_conv_w, lru_cb, wr_bd, lru_br, wi_bd,
                                            lru_bi, lru_lam, h0_ctx, layer=l, h0_layer=None)
            else:
                h_f, h_b, h_fin = _lru_call(xy, lru_conv_w, lru_cb, wr_bd, lru_br, wi_bd, lru_bi, lru_lam, state_lru,
                                            layer=l, h0_layer=l)
            x = _merge_call(x, mod, norm_w, o_f.reshape(shp + (256,)), o_b.reshape(shp + (256,)), onw, bd256, z, o_att,
                            h_f.reshape(shp + (256,)), h_b.reshape(shp + (256,)), xy, w_gate, w_pa, w_pb, w_pc, w_o,
                            layer=l, mod_off=mod_off)
            x = _ffn_call(x, mod, norm_w, ffn2_wgu, ffn2_wd, layer=l, sub=2, mod_off=mod_off)
            if ctx:
                xp = x
                kf, vf = outs[7:9]
                new_k.append(kf.reshape(NB, SEQ, ATT_KV_HEADS, HEAD_DIM))
                new_v.append(vf.reshape(NB, SEQ, ATT_KV_HEADS, HEAD_DIM))
                new_sd.append(_blockdiag_to_state(s_fin))
                new_sl.append(h_fin)
            else:
                xs = x
    return (xp.reshape(NB, SEQ, D), xs, jnp.stack(new_k, axis=1), jnp.stack(new_v, axis=1),
            jnp.stack(new_sd, axis=1), jnp.stack(new_sl, axis=1))
```

```python
import functools
import math

import numpy as np
import jax
import jax.numpy as jnp
from jax import lax
from jax.experimental import pallas as pl
from jax.experimental.pallas import tpu as pltpu

F32 = jnp.float32
BF16 = jnp.bfloat16

EPS = 1e-6
GRID_W = 64
HEAD_DIM = 64
ATT_Q_HEADS = 8
ATT_KV_HEADS = 2
ROPE_BASE = 10000.0
ROPE_PAIRS = HEAD_DIM // 4
DN_HEADS = 4
DN_DK = 64
DN_CHUNK = 64
LRU_WIDTH = 256
LRU_BLOCKS = 4
LRU_C = 8.0
N_MOD = 9
LANES = 128
V7X_VMEM_LIMIT_BYTES = 56 * 1024 * 1024


def _params(*sem):
    return pltpu.CompilerParams(dimension_semantics=sem, vmem_limit_bytes=V7X_VMEM_LIMIT_BYTES)


def _mm(a, b):
    return jnp.dot(a.astype(BF16), b.astype(BF16), preferred_element_type=F32)


def _mm_nt(a, b):
    return lax.dot_general(a.astype(BF16), b.astype(BF16), (((1,), (1,)), ((), ())), preferred_element_type=F32)


def _mm_tn(a, b):
    return lax.dot_general(a.astype(BF16), b.astype(BF16), (((0,), (0,)), ((), ())), preferred_element_type=F32)


def _split3(x):
    hi = x.astype(BF16)
    r = x - hi.astype(F32)
    mid = r.astype(BF16)
    lo = (r - mid.astype(F32)).astype(BF16)
    return hi, mid, lo


def _mm01(x, m01):
    hi, mid, lo = _split3(x)
    d = functools.partial(jnp.dot, preferred_element_type=F32)
    return d(hi, m01) + d(mid, m01) + d(lo, m01)


def _mm01x2(x, m01):
    hi = x.astype(BF16)
    lo = (x - hi.astype(F32)).astype(BF16)
    return jnp.dot(hi, m01, preferred_element_type=F32) + jnp.dot(lo, m01, preferred_element_type=F32)


def _mm01_nt(m01, x):
    hi, mid, lo = _split3(x)
    d = functools.partial(lax.dot_general, dimension_numbers=(((1,), (1,)), ((), ())), preferred_element_type=F32)
    return d(m01, hi) + d(m01, mid) + d(m01, lo)


def _mm3(a, b):
    n = a.shape[0]
    ah = a.astype(BF16)
    al = (a - ah.astype(F32)).astype(BF16)
    bh = b.astype(BF16)
    bl = (b - bh.astype(F32)).astype(BF16)
    top = jnp.dot(jnp.concatenate([ah, al], axis=0), bh, preferred_element_type=F32)
    return (top[0:n] + top[n:2 * n]) + jnp.dot(ah, bl, preferred_element_type=F32)


def _mm3_many(As, Bs):
    n = As[0].shape[0]
    ah = [a.astype(BF16) for a in As]
    bh = [b.astype(BF16) for b in Bs]
    al = [(a - h.astype(F32)).astype(BF16) for a, h in zip(As, ah)]
    bl = [(b - h.astype(F32)).astype(BF16) for b, h in zip(Bs, bh)]
    top = [jnp.dot(jnp.concatenate([h, l], axis=0), b, preferred_element_type=F32) for h, l, b in zip(ah, al, bh)]
    low = [jnp.dot(h, b, preferred_element_type=F32) for h, b in zip(ah, bl)]
    return [(t[0:n] + t[n:2 * n]) + w for t, w in zip(top, low)]


def _silu(x):
    return x * jax.nn.sigmoid(x)


def _softplus(x):
    return jnp.maximum(x, 0.0) + jnp.log1p(jnp.exp(-jnp.abs(x)))


def _gelu_tanh(x):
    return x * (0.5 * (1.0 + jnp.tanh(0.7978845608028654 * (x + 0.044715 * (x * x * x)))))


def _norm_mod(x, nw, scale, shift):
    ms = jnp.mean(x * x, axis=-1, keepdims=True)
    y = (x * lax.rsqrt(ms + EPS)) * nw
    return y * (1.0 + scale) + shift


def _seg_masks(n, seg):
    r = lax.broadcasted_iota(jnp.int32, (n, n), 0)
    c = lax.broadcasted_iota(jnp.int32, (n, n), 1)
    return r, c, (r // seg) == (c // seg)


def _swap16(x):
    w = x.shape[1]
    lane = lax.broadcasted_iota(jnp.int32, x.shape, 1)
    return jnp.where((lane & 16) == 0, pltpu.roll(x, w - 16, axis=1), pltpu.roll(x, 16, axis=1))


def _mod_kernel(c_ref, w_ref, b_ref, o_ref):
    o_ref[...] = _mm(_silu(c_ref[...]), w_ref[...]) + b_ref[...]


def _mod_call(cond, w_mod, b_mod):
    L, D, N = w_mod.shape
    R = cond.shape[0]
    tn = D
    return pl.pallas_call(
        _mod_kernel,
        grid=(L, N // tn),
        in_specs=[pl.BlockSpec((R, D), lambda l, j: (0, 0)),
                  pl.BlockSpec((None, D, tn), lambda l, j: (l, 0, j)),
                  pl.BlockSpec((None, 1, tn), lambda l, j: (l, 0, j))],
        out_specs=pl.BlockSpec((None, R, tn), lambda l, j: (l, 0, j)),
        out_shape=jax.ShapeDtypeStruct((L, R, N), F32),
        compiler_params=_params("parallel", "parallel"),
        name="mod",
    )(cond, w_mod, b_mod.reshape(L, 1, N))


def _ffn_kernel(x_ref, mod_ref, nw_ref, wgu_ref, wd_ref, o_ref, *, sub, F, nc):
    x = x_ref[...]
    h = _norm_mod(x, nw_ref[sub:sub + 1, :], mod_ref[3 * sub + 1:3 * sub + 2, :],
                  mod_ref[3 * sub:3 * sub + 1, :]).astype(BF16)
    dot = functools.partial(jnp.dot, preferred_element_type=F32)
    cf = F // nc

    def gate_up(c):
        return (dot(h, wgu_ref[:, c * cf:(c + 1) * cf]), dot(h, wgu_ref[:, F + c * cf:F + (c + 1) * cf]))

    nxt = gate_up(0)
    acc = None
    for c in range(nc):
        g, u = nxt
        if c + 1 < nc:
            nxt = gate_up(c + 1)
        part = dot((_silu(g) * u).astype(BF16), wd_ref[c * cf:(c + 1) * cf, :])
        acc = part if acc is None else acc + part
    o_ref[...] = x + (0.5 * mod_ref[3 * sub + 2:3 * sub + 3, :]) * acc


def _ffn_call(x, mod, norm_w, w_gu, w_d, *, layer, sub, mod_off, tm=1024):
    B, T, D = x.shape
    F = w_d.shape[1]
    tm = min(tm, T)
    nc = F // 256 if F % 256 == 0 else 1
    return pl.pallas_call(
        functools.partial(_ffn_kernel, sub=sub, F=F, nc=nc),
        grid=(B, T // tm),
        in_specs=[pl.BlockSpec((None, tm, D), lambda b, t: (b, t, 0)),
                  pl.BlockSpec((None, None, N_MOD, D), lambda b, t: (layer, b + mod_off, 0, 0)),
                  pl.BlockSpec((None, 3, D), lambda b, t: (layer, 0, 0)),
                  pl.BlockSpec((None, D, 2 * F), lambda b, t: (layer, 0, 0), pipeline_mode=pl.Buffered(1)),
                  pl.BlockSpec((None, F, D), lambda b, t: (layer, 0, 0), pipeline_mode=pl.Buffered(1))],
        out_specs=pl.BlockSpec((None, tm, D), lambda b, t: (b, t, 0)),
        out_shape=jax.ShapeDtypeStruct((B, T, D), F32),
        compiler_params=_params("parallel", "parallel"),
        name="ffn",
    )(x, mod, norm_w, w_gu, w_d)


def _inproj_kernel(*refs, rope, emit_kv):
    (x_ref, mod_ref, nw_ref, w_ref, wba_ref, qw_ref, kw_ref, bd_ref) = refs[:8]
    refs = refs[8:]
    if rope:
        rc_ref, rs_ref = refs[:2]
        refs = refs[2:]
    qkv_ref, z_ref, xy_ref, q_ref, kT_ref, v_ref, ba_ref = refs[:7]
    refs = refs[7:]

    h = _norm_mod(x_ref[...], nw_ref[1:2, :], mod_ref[4:5, :], mod_ref[3:4, :]).astype(BF16)
    ya = jnp.dot(h, w_ref[:, 0:768], preferred_element_type=F32)
    aq = ya[:, 0:512]
    ak = ya[:, 512:640]
    v = ya[:, 640:768]
    bd = bd_ref[...]
    inv_hd = 1.0 / HEAD_DIM
    ssq = _mm01x2(aq * aq, bd)
    ssk = _mm01x2(ak * ak, bd[0:128, 0:128])
    ba_ref[...] = lax.dot_general(wba_ref[...], h, (((1,), (1,)), ((), ())), preferred_element_type=F32)
    y = jnp.dot(h, w_ref[:, 768:2304], preferred_element_type=F32)
    qkv_ref[...] = y[:, 0:768]
    z_ref[...] = y[:, 768:1024]
    xy_ref[...] = y[:, 1024:1536]
    qn = (aq * lax.rsqrt(ssq * inv_hd + EPS)) * qw_ref[...]
    kn = (ak * lax.rsqrt(ssk * inv_hd + EPS)) * kw_ref[...]
    if emit_kv:
        kf_ref, vf_ref = refs
        kf_ref[...] = kn
        vf_ref[...] = v
    if rope:
        rc = rc_ref[...]
        rs = rs_ref[...]
        kn = kn * rc + _swap16(kn) * rs
        qn = qn * jnp.concatenate([rc] * 4, axis=1) + _swap16(qn) * jnp.concatenate([rs] * 4, axis=1)
    qs = qn * (HEAD_DIM ** -0.5 * math.log2(math.e))
    for hh in range(ATT_Q_HEADS):
        q_ref[hh] = qs[:, hh * HEAD_DIM:(hh + 1) * HEAD_DIM].astype(BF16)
    kT = kn.T
    kT_ref[0] = kT[0:HEAD_DIM, :].astype(BF16)
    kT_ref[1] = kT[HEAD_DIM:2 * HEAD_DIM, :].astype(BF16)
    lane = lax.broadcasted_iota(jnp.int32, v.shape, 1)
    tail = jnp.where(lane == HEAD_DIM, 1.0, 0.0)
    v_ref[0] = jnp.where(lane < HEAD_DIM, v, tail).astype(BF16)
    v_ref[1] = jnp.where(lane < HEAD_DIM, pltpu.roll(v, HEAD_DIM, axis=1), tail).astype(BF16)


def _inproj_call(x, mod, norm_w, w_main, w_ba, qw, kw, bd512, rope_tabs, *, layer, mod_off, emit_kv, tm=512):
    B, T, D = x.shape
    tm = min(tm, T)
    NW = w_main.shape[2]
    rope = rope_tabs is not None
    im = lambda b, t: (b, t, 0)
    in_specs = [pl.BlockSpec((None, tm, D), im),
                pl.BlockSpec((None, None, N_MOD, D), lambda b, t: (layer, b + mod_off, 0, 0)),
                pl.BlockSpec((None, 3, D), lambda b, t: (layer, 0, 0)),
                pl.BlockSpec((None, D, NW), lambda b, t: (layer, 0, 0)),
                pl.BlockSpec((None, 16, D), lambda b, t: (layer, 0, 0)),
                pl.BlockSpec((None, 1, 512), lambda b, t: (layer, 0, 0)),
                pl.BlockSpec((None, 1, 128), lambda b, t: (layer, 0, 0)),
                pl.BlockSpec((512, 512), lambda b, t: (0, 0))]
    args = [x, mod, norm_w, w_main, w_ba, qw, kw, bd512]
    if rope:
        in_specs += [pl.BlockSpec((tm, 128), lambda b, t: (t, 0))] * 2
        args += list(rope_tabs)
    out_shape = [jax.ShapeDtypeStruct((B, T, 768), F32), jax.ShapeDtypeStruct((B, T, 256), F32),
                 jax.ShapeDtypeStruct((B, T, 512), F32), jax.ShapeDtypeStruct((B, ATT_Q_HEADS, T, HEAD_DIM), BF16),
                 jax.ShapeDtypeStruct((B, ATT_KV_HEADS, HEAD_DIM, T), BF16),
                 jax.ShapeDtypeStruct((B, ATT_KV_HEADS, T, 128), BF16),
                 jax.ShapeDtypeStruct((B, 16, T), F32)]
    out_specs = [pl.BlockSpec((None, tm, 768), im), pl.BlockSpec((None, tm, 256), im), pl.BlockSpec((None, tm, 512), im),
                 pl.BlockSpec((None, ATT_Q_HEADS, tm, HEAD_DIM), lambda b, t: (b, 0, t, 0)),
                 pl.BlockSpec((None, ATT_KV_HEADS, HEAD_DIM, tm), lambda b, t: (b, 0, 0, t)),
                 pl.BlockSpec((None, ATT_KV_HEADS, tm, 128), lambda b, t: (b, 0, t, 0)),
                 pl.BlockSpec((None, 16, tm), lambda b, t: (b, 0, t))]
    if emit_kv:
        out_shape += [jax.ShapeDtypeStruct((B, T, 128), F32)] * 2
        out_specs += [pl.BlockSpec((None, tm, 128), im)] * 2
    return pl.pallas_call(
        functools.partial(_inproj_kernel, rope=rope, emit_kv=emit_kv),
        grid=(B, T // tm),
        in_specs=in_specs, out_specs=out_specs, out_shape=out_shape,
        compiler_params=_params("parallel", "parallel"),
        name="inproj",
    )(*args)


def _conv_window(main, prev, nxt, at_start, at_end):
    prev = jnp.where(at_start, 0.0, prev)
    nxt = jnp.where(at_end, 0.0, nxt)
    return jnp.concatenate([prev, main, nxt], axis=0)


def _conv4(win, w, n):
    return (w[0:1, :] * win[6:6 + n] + w[1:2, :] * win[7:7 + n]) + (w[2:3, :] * win[8:8 + n] + w[3:4, :] * win[9:9 + n])


def _dnchunk_kernel(q_ref, qp_ref, qn_ref, k_ref, kp_ref, kn_ref, v_ref, vp_ref, vn_ref, cwq_ref, cwk_ref, cwv_ref,
                    ba_ref, prm_ref, p_ref, qm_ref, o1_ref, o2_ref, gs_ref, *, G):
    C = DN_CHUNK
    W = 2 * C
    r, c, same = _seg_masks(W, C)
    i_loc = r % C
    j_loc = c % C
    m01 = lambda mask: jnp.where(mask, 1.0, 0.0).astype(BF16)
    cum_f = m01(same & (i_loc <= j_loc))
    cum_b = m01(same & (i_loc >= j_loc))
    ones_bd = m01(same)
    eye = m01(r == c)
    eye_f = jnp.where(r == c, 1.0, 0.0)
    blk = tuple((r // s) == (c // s) for s in (8, 16, 32, 64))
    incl = (same & (i_loc >= j_loc), same & (i_loc <= j_loc))
    strict = (same & (i_loc > j_loc), same & (i_loc < j_loc))
    head0 = lax.broadcasted_iota(jnp.int32, (C, W), 1) < C
    row2 = lax.broadcasted_iota(jnp.int32, (2, W), 0)
    alog = prm_ref[0:2, :]
    dtb = prm_ref[2:4, :]

    jblk = pl.program_id(2)
    at_start = jblk == 0
    at_end = jblk == pl.num_programs(2) - 1

    def conv_silu(main_ref, prev_ref, next_ref, cw_ref):
        win = _conv_window(main_ref[...], prev_ref[...], next_ref[...], at_start, at_end)
        return _silu(_conv4(win, cw_ref[...], G * C))

    def l2norm(x):
        return x * lax.rsqrt(_mm01(x * x, ones_bd) + EPS)

    q_all = l2norm(conv_silu(q_ref, qp_ref, qn_ref, cwq_ref)) * (DN_DK ** -0.5)
    k_all = l2norm(conv_silu(k_ref, kp_ref, kn_ref, cwk_ref))
    v_all = conv_silu(v_ref, vp_ref, vn_ref, cwv_ref)

    def stack(x):
        return jnp.concatenate([jnp.where(head0, x, 0.0), jnp.where(head0, 0.0, x)], axis=0)

    chunks = range(G)
    bg = [ba_ref[gi] for gi in chunks]
    beta = [jax.nn.sigmoid(b[0:2, :]) for b in bg]
    g = [-jnp.exp(alog) * _softplus(b[2:4, :] + dtb) for b in bg]
    gcf = [_mm01(x, cum_f) for x in g]
    gcb = [_mm01(x, cum_b) for x in g]
    tot = [_mm01(x, ones_bd) for x in g]
    rows = [jnp.concatenate([beta[i], jnp.where(row2 == 0, gcf[i], gcb[i]), tot[i], jnp.zeros((2, W), F32)], axis=0)
            for i in chunks]
    cols = [_mm01_nt(eye, x) for x in rows]
    Kst = [stack(k_all[gi * C:(gi + 1) * C, :]) for gi in chunks]
    Qst = [stack(q_all[gi * C:(gi + 1) * C, :]) for gi in chunks]
    Vst = [stack(v_all[gi * C:(gi + 1) * C, :]) for gi in chunks]
    kq = [_mm_nt(jnp.concatenate([Kst[i], Qst[i]], axis=0), Kst[i]) for i in chunks]

    chains = [(gi, d) for gi in chunks for d in range(2)]
    col = lambda gi, k: cols[gi][:, k:k + 1]
    dec = [jnp.exp(jnp.where(incl[d], col(gi, 2 + d) - rows[gi][2 + d:3 + d, :], -jnp.inf)) for gi, d in chains]
    L = [(col(gi, d) * kq[gi][0:W]) * jnp.where(strict[d], dec[i], 0.0) for i, (gi, d) in enumerate(chains)]
    QK = [kq[gi][W:2 * W] * dec[i] for i, (gi, d) in enumerate(chains)]
    dot = functools.partial(jnp.dot, preferred_element_type=F32)
    D8f = [jnp.where(blk[0], x, 0.0) for x in L]
    D8 = [x.astype(BF16) for x in D8f]
    M = [dot(d8, d8) for d8 in D8]
    Mb = [m.astype(BF16) for m in M]
    DM = [dot(d8, mb) for d8, mb in zip(D8, Mb)]
    R = [(m - d8) - dm for m, d8, dm in zip(M, D8f, DM)]
    M = [dot(mb, mb) for mb in Mb]
    RM = [dot(r_.astype(BF16), m.astype(BF16)) for r_, m in zip(R, M)]
    Tm = [eye_f + ((r_ + m) + rm) for r_, m, rm in zip(R, M, RM)]
    for lvl in range(3):
        off = blk[lvl + 1] & jnp.logical_not(blk[lvl])
        Th = [t.astype(BF16) for t in Tm]
        Bm = [jnp.where(off, x, 0.0) for x in L]
        Bh = [b.astype(BF16) for b in Bm]
        Bl = [(b - h.astype(F32)).astype(BF16) for b, h in zip(Bm, Bh)]
        TB2 = [dot(th, jnp.concatenate([bh, bl], axis=1)) for th, bh, bl in zip(Th, Bh, Bl)]
        TB = [x[:, 0:W] + x[:, W:2 * W] for x in TB2]
        TBh = [x.astype(BF16) for x in TB]
        TBl = [(x - h.astype(F32)).astype(BF16) for x, h in zip(TB, TBh)]
        TBT2 = [dot(jnp.concatenate([h, lo], axis=0), th) for h, lo, th in zip(TBh, TBl, Th)]
        Tm = [t - (x[0:W] + x[W:2 * W]) for t, x in zip(Tm, TBT2)]
    rhs = [jnp.concatenate([(col(gi, d) * jnp.exp(col(gi, 2 + d))) * Kst[gi], col(gi, d) * Vst[gi]], axis=1)
           for gi, d in chains]
    X = _mm3_many(Tm, rhs)
    kd = [Kst[gi] * jnp.exp(col(gi, 4 + d) - col(gi, 2 + d)) for gi, d in chains]
    PQ = [_mm_tn(a, x) for a, x in zip(kd, X)]
    OO = [_mm(a, x) for a, x in zip(QK, X)]
    for i, (gi, d) in enumerate(chains):
        o1 = Qst[gi] * jnp.exp(col(gi, 2 + d)) - OO[i][:, 0:W]
        o2 = OO[i][:, W:2 * W]
        p_ref[d, gi] = PQ[i][:, 0:W].astype(BF16)
        qm_ref[d, gi] = PQ[i][:, W:2 * W]
        o1_ref[d, gi] = (o1[0:C] + o1[C:W]).astype(BF16)
        o2_ref[d, gi] = o2[0:C] + o2[C:W]
        gs_ref[d, gi] = jnp.exp(rows[gi][4 + d:5 + d, :])


def _dnchunk_call(qkv, conv_w, ba, prm, *, layer, G=8):
    B, T, _ = qkv.shape
    C = DN_CHUNK
    W = 2 * C
    n = T // C
    G = min(G, n)
    rb = G * C // 8
    nb8 = T // 8
    sds = jax.ShapeDtypeStruct
    mat = lambda rows, dt: (sds((B, 2, 2, n, rows, W), dt),
                            pl.BlockSpec((None, None, 2, G, rows, W), lambda b, p, j: (b, p, 0, j, 0, 0)))
    outs = [mat(W, BF16), mat(W, F32), mat(C, BF16), mat(C, F32), mat(1, F32)]
    in_specs = []
    for part in range(3):
        in_specs += [pl.BlockSpec((None, G * C, W), lambda b, p, j, part=part: (b, j, 2 * part + p)),
                     pl.BlockSpec((None, 8, W), lambda b, p, j, part=part: (b, jnp.maximum(j * rb - 1, 0), 2 * part + p)),
                     pl.BlockSpec((None, 8, W),
                                  lambda b, p, j, part=part: (b, jnp.minimum((j + 1) * rb, nb8 - 1), 2 * part + p))]
    in_specs += [pl.BlockSpec((None, 4, W), lambda b, p, j, part=part: (layer, 0, 2 * part + p)) for part in range(3)]
    in_specs += [pl.BlockSpec((None, None, G, 4, W), lambda b, p, j: (b, p, j, 0, 0)),
                 pl.BlockSpec((None, None, 4, W), lambda b, p, j: (layer, p, 0, 0))]
    return pl.pallas_call(
        functools.partial(_dnchunk_kernel, G=G),
        grid=(B, 2, n // G),
        in_specs=in_specs,
        out_specs=[o[1] for o in outs],
        out_shape=[o[0] for o in outs],
        compiler_params=_params("parallel", "parallel", "parallel"),
        name="dnchunk",
    )(*([qkv] * 9), conv_w, conv_w, conv_w, ba, prm)


def _dnscan_kernel(pf_ref, qf_ref, o1f_ref, o2f_ref, gf_ref, pb_ref, qb_ref, o1b_ref, o2b_ref, gb_ref, s0_ref,
                   of_ref, ob_ref, sfin_ref, s_sc, *, Gs):
    C = DN_CHUNK
    W = 2 * C
    j = pl.program_id(1)

    @pl.when(j == 0)
    def _():
        s_sc[...] = s0_ref[...]

    S = [[s_sc[p, d] for d in range(2)] for p in range(2)]
    fwd = (pf_ref, qf_ref, o1f_ref, o2f_ref, gf_ref, of_ref)
    bwd = (pb_ref, qb_ref, o1b_ref, o2b_ref, gb_ref, ob_ref)
    for i in range(Gs):
        for d, (P, Qm, O1, O2, GS, out) in enumerate((fwd, bwd)):
            ci = i if d == 0 else Gs - 1 - i
            for p in range(2):
                Sb = S[p][d].astype(BF16)
                out[ci * C:(ci + 1) * C, p * W:(p + 1) * W] = (
                    jnp.dot(O1[p, ci], Sb, preferred_element_type=F32) + O2[p, ci])
                S[p][d] = (GS[p, ci] * S[p][d] - jnp.dot(P[p, ci], Sb, preferred_element_type=F32)) + Qm[p, ci]
    for p in range(2):
        for d in range(2):
            s_sc[p, d] = S[p][d]

    @pl.when(j == pl.num_programs(1) - 1)
    def _():
        sfin_ref[...] = s_sc[...]


def _dnscan_call(ops, s0, *, T, Gs=8):
    B = s0.shape[0]
    C = DN_CHUNK
    W = 2 * C
    n = T // C
    Gs = min(Gs, n)
    nb = n // Gs
    specs = []
    for d in range(2):
        for a in ops:
            rows = a.shape[4]
            if d == 0:
                specs.append(pl.BlockSpec((None, 2, None, Gs, rows, W), lambda b, j: (b, 0, 0, j, 0, 0)))
            else:
                specs.append(pl.BlockSpec((None, 2, None, Gs, rows, W), lambda b, j: (b, 0, 1, nb - 1 - j, 0, 0)))
    st_spec = pl.BlockSpec((None, 2, 2, W, W), lambda b, j: (b, 0, 0, 0, 0))
    return pl.pallas_call(
        functools.partial(_dnscan_kernel, Gs=Gs),
        grid=(B, nb),
        in_specs=specs + [st_spec],
        out_specs=[pl.BlockSpec((None, Gs * C, 2 * W), lambda b, j: (b, j, 0)),
                   pl.BlockSpec((None, Gs * C, 2 * W), lambda b, j: (b, nb - 1 - j, 0)),
                   st_spec],
        out_shape=[jax.ShapeDtypeStruct((B, T, 2 * W), F32), jax.ShapeDtypeStruct((B, T, 2 * W), F32),
                   jax.ShapeDtypeStruct((B, 2, 2, W, W), F32)],
        scratch_shapes=[pltpu.VMEM((2, 2, W, W), F32)],
        compiler_params=_params("parallel", "arbitrary"),
        name="dnscan",
    )(*ops, *ops, s0)


def _attn_kernel(*refs, seg_blocks, tq, rt):
    q_ref = refs[0]
    nseg = len(seg_blocks)
    e_ref, o_ref = refs[1 + 2 * nseg:3 + 2 * nseg]
    G = ATT_Q_HEADS // ATT_KV_HEADS
    rows = G * tq
    rt = min(rt, rows)
    for g in range(ATT_KV_HEADS):
        q = q_ref[g * G:(g + 1) * G].reshape(rows, HEAD_DIM)
        blocks = [(si, j, kb) for si, (nblk, kb) in enumerate(seg_blocks) for j in range(nblk)]

        def scores(b):
            si, j, kb = blocks[b]
            return jnp.dot(q, refs[1 + 2 * si][g, :, j * kb:(j + 1) * kb], preferred_element_type=F32)

        m = None
        acc = None
        s_next = scores(0)
        for b, (si, j, kb) in enumerate(blocks):
            s_all = s_next
            if b + 1 < len(blocks):
                s_next = scores(b + 1)
            v = refs[2 + 2 * si][g, j * kb:(j + 1) * kb, :]
            S = [s_all[r * rt:(r + 1) * rt] for r in range(rows // rt)]
            smax = [jnp.max(s, axis=-1, keepdims=True) for s in S]
            if m is None:
                m_new = smax
            else:
                m_new = [jnp.maximum(a, b_) for a, b_ in zip(m, smax)]
                alpha = [jnp.exp2(a - b_) for a, b_ in zip(m, m_new)]
            P = [jnp.exp2((s - mn).astype(BF16)) for s, mn in zip(S, m_new)]
            pv = jnp.dot(jnp.concatenate(P, axis=0), v, preferred_element_type=F32)
            acc = pv if acc is None else jnp.concatenate(alpha, axis=0) * acc + pv
            m = m_new
        o = (acc / acc[:, HEAD_DIM:HEAD_DIM + 1]).astype(BF16)
        out = jnp.dot(o[0:tq], e_ref[0], preferred_element_type=F32)
        for hh in range(1, G):
            out = out + jnp.dot(o[hh * tq:(hh + 1) * tq], e_ref[hh], preferred_element_type=F32)
        o_ref[:, g * G * HEAD_DIM:(g + 1) * G * HEAD_DIM] = out.astype(BF16)


def _attn_call(q, segs, place, *, grid, q_map, seg_maps, out_map, out_rows, tq, kb, rt=512):
    G = ATT_Q_HEADS // ATT_KV_HEADS
    in_specs = [pl.BlockSpec((None, ATT_Q_HEADS, tq, HEAD_DIM), q_map)]
    args = [q]
    seg_blocks = []
    for (kT, v, S), (k_map, v_map) in zip(segs, seg_maps):
        blk = min(kb, S)
        seg_blocks.append((S // blk, blk))
        in_specs.append(pl.BlockSpec((None,) * (kT.ndim - 3) + (ATT_KV_HEADS, HEAD_DIM, S), k_map))
        in_specs.append(pl.BlockSpec((None,) * (v.ndim - 3) + (ATT_KV_HEADS, S, 2 * HEAD_DIM), v_map))
        args += [kT, v]
    in_specs.append(pl.BlockSpec((G, 2 * HEAD_DIM, G * HEAD_DIM), lambda b, t: (0, 0, 0)))
    args.append(place)
    return pl.pallas_call(
        functools.partial(_attn_kernel, seg_blocks=tuple(seg_blocks), tq=tq, rt=rt),
        grid=grid,
        in_specs=in_specs,
        out_specs=pl.BlockSpec((None, tq, ATT_Q_HEADS * HEAD_DIM), out_map),
        out_shape=jax.ShapeDtypeStruct(out_rows + (ATT_Q_HEADS * HEAD_DIM,), BF16),
        compiler_params=_params("parallel", "parallel"),
        name="attn",
    )(*args)


def _lru_kernel(xf_ref, xfp_ref, xfn_ref, xb_ref, xbp_ref, xbn_ref, cw_ref, cb_ref, wr_ref, br_ref, wi_ref, bi_ref,
                lam_ref, h0_ref, hf_ref, hb_ref, hfin_ref, h_sc, a_sc, u_sc, *, SB, tt):
    W = LRU_WIDTH
    j = pl.program_id(1)
    last = pl.num_programs(1) - 1
    cw = cw_ref[...]
    cb = cb_ref[...]
    rid = lax.broadcasted_iota(jnp.int32, (8, W), 0)

    @pl.when(j == 0)
    def _():
        for s in range(SB):
            for d in range(2):
                h_sc[2 * s + d] = jnp.broadcast_to(h0_ref[s, d:d + 1, :], (8, W))

    def gates(x_ref, p_ref, n_ref, at_start, at_end, d):
        x = jnp.concatenate(
            [_conv4(_conv_window(x_ref[s, :, 0:W], p_ref[s, :, 0:W], n_ref[s, :, 0:W], at_start, at_end), cw, tt)
             for s in range(SB)], axis=0) + cb
        rg = jax.nn.sigmoid(_mm(x, wr_ref[d]) + br_ref[d:d + 1, :])
        ig = jax.nn.sigmoid(_mm(x, wi_ref[d]) + bi_ref[d:d + 1, :])
        log_a = (-LRU_C * rg) * _softplus(-lam_ref[d:d + 1, :])
        a = jnp.exp(log_a)
        a_sc[d] = a
        u_sc[d] = jnp.sqrt(-jnp.tanh(log_a) * (a * a + 1.0)) * (ig * x)

    gates(xf_ref, xfp_ref, xfn_ref, j == 0, j == last, 0)
    gates(xb_ref, xbp_ref, xbn_ref, j == last, j == 0, 1)

    chains = [(s, d) for s in range(SB) for d in range(2)]
    nblk = tt // 8

    def scan8(j8, hs):
        blk = [j8 if d == 0 else nblk - 1 - j8 for _, d in chains]
        row0 = [pl.multiple_of(s * tt + b * 8, 8) for (s, _), b in zip(chains, blk)]
        a8 = [a_sc[d, pl.ds(r0, 8), :] for (_, d), r0 in zip(chains, row0)]
        u8 = [u_sc[d, pl.ds(r0, 8), :] for (_, d), r0 in zip(chains, row0)]
        hs = list(hs)
        out = [jnp.zeros((8, W), F32)] * len(chains)
        for step in range(8):
            for i, (_, d) in enumerate(chains):
                r = step if d == 0 else 7 - step
                hs[i] = (jnp.broadcast_to(a8[i][r:r + 1, :], (8, W)) * hs[i]
                         + jnp.broadcast_to(u8[i][r:r + 1, :], (8, W)))
                out[i] = jnp.where(rid == r, hs[i], out[i])
        for i, (s, d) in enumerate(chains):
            dst = hf_ref if d == 0 else hb_ref
            dst[s, pl.ds(pl.multiple_of(blk[i] * 8, 8), 8), :] = out[i]
        return tuple(hs)

    hs = lax.fori_loop(0, nblk, scan8, tuple(h_sc[i] for i in range(len(chains))))
    for i in range(len(chains)):
        h_sc[i] = hs[i]

    @pl.when(j == last)
    def _():
        for i, (s, d) in enumerate(chains):
            hfin_ref[s, d:d + 1, :] = hs[i][0:1, :]


def _lru_call(xy, conv_w, conv_b, wr, br, wi, bi, lam, h0, *, layer, h0_layer, SB=4):
    B, T, _ = xy.shape
    W = LRU_WIDTH
    tt = min(256, T)
    nt = T // tt
    SB = min(SB, B)
    rb = tt // 8
    nb8 = T // 8
    lmap = lambda b, j: (layer, 0, 0)
    h0_spec = (pl.BlockSpec((SB, 2, W), lambda b, j: (b, 0, 0)) if h0_layer is None
               else pl.BlockSpec((SB, None, 2, W), lambda b, j: (b, h0_layer, 0, 0)))
    fwd = lambda b, j: j
    bwd = lambda b, j: nt - 1 - j
    tiles = []
    for tile in (fwd, bwd):
        tiles += [pl.BlockSpec((SB, tt, 2 * W), lambda b, j, tile=tile: (b, tile(b, j), 0)),
                  pl.BlockSpec((SB, 8, 2 * W), lambda b, j, tile=tile: (b, jnp.maximum(tile(b, j) * rb - 1, 0), 0)),
                  pl.BlockSpec((SB, 8, 2 * W),
                               lambda b, j, tile=tile: (b, jnp.minimum((tile(b, j) + 1) * rb, nb8 - 1), 0))]
    return pl.pallas_call(
        functools.partial(_lru_kernel, SB=SB, tt=tt),
        grid=(B // SB, nt),
        in_specs=tiles + [pl.BlockSpec((None, 4, W), lmap),
                          pl.BlockSpec((None, 1, W), lmap),
                          pl.BlockSpec((None, 2, W, W), lambda b, j: (layer, 0, 0, 0)),
                          pl.BlockSpec((None, 2, W), lmap),
                          pl.BlockSpec((None, 2, W, W), lambda b, j: (layer, 0, 0, 0)),
                          pl.BlockSpec((None, 2, W), lmap),
                          pl.BlockSpec((None, 2, W), lmap),
                          h0_spec],
        out_specs=[pl.BlockSpec((SB, tt, W), lambda b, j: (b, j, 0)),
                   pl.BlockSpec((SB, tt, W), lambda b, j: (b, nt - 1 - j, 0)),
                   pl.BlockSpec((SB, 2, W), lambda b, j: (b, 0, 0))],
        out_shape=[jax.ShapeDtypeStruct((B, T, W), F32), jax.ShapeDtypeStruct((B, T, W), F32),
                   jax.ShapeDtypeStruct((B, 2, W), F32)],
        scratch_shapes=[pltpu.VMEM((2 * SB, 8, W), F32), pltpu.VMEM((2, SB * tt, W), F32),
                        pltpu.VMEM((2, SB * tt, W), F32)],
        compiler_params=_params("parallel", "arbitrary"),
        name="lru",
    )(*([xy] * 6), conv_w, conv_b, wr, br, wi, bi, lam, h0)


def _merge_kernel(x_ref, mod_ref, nw_ref, of_ref, ob_ref, onw_ref, bd_ref, z_ref, oatt_ref, hf_ref, hb_ref, xy_ref,
                  wg_ref, wpa_ref, wpb_ref, wpc_ref, wo_ref, o_ref):
    D = x_ref.shape[-1]
    x = x_ref[...]
    h = _norm_mod(x, nw_ref[1:2, :], mod_ref[4:5, :], mod_ref[3:4, :]).astype(BF16)
    gates = jax.nn.sigmoid(jnp.dot(h, wg_ref[...], preferred_element_type=F32))
    odn = of_ref[...] + ob_ref[...]
    ms = _mm01(odn * odn, bd_ref[...]) * (1.0 / DN_DK)
    odn = (odn * lax.rsqrt(ms + EPS)) * onw_ref[...]
    a = _mm(odn * _silu(z_ref[...]), wpa_ref[...])
    b = jnp.dot(oatt_ref[...], wpb_ref[...], preferred_element_type=F32)
    c = _mm(_gelu_tanh(xy_ref[:, LRU_WIDTH:2 * LRU_WIDTH]) * (hf_ref[...] + hb_ref[...]), wpc_ref[...])
    merged = (gates[:, 0:D] * a + gates[:, D:2 * D] * b) + gates[:, 2 * D:3 * D] * c
    o_ref[...] = x + mod_ref[5:6, :] * _mm(merged, wo_ref[...])


def _merge_call(x, mod, norm_w, o_f, o_b, onw, bd256, z, o_att, h_f, h_b, xy, w_gate, w_pa, w_pb, w_pc, w_o, *,
                layer, mod_off, tm=512):
    B, T, D = x.shape
    tm = min(tm, T)
    im = lambda b, t: (b, t, 0)
    wspec = lambda w: pl.BlockSpec((None,) + w.shape[1:], lambda b, t: (layer, 0, 0), pipeline_mode=pl.Buffered(1))
    return pl.pallas_call(
        _merge_kernel,
        grid=(B, T // tm),
        in_specs=[pl.BlockSpec((None, tm, D), im),
                  pl.BlockSpec((None, None, N_MOD, D), lambda b, t: (layer, b + mod_off, 0, 0)),
                  pl.BlockSpec((None, 3, D), lambda b, t: (layer, 0, 0)),
                  pl.BlockSpec((None, tm, 256), im), pl.BlockSpec((None, tm, 256), im),
                  pl.BlockSpec((None, 1, 256), lambda b, t: (layer, 0, 0)),
                  pl.BlockSpec((256, 256), lambda b, t: (0, 0)),
                  pl.BlockSpec((None, tm, 256), im),
                  pl.BlockSpec((None, tm, 512), im),
                  pl.BlockSpec((None, tm, 256), im), pl.BlockSpec((None, tm, 256), im), pl.BlockSpec((None, tm, 512), im),
                  wspec(w_gate), wspec(w_pa), wspec(w_pb), wspec(w_pc), wspec(w_o)],
        out_specs=pl.BlockSpec((None, tm, D), im),
        out_shape=jax.ShapeDtypeStruct((B, T, D), F32),
        compiler_params=_params("parallel", "parallel"),
        name="merge",
    )(x, mod, norm_w, o_f, o_b, onw, bd256, z, o_att, h_f, h_b, xy, w_gate, w_pa, w_pb, w_pc, w_o)


def _blockdiag_ones(n, seg):
    i = np.arange(n)
    return jnp.asarray((i[:, None] // seg) == (i[None, :] // seg), BF16)


def _placement():
    G = ATT_Q_HEADS // ATT_KV_HEADS
    e = np.zeros((G, 2 * HEAD_DIM, G * HEAD_DIM), np.float32)
    d = np.arange(HEAD_DIM)
    for hh in range(G):
        e[hh, d, hh * HEAD_DIM + d] = 1.0
    return jnp.asarray(e, BF16)


def _with_ones_column(v):
    one = jnp.ones(v.shape[:-1] + (1,), v.dtype)
    zero = jnp.zeros(v.shape[:-1] + (HEAD_DIM - 1,), v.dtype)
    return jnp.concatenate([v, one, zero], axis=-1).astype(BF16)


def _rope_tables(n_tokens):
    rows = n_tokens // GRID_W
    row = jnp.broadcast_to(jnp.arange(rows, dtype=F32)[:, None], (rows, GRID_W)).reshape(-1)
    col = jnp.broadcast_to(jnp.arange(GRID_W, dtype=F32)[None, :], (rows, GRID_W)).reshape(-1)
    freqs = ROPE_BASE ** (-jnp.arange(ROPE_PAIRS, dtype=F32) / ROPE_PAIRS)
    ang = jnp.stack([row[:, None] * freqs, col[:, None] * freqs], axis=1)
    cos = jnp.cos(ang)[:, :, None, :]
    sin = jnp.sin(ang)[:, :, None, :]
    c = jnp.broadcast_to(cos, (n_tokens, 2, 2, ROPE_PAIRS)).reshape(n_tokens, HEAD_DIM)
    s = jnp.concatenate([-sin, sin], axis=2).reshape(n_tokens, HEAD_DIM)
    return jnp.concatenate([c, c], axis=1), jnp.concatenate([s, s], axis=1)


def _ba_layout(ba, n):
    B = ba.shape[0]
    x = ba.reshape(B, 2, 2, 2, 2, n, DN_CHUNK)
    x = x.transpose(0, 3, 5, 1, 2, 4, 6)
    return x.reshape(B, 2, n, 4, 2 * DN_CHUNK)


def _state_to_blockdiag(s):
    B = s.shape[0]
    x = s.reshape(B, 2, 2, 2, DN_DK, DN_DK)
    z = jnp.zeros_like(x[:, :, :, 0])
    top = jnp.concatenate([x[:, :, :, 0], z], axis=-1)
    bot = jnp.concatenate([z, x[:, :, :, 1]], axis=-1)
    return jnp.concatenate([top, bot], axis=-2).transpose(0, 2, 1, 3, 4)


def _blockdiag_to_state(sb):
    B = sb.shape[0]
    x = sb.transpose(0, 2, 1, 3, 4)
    h0 = x[..., 0:DN_DK, 0:DN_DK]
    h1 = x[..., DN_DK:, DN_DK:]
    return jnp.stack([h0, h1], axis=3).reshape(B, 2, DN_HEADS, DN_DK, DN_DK)


def _lru_blockdiag(w):
    L = w.shape[0]
    bw = LRU_WIDTH // LRU_BLOCKS
    out = jnp.zeros((L, 2, LRU_WIDTH, LRU_WIDTH), w.dtype)
    for n in range(LRU_BLOCKS):
        out = out.at[:, :, n * bw:(n + 1) * bw, n * bw:(n + 1) * bw].set(w[:, :, n])
    return out


def kernel(x_prompt, x_sample, cache_k, cache_v, state_delta, state_lru, c, c_ctx, w_mod, b_mod, norm_w, ffn1_wgu,
           ffn1_wd, ffn2_wgu, ffn2_wd, w_in, dn_conv_w, dn_a_log, dn_dt_bias, dn_onorm_w, att_qnorm_w, att_knorm_w,
           lru_conv_w, lru_conv_b, lru_wr, lru_br, lru_wi, lru_bi, lru_lam, w_pa, w_pb, w_pc, w_o):
    NB, SEQ, D = x_prompt.shape
    DB, DSEQ, _ = x_sample.shape
    L = w_mod.shape[0]
    PAST = cache_k.shape[2]
    TC = NB * SEQ

    cond = jnp.zeros((16, D), F32).at[0].set(c_ctx).at[1:1 + DB].set(c)
    mod = _mod_call(cond, w_mod, b_mod).reshape(L, 16, N_MOD, D)

    bf = lambda w: w.astype(BF16)
    ffn1_wgu, ffn1_wd, ffn2_wgu, ffn2_wd = bf(ffn1_wgu), bf(ffn1_wd), bf(ffn2_wgu), bf(ffn2_wd)
    w_main = bf(jnp.concatenate([w_in[:, :, 1040:1808], w_in[:, :, 0:1024], w_in[:, :, 1808:2320]], axis=-1))
    w_ba = bf(jnp.swapaxes(w_in[:, :, 1024:1040], 1, 2))
    w_gate = bf(w_in[:, :, 2320:])
    w_pa, w_pb, w_pc, w_o = bf(w_pa), bf(w_pb), bf(w_pc), bf(w_o)
    qw = jnp.tile(att_qnorm_w, (1, ATT_Q_HEADS)).reshape(L, 1, ATT_Q_HEADS * HEAD_DIM)
    kw = jnp.tile(att_knorm_w, (1, ATT_KV_HEADS)).reshape(L, 1, ATT_KV_HEADS * HEAD_DIM)
    onw = jnp.tile(dn_onorm_w, (1, DN_HEADS)).reshape(L, 1, DN_HEADS * DN_DK)
    pr = lambda p: jnp.repeat(p.reshape(L, 2, 2, 2), DN_CHUNK, axis=-1).reshape(L, 2, 2, 2 * DN_CHUNK).transpose(0, 2, 1, 3)
    dn_prm = jnp.concatenate([pr(dn_a_log), pr(dn_dt_bias)], axis=2)
    wr_bd, wi_bd = bf(_lru_blockdiag(lru_wr)), bf(_lru_blockdiag(lru_wi))
    lru_cb = lru_conv_b.reshape(L, 1, LRU_WIDTH)
    bd512 = _blockdiag_ones(512, HEAD_DIM)
    bd256 = _blockdiag_ones(256, DN_DK)
    place = _placement()
    rope_tabs = _rope_tables(DSEQ)
    cache_kT = bf(cache_k.transpose(0, 1, 3, 4, 2))
    cache_v1 = _with_ones_column(cache_v.transpose(0, 1, 3, 2, 4))
    s0_lat = _state_to_blockdiag(state_delta.transpose(1, 0, 2, 3, 4, 5).reshape(L * DB, 2, DN_HEADS, DN_DK, DN_DK))
    s0_lat = s0_lat.reshape(L, DB, 2, 2, 2 * DN_DK, 2 * DN_DK)
    s0_ctx = jnp.zeros((NB, 2, 2, 2 * DN_DK, 2 * DN_DK), F32)
    h0_ctx = jnp.zeros((NB, 2, LRU_WIDTH), F32)

    xp = x_prompt.reshape(1, TC, D)
    xs = x_sample
    new_k, new_v, new_sd, new_sl = [], [], [], []
    for l in range(L):
        for ctx in (True, False):
            x = xp if ctx else xs
            mod_off = 0 if ctx else 1
            seq = SEQ if ctx else DSEQ
            nseq = NB if ctx else DB
            shp = x.shape[:2]
            x = _ffn_call(x, mod, norm_w, ffn1_wgu, ffn1_wd, layer=l, sub=0, mod_off=mod_off)
            outs = _inproj_call(x, mod, norm_w, w_main, w_ba, qw, kw, bd512, None if ctx else rope_tabs,
                                layer=l, mod_off=mod_off, emit_kv=ctx)
            qkv, z, xy, q_hm, kT, v_bf, ba = outs[:7]
            n = seq // DN_CHUNK
            ba_l = _ba_layout(ba.reshape(shp[0], 16, -1, seq).transpose(0, 2, 1, 3).reshape(nseq, 16, seq), n)
            ops = _dnchunk_call(qkv.reshape(nseq, seq, 768), dn_conv_w, ba_l, dn_prm, layer=l)
            o_f, o_b, s_fin = _dnscan_call(ops, s0_ctx if ctx else s0_lat[l], T=seq)
            if ctx:
                o_att = _attn_call(
                    q_hm, [(kT, v_bf, SEQ)], place, grid=(NB, 1),
                    q_map=lambda s, t: (0, 0, s, 0),
                    seg_maps=[(lambda s, t: (0, 0, 0, s), lambda s, t: (0, 0, s, 0))],
                    out_map=lambda s, t: (0, s, 0), out_rows=(1, TC), tq=SEQ, kb=512)
            else:
                tq = 128
                o_att = _attn_call(
                    q_hm, [(cache_kT, cache_v1, PAST), (kT, v_bf, DSEQ)], place, grid=(DB, DSEQ // tq),
                    q_map=lambda b, t: (b, 0, t, 0),
                    seg_maps=[(lambda b, t: (b, l, 0, 0, 0), lambda b, t: (b, l, 0, 0, 0)),
                              (lambda b, t: (b, 0, 0, 0), lambda b, t: (b, 0, 0, 0))],
                    out_map=lambda b, t: (b, t, 0), out_rows=(DB, DSEQ), tq=tq, kb=512)
            if ctx:
                h_f, h_b, h_fin = _lru_call(xy.reshape(nseq, seq, 512), lru_conv_w, lru_cb, wr_bd, lru_br, wi_bd,
                                            lru_bi, lru_lam, h0_ctx, layer=l, h0_layer=None)
            else:
                h_f, h_b, h_fin = _lru_call(xy, lru_conv_w, lru_cb, wr_bd, lru_br, wi_bd, lru_bi, lru_lam, state_lru,
                                            layer=l, h0_layer=l)
            x = _merge_call(x, mod, norm_w, o_f.reshape(shp + (256,)), o_b.reshape(shp + (256,)), onw, bd256, z, o_att,
                            h_f.reshape(shp + (256,)), h_b.reshape(shp + (256,)), xy, w_gate, w_pa, w_pb, w_pc, w_o,
                            layer=l, mod_off=mod_off)
            x = _ffn_call(x, mod, norm_w, ffn2_wgu, ffn2_wd, layer=l, sub=2, mod_off=mod_off)
            if ctx:
                xp = x
                kf, vf = outs[7:9]
                new_k.append(kf.reshape(NB, SEQ, ATT_KV_HEADS, HEAD_DIM))
                new_v.append(vf.reshape(NB, SEQ, ATT_KV_HEADS, HEAD_DIM))
                new_sd.append(_blockdiag_to_state(s_fin))
                new_sl.append(h_fin)
            else:
                xs = x
    return (xp.reshape(NB, SEQ, D), xs, jnp.stack(new_k, axis=1), jnp.stack(new_v, axis=1),
            jnp.stack(new_sd, axis=1), jnp.stack(new_sl, axis=1))
```

```python
import functools
import math

import numpy as np
import jax
import jax.numpy as jnp
from jax import lax
from jax.experimental import pallas as pl
from jax.experimental.pallas import tpu as pltpu

F32 = jnp.float32
BF16 = jnp.bfloat16

EPS = 1e-6
GRID_W = 64
HEAD_DIM = 64
ATT_Q_HEADS = 8
ATT_KV_HEADS = 2
ROPE_BASE = 10000.0
ROPE_PAIRS = HEAD_DIM // 4
DN_HEADS = 4
DN_DK = 64
DN_CHUNK = 64
LRU_WIDTH = 256
LRU_BLOCKS = 4
LRU_C = 8.0
N_MOD = 9
LANES = 128
V7X_VMEM_LIMIT_BYTES = 56 * 1024 * 1024


def _params(*sem):
    return pltpu.CompilerParams(dimension_semantics=sem, vmem_limit_bytes=V7X_VMEM_LIMIT_BYTES)


def _mm(a, b):
    return jnp.dot(a.astype(BF16), b.astype(BF16), preferred_element_type=F32)


def _mm_nt(a, b):
    return lax.dot_general(a.astype(BF16), b.astype(BF16), (((1,), (1,)), ((), ())), preferred_element_type=F32)


def _mm_tn(a, b):
    return lax.dot_general(a.astype(BF16), b.astype(BF16), (((0,), (0,)), ((), ())), preferred_element_type=F32)


def _split3(x):
    hi = x.astype(BF16)
    r = x - hi.astype(F32)
    mid = r.astype(BF16)
    lo = (r - mid.astype(F32)).astype(BF16)
    return hi, mid, lo


def _mm01(x, m01):
    hi, mid, lo = _split3(x)
    d = functools.partial(jnp.dot, preferred_element_type=F32)
    return d(hi, m01) + d(mid, m01) + d(lo, m01)


def _mm01x2(x, m01):
    hi = x.astype(BF16)
    lo = (x - hi.astype(F32)).astype(BF16)
    return jnp.dot(hi, m01, preferred_element_type=F32) + jnp.dot(lo, m01, preferred_element_type=F32)


def _mm01_nt(m01, x):
    hi, mid, lo = _split3(x)
    d = functools.partial(lax.dot_general, dimension_numbers=(((1,), (1,)), ((), ())), preferred_element_type=F32)
    return d(m01, hi) + d(m01, mid) + d(m01, lo)


def _mm3(a, b):
    n = a.shape[0]
    ah = a.astype(BF16)
    al = (a - ah.astype(F32)).astype(BF16)
    bh = b.astype(BF16)
    bl = (b - bh.astype(F32)).astype(BF16)
    top = jnp.dot(jnp.concatenate([ah, al], axis=0), bh, preferred_element_type=F32)
    return (top[0:n] + top[n:2 * n]) + jnp.dot(ah, bl, preferred_element_type=F32)


def _mm3_many(As, Bs):
    n = As[0].shape[0]
    ah = [a.astype(BF16) for a in As]
    bh = [b.astype(BF16) for b in Bs]
    al = [(a - h.astype(F32)).astype(BF16) for a, h in zip(As, ah)]
    bl = [(b - h.astype(F32)).astype(BF16) for b, h in zip(Bs, bh)]
    top = [jnp.dot(jnp.concatenate([h, l], axis=0), b, preferred_element_type=F32) for h, l, b in zip(ah, al, bh)]
    low = [jnp.dot(h, b, preferred_element_type=F32) for h, b in zip(ah, bl)]
    return [(t[0:n] + t[n:2 * n]) + w for t, w in zip(top, low)]


def _silu(x):
    return x * jax.nn.sigmoid(x)


def _softplus(x):
    return jnp.maximum(x, 0.0) + jnp.log1p(jnp.exp(-jnp.abs(x)))


def _gelu_tanh(x):
    return x * (0.5 * (1.0 + jnp.tanh(0.7978845608028654 * (x + 0.044715 * (x * x * x)))))


def _norm_mod(x, nw, scale, shift):
    ms = jnp.mean(x * x, axis=-1, keepdims=True)
    y = (x * lax.rsqrt(ms + EPS)) * nw
    return y * (1.0 + scale) + shift


def _seg_masks(n, seg):
    r = lax.broadcasted_iota(jnp.int32, (n, n), 0)
    c = lax.broadcasted_iota(jnp.int32, (n, n), 1)
    return r, c, (r // seg) == (c // seg)


def _swap16(x):
    w = x.shape[1]
    lane = lax.broadcasted_iota(jnp.int32, x.shape, 1)
    return jnp.where((lane & 16) == 0, pltpu.roll(x, w - 16, axis=1), pltpu.roll(x, 16, axis=1))


def _mod_kernel(c_ref, w_ref, b_ref, o_ref):
    o_ref[...] = _mm(_silu(c_ref[...]), w_ref[...]) + b_ref[...]


def _mod_call(cond, w_mod, b_mod):
    L, D, N = w_mod.shape
    R = cond.shape[0]
    tn = D
    return pl.pallas_call(
        _mod_kernel,
        grid=(L, N // tn),
        in_specs=[pl.BlockSpec((R, D), lambda l, j: (0, 0)),
                  pl.BlockSpec((None, D, tn), lambda l, j: (l, 0, j)),
                  pl.BlockSpec((None, 1, tn), lambda l, j: (l, 0, j))],
        out_specs=pl.BlockSpec((None, R, tn), lambda l, j: (l, 0, j)),
        out_shape=jax.ShapeDtypeStruct((L, R, N), F32),
        compiler_params=_params("parallel", "parallel"),
        name="mod",
    )(cond, w_mod, b_mod.reshape(L, 1, N))


def _ffn_kernel(x_ref, mod_ref, nw_ref, wgu_ref, wd_ref, o_ref, *, sub, F, nc):
    x = x_ref[...]
    h = _norm_mod(x, nw_ref[sub:sub + 1, :], mod_ref[3 * sub + 1:3 * sub + 2, :],
                  mod_ref[3 * sub:3 * sub + 1, :]).astype(BF16)
    dot = functools.partial(jnp.dot, preferred_element_type=F32)
    cf = F // nc

    def gate_up(c):
        return (dot(h, wgu_ref[:, c * cf:(c + 1) * cf]), dot(h, wgu_ref[:, F + c * cf:F + (c + 1) * cf]))

    nxt = gate_up(0)
    acc = None
    for c in range(nc):
        g, u = nxt
        if c + 1 < nc:
            nxt = gate_up(c + 1)
        part = dot((_silu(g) * u).astype(BF16), wd_ref[c * cf:(c + 1) * cf, :])
        acc = part if acc is None else acc + part
    o_ref[...] = x + (0.5 * mod_ref[3 * sub + 2:3 * sub + 3, :]) * acc


def _ffn_call(x, mod, norm_w, w_gu, w_d, *, layer, sub, mod_off, tm=1024):
    B, T, D = x.shape
    F = w_d.shape[1]
    tm = min(tm, T)
    nc = F // 256 if F % 256 == 0 else 1
    return pl.pallas_call(
        functools.partial(_ffn_kernel, sub=sub, F=F, nc=nc),
        grid=(B, T // tm),
        in_specs=[pl.BlockSpec((None, tm, D), lambda b, t: (b, t, 0)),
                  pl.BlockSpec((None, None, N_MOD, D), lambda b, t: (layer, b + mod_off, 0, 0)),
                  pl.BlockSpec((None, 3, D), lambda b, t: (layer, 0, 0)),
                  pl.BlockSpec((None, D, 2 * F), lambda b, t: (layer, 0, 0), pipeline_mode=pl.Buffered(1)),
                  pl.BlockSpec((None, F, D), lambda b, t: (layer, 0, 0), pipeline_mode=pl.Buffered(1))],
        out_specs=pl.BlockSpec((None, tm, D), lambda b, t: (b, t, 0)),
        out_shape=jax.ShapeDtypeStruct((B, T, D), F32),
        compiler_params=_params("parallel", "parallel"),
        name="ffn",
    )(x, mod, norm_w, w_gu, w_d)


def _inproj_kernel(*refs, rope, emit_kv):
    (x_ref, mod_ref, nw_ref, w_ref, wba_ref, qw_ref, kw_ref, bd_ref) = refs[:8]
    refs = refs[8:]
    if rope:
        rc_ref, rs_ref = refs[:2]
        refs = refs[2:]
    qkv_ref, z_ref, xy_ref, q_ref, kT_ref, v_ref, ba_ref = refs[:7]
    refs = refs[7:]

    h = _norm_mod(x_ref[...], nw_ref[1:2, :], mod_ref[4:5, :], mod_ref[3:4, :]).astype(BF16)
    ya = jnp.dot(h, w_ref[:, 0:768], preferred_element_type=F32)
    aq = ya[:, 0:512]
    ak = ya[:, 512:640]
    v = ya[:, 640:768]
    bd = bd_ref[...]
    inv_hd = 1.0 / HEAD_DIM
    ssq = _mm01x2(aq * aq, bd)
    ssk = _mm01x2(ak * ak, bd[0:128, 0:128])
    ba_ref[...] = lax.dot_general(wba_ref[...], h, (((1,), (1,)), ((), ())), preferred_element_type=F32)
    y = jnp.dot(h, w_ref[:, 768:2304], preferred_element_type=F32)
    qkv_ref[...] = y[:, 0:768]
    z_ref[...] = y[:, 768:1024]
    xy_ref[...] = y[:, 1024:1536]
    qn = (aq * lax.rsqrt(ssq * inv_hd + EPS)) * qw_ref[...]
    kn = (ak * lax.rsqrt(ssk * inv_hd + EPS)) * kw_ref[...]
    if emit_kv:
        kf_ref, vf_ref = refs
        kf_ref[...] = kn
        vf_ref[...] = v
    if rope:
        rc = rc_ref[...]
        rs = rs_ref[...]
        kn = kn * rc + _swap16(kn) * rs
        qn = qn * jnp.concatenate([rc] * 4, axis=1) + _swap16(qn) * jnp.concatenate([rs] * 4, axis=1)
    qs = qn * (HEAD_DIM ** -0.5 * math.log2(math.e))
    for hh in range(ATT_Q_HEADS):
        q_ref[hh] = qs[:, hh * HEAD_DIM:(hh + 1) * HEAD_DIM].astype(BF16)
    kT = kn.T
    kT_ref[0] = kT[0:HEAD_DIM, :].astype(BF16)
    kT_ref[1] = kT[HEAD_DIM:2 * HEAD_DIM, :].astype(BF16)
    lane = lax.broadcasted_iota(jnp.int32, v.shape, 1)
    tail = jnp.where(lane == HEAD_DIM, 1.0, 0.0)
    v_ref[0] = jnp.where(lane < HEAD_DIM, v, tail).astype(BF16)
    v_ref[1] = jnp.where(lane < HEAD_DIM, pltpu.roll(v, HEAD_DIM, axis=1), tail).astype(BF16)


def _inproj_call(x, mod, norm_w, w_main, w_ba, qw, kw, bd512, rope_tabs, *, layer, mod_off, emit_kv, tm=1024):
    B, T, D = x.shape
    tm = min(tm, T)
    NW = w_main.shape[2]
    rope = rope_tabs is not None
    im = lambda b, t: (b, t, 0)
    in_specs = [pl.BlockSpec((None, tm, D), im),
                pl.BlockSpec((None, None, N_MOD, D), lambda b, t: (layer, b + mod_off, 0, 0)),
                pl.BlockSpec((None, 3, D), lambda b, t: (layer, 0, 0)),
                pl.BlockSpec((None, D, NW), lambda b, t: (layer, 0, 0)),
                pl.BlockSpec((None, 16, D), lambda b, t: (layer, 0, 0)),
                pl.BlockSpec((None, 1, 512), lambda b, t: (layer, 0, 0)),
                pl.BlockSpec((None, 1, 128), lambda b, t: (layer, 0, 0)),
                pl.BlockSpec((512, 512), lambda b, t: (0, 0))]
    args = [x, mod, norm_w, w_main, w_ba, qw, kw, bd512]
    if rope:
        in_specs += [pl.BlockSpec((tm, 128), lambda b, t: (t, 0))] * 2
        args += list(rope_tabs)
    out_shape = [jax.ShapeDtypeStruct((B, T, 768), F32), jax.ShapeDtypeStruct((B, T, 256), F32),
                 jax.ShapeDtypeStruct((B, T, 512), F32), jax.ShapeDtypeStruct((B, ATT_Q_HEADS, T, HEAD_DIM), BF16),
                 jax.ShapeDtypeStruct((B, ATT_KV_HEADS, HEAD_DIM, T), BF16),
                 jax.ShapeDtypeStruct((B, ATT_KV_HEADS, T, 128), BF16),
                 jax.ShapeDtypeStruct((B, 16, T), F32)]
    out_specs = [pl.BlockSpec((None, tm, 768), im), pl.BlockSpec((None, tm, 256), im), pl.BlockSpec((None, tm, 512), im),
                 pl.BlockSpec((None, ATT_Q_HEADS, tm, HEAD_DIM), lambda b, t: (b, 0, t, 0)),
                 pl.BlockSpec((None, ATT_KV_HEADS, HEAD_DIM, tm), lambda b, t: (b, 0, 0, t)),
                 pl.BlockSpec((None, ATT_KV_HEADS, tm, 128), lambda b, t: (b, 0, t, 0)),
                 pl.BlockSpec((None, 16, tm), lambda b, t: (b, 0, t))]
    if emit_kv:
        out_shape += [jax.ShapeDtypeStruct((B, T, 128), F32)] * 2
        out_specs += [pl.BlockSpec((None, tm, 128), im)] * 2
    return pl.pallas_call(
        functools.partial(_inproj_kernel, rope=rope, emit_kv=emit_kv),
        grid=(B, T // tm),
        in_specs=in_specs, out_specs=out_specs, out_shape=out_shape,
        compiler_params=_params("parallel", "parallel"),
        name="inproj",
    )(*args)


def _conv_window(main, prev, nxt, at_start, at_end):
    prev = jnp.where(at_start, 0.0, prev)
    nxt = jnp.where(at_end, 0.0, nxt)
    return jnp.concatenate([prev, main, nxt], axis=0)


def _conv4(win, w, n):
    return (w[0:1, :] * win[6:6 + n] + w[1:2, :] * win[7:7 + n]) + (w[2:3, :] * win[8:8 + n] + w[3:4, :] * win[9:9 + n])


def _dnchunk_kernel(q_ref, qp_ref, qn_ref, k_ref, kp_ref, kn_ref, v_ref, vp_ref, vn_ref, cwq_ref, cwk_ref, cwv_ref,
                    ba_ref, prm_ref, p_ref, qm_ref, o1_ref, o2_ref, gs_ref, *, G, groups):
    C = DN_CHUNK
    W = 2 * C
    r, c, same = _seg_masks(W, C)
    i_loc = r % C
    j_loc = c % C
    m01 = lambda mask: jnp.where(mask, 1.0, 0.0).astype(BF16)
    cum_f = m01(same & (i_loc <= j_loc))
    cum_b = m01(same & (i_loc >= j_loc))
    ones_bd = m01(same)
    eye = m01(r == c)
    eye_f = jnp.where(r == c, 1.0, 0.0)
    blk = tuple((r // s) == (c // s) for s in (8, 16, 32, 64))
    incl = (same & (i_loc >= j_loc), same & (i_loc <= j_loc))
    strict = (same & (i_loc > j_loc), same & (i_loc < j_loc))
    head0 = lax.broadcasted_iota(jnp.int32, (C, W), 1) < C
    row2 = lax.broadcasted_iota(jnp.int32, (2, W), 0)
    alog = prm_ref[0:2, :]
    dtb = prm_ref[2:4, :]

    jblk = pl.program_id(2)
    at_start = jblk == 0
    at_end = jblk == pl.num_programs(2) - 1

    def window(main_ref, prev_ref, next_ref):
        return _conv_window(main_ref[...], prev_ref[...], next_ref[...], at_start, at_end)

    def l2norm(x):
        return x * lax.rsqrt(_mm01(x * x, ones_bd) + EPS)

    def stack(x):
        return jnp.concatenate([jnp.where(head0, x, 0.0), jnp.where(head0, 0.0, x)], axis=0)

    win_q = window(q_ref, qp_ref, qn_ref)
    win_k = window(k_ref, kp_ref, kn_ref)
    win_v = window(v_ref, vp_ref, vn_ref)
    dot = functools.partial(jnp.dot, preferred_element_type=F32)
    col = lambda cols, gi, k: cols[gi][:, k:k + 1]

    gsz = G // groups
    for grp in range(groups):
        chunks = list(range(grp * gsz, (grp + 1) * gsz))
        r0 = grp * gsz * C
        n = gsz * C

        def conv_silu(win, cw_ref):
            return _silu(_conv4(win[r0:r0 + n + 16], cw_ref[...], n))

        q_all = l2norm(conv_silu(win_q, cwq_ref)) * (DN_DK ** -0.5)
        k_all = l2norm(conv_silu(win_k, cwk_ref))
        v_all = conv_silu(win_v, cwv_ref)
        rows_of = lambda x, gi: x[(gi - grp * gsz) * C:(gi - grp * gsz + 1) * C, :]

        bg = {gi: ba_ref[gi] for gi in chunks}
        beta = {gi: jax.nn.sigmoid(bg[gi][0:2, :]) for gi in chunks}
        g = {gi: -jnp.exp(alog) * _softplus(bg[gi][2:4, :] + dtb) for gi in chunks}
        gcf = {gi: _mm01(g[gi], cum_f) for gi in chunks}
        gcb = {gi: _mm01(g[gi], cum_b) for gi in chunks}
        tot = {gi: _mm01(g[gi], ones_bd) for gi in chunks}
        rows = {gi: jnp.concatenate([beta[gi], jnp.where(row2 == 0, gcf[gi], gcb[gi]), tot[gi],
                                     jnp.zeros((2, W), F32)], axis=0) for gi in chunks}
        cols = {gi: _mm01_nt(eye, rows[gi]) for gi in chunks}
        Kst = {gi: stack(rows_of(k_all, gi)) for gi in chunks}
        Qst = {gi: stack(rows_of(q_all, gi)) for gi in chunks}
        Vst = {gi: stack(rows_of(v_all, gi)) for gi in chunks}
        kq = {gi: _mm_nt(jnp.concatenate([Kst[gi], Qst[gi]], axis=0), Kst[gi]) for gi in chunks}

        chains = [(gi, d) for gi in chunks for d in range(2)]
        dec = [jnp.exp(jnp.where(incl[d], col(cols, gi, 2 + d) - rows[gi][2 + d:3 + d, :], -jnp.inf))
               for gi, d in chains]
        L = [(col(cols, gi, d) * kq[gi][0:W]) * jnp.where(strict[d], dec[i], 0.0) for i, (gi, d) in enumerate(chains)]
        QK = [kq[gi][W:2 * W] * dec[i] for i, (gi, d) in enumerate(chains)]
        D8f = [jnp.where(blk[0], x, 0.0) for x in L]
        D8 = [x.astype(BF16) for x in D8f]
        M = [dot(d8, d8) for d8 in D8]
        Mb = [m.astype(BF16) for m in M]
        DM = [dot(d8, mb) for d8, mb in zip(D8, Mb)]
        R = [(m - d8) - dm for m, d8, dm in zip(M, D8f, DM)]
        M = [dot(mb, mb) for mb in Mb]
        RM = [dot(r_.astype(BF16), m.astype(BF16)) for r_, m in zip(R, M)]
        Tm = [eye_f + ((r_ + m) + rm) for r_, m, rm in zip(R, M, RM)]
        for lvl in range(3):
            off = blk[lvl + 1] & jnp.logical_not(blk[lvl])
            Th = [t.astype(BF16) for t in Tm]
            Bm = [jnp.where(off, x, 0.0) for x in L]
            Bh = [b.astype(BF16) for b in Bm]
            Bl = [(b - h.astype(F32)).astype(BF16) for b, h in zip(Bm, Bh)]
            TB2 = [dot(th, jnp.concatenate([bh, bl], axis=1)) for th, bh, bl in zip(Th, Bh, Bl)]
            TB = [x[:, 0:W] + x[:, W:2 * W] for x in TB2]
            TBh = [x.astype(BF16) for x in TB]
            TBl = [(x - h.astype(F32)).astype(BF16) for x, h in zip(TB, TBh)]
            TBT2 = [dot(jnp.concatenate([h, lo], axis=0), th) for h, lo, th in zip(TBh, TBl, Th)]
            Tm = [t - (x[0:W] + x[W:2 * W]) for t, x in zip(Tm, TBT2)]
        rhs = [jnp.concatenate([(col(cols, gi, d) * jnp.exp(col(cols, gi, 2 + d))) * Kst[gi],
                                col(cols, gi, d) * Vst[gi]], axis=1) for gi, d in chains]
        X = _mm3_many(Tm, rhs)
        kd = [Kst[gi] * jnp.exp(col(cols, gi, 4 + d) - col(cols, gi, 2 + d)) for gi, d in chains]
        PQ = [_mm_tn(a, x) for a, x in zip(kd, X)]
        OO = [_mm(a, x) for a, x in zip(QK, X)]
        for i, (gi, d) in enumerate(chains):
            o1 = Qst[gi] * jnp.exp(col(cols, gi, 2 + d)) - OO[i][:, 0:W]
            o2 = OO[i][:, W:2 * W]
            p_ref[d, gi] = PQ[i][:, 0:W].astype(BF16)
            qm_ref[d, gi] = PQ[i][:, W:2 * W]
            o1_ref[d, gi] = (o1[0:C] + o1[C:W]).astype(BF16)
            o2_ref[d, gi] = o2[0:C] + o2[C:W]
            gs_ref[d, gi] = jnp.exp(rows[gi][4 + d:5 + d, :])


def _dnchunk_call(qkv, conv_w, ba, prm, *, layer, G=16):
    B, T, _ = qkv.shape
    C = DN_CHUNK
    W = 2 * C
    n = T // C
    G = min(G, n)
    rb = G * C // 8
    nb8 = T // 8
    sds = jax.ShapeDtypeStruct
    mat = lambda rows, dt: (sds((B, 2, 2, n, rows, W), dt),
                            pl.BlockSpec((None, None, 2, G, rows, W), lambda b, p, j: (b, p, 0, j, 0, 0)))
    outs = [mat(W, BF16), mat(W, F32), mat(C, BF16), mat(C, F32), mat(1, F32)]
    in_specs = []
    for part in range(3):
        in_specs += [pl.BlockSpec((None, G * C, W), lambda b, p, j, part=part: (b, j, 2 * part + p)),
                     pl.BlockSpec((None, 8, W), lambda b, p, j, part=part: (b, jnp.maximum(j * rb - 1, 0), 2 * part + p)),
                     pl.BlockSpec((None, 8, W),
                                  lambda b, p, j, part=part: (b, jnp.minimum((j + 1) * rb, nb8 - 1), 2 * part + p))]
    in_specs += [pl.BlockSpec((None, 4, W), lambda b, p, j, part=part: (layer, 0, 2 * part + p)) for part in range(3)]
    in_specs += [pl.BlockSpec((None, None, G, 4, W), lambda b, p, j: (b, p, j, 0, 0)),
                 pl.BlockSpec((None, None, 4, W), lambda b, p, j: (layer, p, 0, 0))]
    return pl.pallas_call(
        functools.partial(_dnchunk_kernel, G=G, groups=1),
        grid=(B, 2, n // G),
        in_specs=in_specs,
        out_specs=[o[1] for o in outs],
        out_shape=[o[0] for o in outs],
        compiler_params=_params("parallel", "parallel", "parallel"),
        name="dnchunk",
    )(*([qkv] * 9), conv_w, conv_w, conv_w, ba, prm)


def _dnscan_kernel(pf_ref, qf_ref, o1f_ref, o2f_ref, gf_ref, pb_ref, qb_ref, o1b_ref, o2b_ref, gb_ref, s0_ref,
                   of_ref, ob_ref, sfin_ref, s_sc, *, Gs):
    C = DN_CHUNK
    W = 2 * C
    j = pl.program_id(1)

    @pl.when(j == 0)
    def _():
        s_sc[...] = s0_ref[...]

    S = [[s_sc[p, d] for d in range(2)] for p in range(2)]
    fwd = (pf_ref, qf_ref, o1f_ref, o2f_ref, gf_ref, of_ref)
    bwd = (pb_ref, qb_ref, o1b_ref, o2b_ref, gb_ref, ob_ref)
    for i in range(Gs):
        for d, (P, Qm, O1, O2, GS, out) in enumerate((fwd, bwd)):
            ci = i if d == 0 else Gs - 1 - i
            for p in range(2):
                Sb = S[p][d].astype(BF16)
                out[ci * C:(ci + 1) * C, p * W:(p + 1) * W] = (
                    jnp.dot(O1[p, ci], Sb, preferred_element_type=F32) + O2[p, ci])
                S[p][d] = (GS[p, ci] * S[p][d] - jnp.dot(P[p, ci], Sb, preferred_element_type=F32)) + Qm[p, ci]
    for p in range(2):
        for d in range(2):
            s_sc[p, d] = S[p][d]

    @pl.when(j == pl.num_programs(1) - 1)
    def _():
        sfin_ref[...] = s_sc[...]


def _dnscan_call(ops, s0, *, T, Gs=8):
    B = s0.shape[0]
    C = DN_CHUNK
    W = 2 * C
    n = T // C
    Gs = min(Gs, n)
    nb = n // Gs
    specs = []
    for d in range(2):
        for a in ops:
            rows = a.shape[4]
            if d == 0:
                specs.append(pl.BlockSpec((None, 2, None, Gs, rows, W), lambda b, j: (b, 0, 0, j, 0, 0)))
            else:
                specs.append(pl.BlockSpec((None, 2, None, Gs, rows, W), lambda b, j: (b, 0, 1, nb - 1 - j, 0, 0)))
    st_spec = pl.BlockSpec((None, 2, 2, W, W), lambda b, j: (b, 0, 0, 0, 0))
    return pl.pallas_call(
        functools.partial(_dnscan_kernel, Gs=Gs),
        grid=(B, nb),
        in_specs=specs + [st_spec],
        out_specs=[pl.BlockSpec((None, Gs * C, 2 * W), lambda b, j: (b, j, 0)),
                   pl.BlockSpec((None, Gs * C, 2 * W), lambda b, j: (b, nb - 1 - j, 0)),
                   st_spec],
        out_shape=[jax.ShapeDtypeStruct((B, T, 2 * W), F32), jax.ShapeDtypeStruct((B, T, 2 * W), F32),
                   jax.ShapeDtypeStruct((B, 2, 2, W, W), F32)],
        scratch_shapes=[pltpu.VMEM((2, 2, W, W), F32)],
        compiler_params=_params("parallel", "arbitrary"),
        name="dnscan",
    )(*ops, *ops, s0)


def _attn_kernel(*refs, seg_blocks, tq, rt):
    q_ref = refs[0]
    nseg = len(seg_blocks)
    e_ref, o_ref = refs[1 + 2 * nseg:3 + 2 * nseg]
    G = ATT_Q_HEADS // ATT_KV_HEADS
    rows = G * tq
    rt = min(rt, rows)
    for g in range(ATT_KV_HEADS):
        q = q_ref[g * G:(g + 1) * G].reshape(rows, HEAD_DIM)
        blocks = [(si, j, kb) for si, (nblk, kb) in enumerate(seg_blocks) for j in range(nblk)]

        def scores(b):
            si, j, kb = blocks[b]
            return jnp.dot(q, refs[1 + 2 * si][g, :, j * kb:(j + 1) * kb], preferred_element_type=F32)

        m = None
        acc = None
        s_next = scores(0)
        for b, (si, j, kb) in enumerate(blocks):
            s_all = s_next
            if b + 1 < len(blocks):
                s_next = scores(b + 1)
            v = refs[2 + 2 * si][g, j * kb:(j + 1) * kb, :]
            S = [s_all[r * rt:(r + 1) * rt] for r in range(rows // rt)]
            smax = [jnp.max(s, axis=-1, keepdims=True) for s in S]
            if m is None:
                m_new = smax
            else:
                m_new = [jnp.maximum(a, b_) for a, b_ in zip(m, smax)]
                alpha = [jnp.exp2(a - b_) for a, b_ in zip(m, m_new)]
            P = [jnp.exp2((s - mn).astype(BF16)) for s, mn in zip(S, m_new)]
            pv = jnp.dot(jnp.concatenate(P, axis=0), v, preferred_element_type=F32)
            acc = pv if acc is None else jnp.concatenate(alpha, axis=0) * acc + pv
            m = m_new
        o = (acc / acc[:, HEAD_DIM:HEAD_DIM + 1]).astype(BF16)
        out = jnp.dot(o[0:tq], e_ref[0], preferred_element_type=F32)
        for hh in range(1, G):
            out = out + jnp.dot(o[hh * tq:(hh + 1) * tq], e_ref[hh], preferred_element_type=F32)
        o_ref[:, g * G * HEAD_DIM:(g + 1) * G * HEAD_DIM] = out.astype(BF16)


def _attn_call(q, segs, place, *, grid, q_map, seg_maps, out_map, out_rows, tq, kb, rt=512):
    G = ATT_Q_HEADS // ATT_KV_HEADS
    in_specs = [pl.BlockSpec((None, ATT_Q_HEADS, tq, HEAD_DIM), q_map)]
    args = [q]
    seg_blocks = []
    for (kT, v, S), (k_map, v_map) in zip(segs, seg_maps):
        blk = min(kb, S)
        seg_blocks.append((S // blk, blk))
        in_specs.append(pl.BlockSpec((None,) * (kT.ndim - 3) + (ATT_KV_HEADS, HEAD_DIM, S), k_map))
        in_specs.append(pl.BlockSpec((None,) * (v.ndim - 3) + (ATT_KV_HEADS, S, 2 * HEAD_DIM), v_map))
        args += [kT, v]
    in_specs.append(pl.BlockSpec((G, 2 * HEAD_DIM, G * HEAD_DIM), lambda b, t: (0, 0, 0)))
    args.append(place)
    return pl.pallas_call(
        functools.partial(_attn_kernel, seg_blocks=tuple(seg_blocks), tq=tq, rt=rt),
        grid=grid,
        in_specs=in_specs,
        out_specs=pl.BlockSpec((None, tq, ATT_Q_HEADS * HEAD_DIM), out_map),
        out_shape=jax.ShapeDtypeStruct(out_rows + (ATT_Q_HEADS * HEAD_DIM,), BF16),
        compiler_params=_params("parallel", "parallel"),
        name="attn",
    )(*args)


def _lru_kernel(xf_ref, xfp_ref, xfn_ref, xb_ref, xbp_ref, xbn_ref, cw_ref, cb_ref, wr_ref, br_ref, wi_ref, bi_ref,
                lam_ref, h0_ref, hf_ref, hb_ref, hfin_ref, h_sc, a_sc, u_sc, *, SB, tt):
    W = LRU_WIDTH
    j = pl.program_id(1)
    last = pl.num_programs(1) - 1
    cw = cw_ref[...]
    cb = cb_ref[...]
    rid = lax.broadcasted_iota(jnp.int32, (8, W), 0)

    @pl.when(j == 0)
    def _():
        for s in range(SB):
            for d in range(2):
                h_sc[2 * s + d] = jnp.broadcast_to(h0_ref[s, d:d + 1, :], (8, W))

    def gates(x_ref, p_ref, n_ref, at_start, at_end, d):
        x = jnp.concatenate(
            [_conv4(_conv_window(x_ref[s, :, 0:W], p_ref[s, :, 0:W], n_ref[s, :, 0:W], at_start, at_end), cw, tt)
             for s in range(SB)], axis=0) + cb
        rg = jax.nn.sigmoid(_mm(x, wr_ref[d]) + br_ref[d:d + 1, :])
        ig = jax.nn.sigmoid(_mm(x, wi_ref[d]) + bi_ref[d:d + 1, :])
        log_a = (-LRU_C * rg) * _softplus(-lam_ref[d:d + 1, :])
        a = jnp.exp(log_a)
        a_sc[d] = a
        u_sc[d] = jnp.sqrt(-jnp.tanh(log_a) * (a * a + 1.0)) * (ig * x)

    gates(xf_ref, xfp_ref, xfn_ref, j == 0, j == last, 0)
    gates(xb_ref, xbp_ref, xbn_ref, j == last, j == 0, 1)

    chains = [(s, d) for s in range(SB) for d in range(2)]
    nblk = tt // 8

    def scan8(j8, hs):
        blk = [j8 if d == 0 else nblk - 1 - j8 for _, d in chains]
        row0 = [pl.multiple_of(s * tt + b * 8, 8) for (s, _), b in zip(chains, blk)]
        a8 = [a_sc[d, pl.ds(r0, 8), :] for (_, d), r0 in zip(chains, row0)]
        u8 = [u_sc[d, pl.ds(r0, 8), :] for (_, d), r0 in zip(chains, row0)]
        hs = list(hs)
        out = [jnp.zeros((8, W), F32)] * len(chains)
        for step in range(8):
            for i, (_, d) in enumerate(chains):
                r = step if d == 0 else 7 - step
                hs[i] = (jnp.broadcast_to(a8[i][r:r + 1, :], (8, W)) * hs[i]
                         + jnp.broadcast_to(u8[i][r:r + 1, :], (8, W)))
                out[i] = jnp.where(rid == r, hs[i], out[i])
        for i, (s, d) in enumerate(chains):
            dst = hf_ref if d == 0 else hb_ref
            dst[s, pl.ds(pl.multiple_of(blk[i] * 8, 8), 8), :] = out[i]
        return tuple(hs)

    hs = lax.fori_loop(0, nblk, scan8, tuple(h_sc[i] for i in range(len(chains))))
    for i in range(len(chains)):
        h_sc[i] = hs[i]

    @pl.when(j == last)
    def _():
        for i, (s, d) in enumerate(chains):
            hfin_ref[s, d:d + 1, :] = hs[i][0:1, :]


def _lru_call(xy, conv_w, conv_b, wr, br, wi, bi, lam, h0, *, layer, h0_layer, SB=4):
    B, T, _ = xy.shape
    W = LRU_WIDTH
    tt = min(256, T)
    nt = T // tt
    SB = min(SB, B)
    rb = tt // 8
    nb8 = T // 8
    lmap = lambda b, j: (layer, 0, 0)
    h0_spec = (pl.BlockSpec((SB, 2, W), lambda b, j: (b, 0, 0)) if h0_layer is None
               else pl.BlockSpec((SB, None, 2, W), lambda b, j: (b, h0_layer, 0, 0)))
    fwd = lambda b, j: j
    bwd = lambda b, j: nt - 1 - j
    tiles = []
    for tile in (fwd, bwd):
        tiles += [pl.BlockSpec((SB, tt, 2 * W), lambda b, j, tile=tile: (b, tile(b, j), 0)),
                  pl.BlockSpec((SB, 8, 2 * W), lambda b, j, tile=tile: (b, jnp.maximum(tile(b, j) * rb - 1, 0), 0)),
                  pl.BlockSpec((SB, 8, 2 * W),
                               lambda b, j, tile=tile: (b, jnp.minimum((tile(b, j) + 1) * rb, nb8 - 1), 0))]
    return pl.pallas_call(
        functools.partial(_lru_kernel, SB=SB, tt=tt),
        grid=(B // SB, nt),
        in_specs=tiles + [pl.BlockSpec((None, 4, W), lmap),
                          pl.BlockSpec((None, 1, W), lmap),
                          pl.BlockSpec((None, 2, W, W), lambda b, j: (layer, 0, 0, 0)),
                          pl.BlockSpec((None, 2, W), lmap),
                          pl.BlockSpec((None, 2, W, W), lambda b, j: (layer, 0, 0, 0)),
                          pl.BlockSpec((None, 2, W), lmap),
                          pl.BlockSpec((None, 2, W), lmap),
                          h0_spec],
        out_specs=[pl.BlockSpec((SB, tt, W), lambda b, j: (b, j, 0)),
                   pl.BlockSpec((SB, tt, W), lambda b, j: (b, nt - 1 - j, 0)),
                   pl.BlockSpec((SB, 2, W), lambda b, j: (b, 0, 0))],
        out_shape=[jax.ShapeDtypeStruct((B, T, W), F32), jax.ShapeDtypeStruct((B, T, W), F32),
                   jax.ShapeDtypeStruct((B, 2, W), F32)],
        scratch_shapes=[pltpu.VMEM((2 * SB, 8, W), F32), pltpu.VMEM((2, SB * tt, W), F32),
                        pltpu.VMEM((2, SB * tt, W), F32)],
        compiler_params=_params("parallel", "arbitrary"),
        name="lru",
    )(*([xy] * 6), conv_w, conv_b, wr, br, wi, bi, lam, h0)


def _merge_kernel(x_ref, mod_ref, nw_ref, of_ref, ob_ref, onw_ref, bd_ref, z_ref, oatt_ref, hf_ref, hb_ref, xy_ref,
                  wg_ref, wpa_ref, wpb_ref, wpc_ref, wo_ref, o_ref):
    D = x_ref.shape[-1]
    x = x_ref[...]
    h = _norm_mod(x, nw_ref[1:2, :], mod_ref[4:5, :], mod_ref[3:4, :]).astype(BF16)
    gates = jax.nn.sigmoid(jnp.dot(h, wg_ref[...], preferred_element_type=F32))
    odn = of_ref[...] + ob_ref[...]
    ms = _mm01(odn * odn, bd_ref[...]) * (1.0 / DN_DK)
    odn = (odn * lax.rsqrt(ms + EPS)) * onw_ref[...]
    a = _mm(odn * _silu(z_ref[...]), wpa_ref[...])
    b = jnp.dot(oatt_ref[...], wpb_ref[...], preferred_element_type=F32)
    c = _mm(_gelu_tanh(xy_ref[:, LRU_WIDTH:2 * LRU_WIDTH]) * (hf_ref[...] + hb_ref[...]), wpc_ref[...])
    merged = (gates[:, 0:D] * a + gates[:, D:2 * D] * b) + gates[:, 2 * D:3 * D] * c
    o_ref[...] = x + mod_ref[5:6, :] * _mm(merged, wo_ref[...])


def _merge_call(x, mod, norm_w, o_f, o_b, onw, bd256, z, o_att, h_f, h_b, xy, w_gate, w_pa, w_pb, w_pc, w_o, *,
                layer, mod_off, tm=512):
    B, T, D = x.shape
    tm = min(tm, T)
    im = lambda b, t: (b, t, 0)
    wspec = lambda w: pl.BlockSpec((None,) + w.shape[1:], lambda b, t: (layer, 0, 0), pipeline_mode=pl.Buffered(1))
    return pl.pallas_call(
        _merge_kernel,
        grid=(B, T // tm),
        in_specs=[pl.BlockSpec((None, tm, D), im),
                  pl.BlockSpec((None, None, N_MOD, D), lambda b, t: (layer, b + mod_off, 0, 0)),
                  pl.BlockSpec((None, 3, D), lambda b, t: (layer, 0, 0)),
                  pl.BlockSpec((None, tm, 256), im), pl.BlockSpec((None, tm, 256), im),
                  pl.BlockSpec((None, 1, 256), lambda b, t: (layer, 0, 0)),
                  pl.BlockSpec((256, 256), lambda b, t: (0, 0)),
                  pl.BlockSpec((None, tm, 256), im),
                  pl.BlockSpec((None, tm, 512), im),
                  pl.BlockSpec((None, tm, 256), im), pl.BlockSpec((None, tm, 256), im), pl.BlockSpec((None, tm, 512), im),
                  wspec(w_gate), wspec(w_pa), wspec(w_pb), wspec(w_pc), wspec(w_o)],
        out_specs=pl.BlockSpec((None, tm, D), im),
        out_shape=jax.ShapeDtypeStruct((B, T, D), F32),
        compiler_params=_params("parallel", "parallel"),
        name="merge",
    )(x, mod, norm_w, o_f, o_b, onw, bd256, z, o_att, h_f, h_b, xy, w_gate, w_pa, w_pb, w_pc, w_o)


def _blockdiag_ones(n, seg):
    i = np.arange(n)
    return jnp.asarray((i[:, None] // seg) == (i[None, :] // seg), BF16)


def _placement():
    G = ATT_Q_HEADS // ATT_KV_HEADS
    e = np.zeros((G, 2 * HEAD_DIM, G * HEAD_DIM), np.float32)
    d = np.arange(HEAD_DIM)
    for hh in range(G):
        e[hh, d, hh * HEAD_DIM + d] = 1.0
    return jnp.asarray(e, BF16)


def _with_ones_column(v):
    one = jnp.ones(v.shape[:-1] + (1,), v.dtype)
    zero = jnp.zeros(v.shape[:-1] + (HEAD_DIM - 1,), v.dtype)
    return jnp.concatenate([v, one, zero], axis=-1).astype(BF16)


def _rope_tables(n_tokens):
    rows = n_tokens // GRID_W
    row = jnp.broadcast_to(jnp.arange(rows, dtype=F32)[:, None], (rows, GRID_W)).reshape(-1)
    col = jnp.broadcast_to(jnp.arange(GRID_W, dtype=F32)[None, :], (rows, GRID_W)).reshape(-1)
    freqs = ROPE_BASE ** (-jnp.arange(ROPE_PAIRS, dtype=F32) / ROPE_PAIRS)
    ang = jnp.stack([row[:, None] * freqs, col[:, None] * freqs], axis=1)
    cos = jnp.cos(ang)[:, :, None, :]
    sin = jnp.sin(ang)[:, :, None, :]
    c = jnp.broadcast_to(cos, (n_tokens, 2, 2, ROPE_PAIRS)).reshape(n_tokens, HEAD_DIM)
    s = jnp.concatenate([-sin, sin], axis=2).reshape(n_tokens, HEAD_DIM)
    return jnp.concatenate([c, c], axis=1), jnp.concatenate([s, s], axis=1)


def _ba_layout(ba, n):
    B = ba.shape[0]
    x = ba.reshape(B, 2, 2, 2, 2, n, DN_CHUNK)
    x = x.transpose(0, 3, 5, 1, 2, 4, 6)
    return x.reshape(B, 2, n, 4, 2 * DN_CHUNK)


def _state_to_blockdiag(s):
    B = s.shape[0]
    x = s.reshape(B, 2, 2, 2, DN_DK, DN_DK)
    z = jnp.zeros_like(x[:, :, :, 0])
    top = jnp.concatenate([x[:, :, :, 0], z], axis=-1)
    bot = jnp.concatenate([z, x[:, :, :, 1]], axis=-1)
    return jnp.concatenate([top, bot], axis=-2).transpose(0, 2, 1, 3, 4)


def _blockdiag_to_state(sb):
    B = sb.shape[0]
    x = sb.transpose(0, 2, 1, 3, 4)
    h0 = x[..., 0:DN_DK, 0:DN_DK]
    h1 = x[..., DN_DK:, DN_DK:]
    return jnp.stack([h0, h1], axis=3).reshape(B, 2, DN_HEADS, DN_DK, DN_DK)


def _lru_blockdiag(w):
    L = w.shape[0]
    bw = LRU_WIDTH // LRU_BLOCKS
    out = jnp.zeros((L, 2, LRU_WIDTH, LRU_WIDTH), w.dtype)
    for n in range(LRU_BLOCKS):
        out = out.at[:, :, n * bw:(n + 1) * bw, n * bw:(n + 1) * bw].set(w[:, :, n])
    return out


def kernel(x_prompt, x_sample, cache_k, cache_v, state_delta, state_lru, c, c_ctx, w_mod, b_mod, norm_w, ffn1_wgu,
           ffn1_wd, ffn2_wgu, ffn2_wd, w_in, dn_conv_w, dn_a_log, dn_dt_bias, dn_onorm_w, att_qnorm_w, att_knorm_w,
           lru_conv_w, lru_conv_b, lru_wr, lru_br, lru_wi, lru_bi, lru_lam, w_pa, w_pb, w_pc, w_o):
    NB, SEQ, D = x_prompt.shape
    DB, DSEQ, _ = x_sample.shape
    L = w_mod.shape[0]
    PAST = cache_k.shape[2]
    TC = NB * SEQ

    cond = jnp.zeros((16, D), F32).at[0].set(c_ctx).at[1:1 + DB].set(c)
    mod = _mod_call(cond, w_mod, b_mod).reshape(L, 16, N_MOD, D)

    bf = lambda w: w.astype(BF16)
    ffn1_wgu, ffn1_wd, ffn2_wgu, ffn2_wd = bf(ffn1_wgu), bf(ffn1_wd), bf(ffn2_wgu), bf(ffn2_wd)
    w_main = bf(jnp.concatenate([w_in[:, :, 1040:1808], w_in[:, :, 0:1024], w_in[:, :, 1808:2320]], axis=-1))
    w_ba = bf(jnp.swapaxes(w_in[:, :, 1024:1040], 1, 2))
    w_gate = bf(w_in[:, :, 2320:])
    w_pa, w_pb, w_pc, w_o = bf(w_pa), bf(w_pb), bf(w_pc), bf(w_o)
    qw = jnp.tile(att_qnorm_w, (1, ATT_Q_HEADS)).reshape(L, 1, ATT_Q_HEADS * HEAD_DIM)
    kw = jnp.tile(att_knorm_w, (1, ATT_KV_HEADS)).reshape(L, 1, ATT_KV_HEADS * HEAD_DIM)
    onw = jnp.tile(dn_onorm_w, (1, DN_HEADS)).reshape(L, 1, DN_HEADS * DN_DK)
    pr = lambda p: jnp.repeat(p.reshape(L, 2, 2, 2), DN_CHUNK, axis=-1).reshape(L, 2, 2, 2 * DN_CHUNK).transpose(0, 2, 1, 3)
    dn_prm = jnp.concatenate([pr(dn_a_log), pr(dn_dt_bias)], axis=2)
    wr_bd, wi_bd = bf(_lru_blockdiag(lru_wr)), bf(_lru_blockdiag(lru_wi))
    lru_cb = lru_conv_b.reshape(L, 1, LRU_WIDTH)
    bd512 = _blockdiag_ones(512, HEAD_DIM)
    bd256 = _blockdiag_ones(256, DN_DK)
    place = _placement()
    rope_tabs = _rope_tables(DSEQ)
    cache_kT = bf(cache_k.transpose(0, 1, 3, 4, 2))
    cache_v1 = _with_ones_column(cache_v.transpose(0, 1, 3, 2, 4))
    s0_lat = _state_to_blockdiag(state_delta.transpose(1, 0, 2, 3, 4, 5).reshape(L * DB, 2, DN_HEADS, DN_DK, DN_DK))
    s0_lat = s0_lat.reshape(L, DB, 2, 2, 2 * DN_DK, 2 * DN_DK)
    s0_ctx = jnp.zeros((NB, 2, 2, 2 * DN_DK, 2 * DN_DK), F32)
    h0_ctx = jnp.zeros((NB, 2, LRU_WIDTH), F32)

    xp = x_prompt.reshape(1, TC, D)
    xs = x_sample
    new_k, new_v, new_sd, new_sl = [], [], [], []
    for l in range(L):
        for ctx in (True, False):
            x = xp if ctx else xs
            mod_off = 0 if ctx else 1
            seq = SEQ if ctx else DSEQ
            nseq = NB if ctx else DB
            shp = x.shape[:2]
            x = _ffn_call(x, mod, norm_w, ffn1_wgu, ffn1_wd, layer=l, sub=0, mod_off=mod_off)
            outs = _inproj_call(x, mod, norm_w, w_main, w_ba, qw, kw, bd512, None if ctx else rope_tabs,
                                layer=l, mod_off=mod_off, emit_kv=ctx)
            qkv, z, xy, q_hm, kT, v_bf, ba = outs[:7]
            n = seq // DN_CHUNK
            ba_l = _ba_layout(ba.reshape(shp[0], 16, -1, seq).transpose(0, 2, 1, 3).reshape(nseq, 16, seq), n)
            ops = _dnchunk_call(qkv.reshape(nseq, seq, 768), dn_conv_w, ba_l, dn_prm, layer=l)
            o_f, o_b, s_fin = _dnscan_call(ops, s0_ctx if ctx else s0_lat[l], T=seq)
            if ctx:
                o_att = _attn_call(
                    q_hm, [(kT, v_bf, SEQ)], place, grid=(NB, 1),
                    q_map=lambda s, t: (0, 0, s, 0),
                    seg_maps=[(lambda s, t: (0, 0, 0, s), lambda s, t: (0, 0, s, 0))],
                    out_map=lambda s, t: (0, s, 0), out_rows=(1, TC), tq=SEQ, kb=512)
            else:
                tq = 128
                o_att = _attn_call(
                    q_hm, [(cache_kT, cache_v1, PAST), (kT, v_bf, DSEQ)], place, grid=(DB, DSEQ // tq),
                    q_map=lambda b, t: (b, 0, t, 0),
                    seg_maps=[(lambda b, t: (b, l, 0, 0, 0), lambda b, t: (b, l, 0, 0, 0)),
                              (lambda b, t: (b, 0, 0, 0), lambda b, t: (b, 0, 0, 0))],
                    out_map=lambda b, t: (b, t, 0), out_rows=(DB, DSEQ), tq=tq, kb=512)
            if ctx:
                h_f, h_b, h_fin = _lru_call(xy.reshape(nseq, seq, 512), lru_conv_w, lru_cb, wr_bd, lru_br, wi_bd,
                                            lru_bi, lru_lam, h0_ctx, layer=l, h0_layer=None)
            else:
                h_f, h_b, h_fin = _lru_call(xy, lru_conv_w, lru_cb, wr_bd, lru_br, wi_bd, lru_bi, lru_lam, state_lru,
                                            layer=l, h0_layer=l)
            x = _merge_call(x, mod, norm_w, o_f.reshape(shp + (256,)), o_b.reshape(shp + (256,)), onw, bd256, z, o_att,
                            h_f.reshape(shp + (256,)), h_b.reshape(shp + (256,)), xy, w_gate, w_pa, w_pb, w_pc, w_o,
                            layer=l, mod_off=mod_off)
            x = _ffn_call(x, mod, norm_w, ffn2_wgu, ffn2_wd, layer=l, sub=2, mod_off=mod_off)
            if ctx:
                xp = x
                kf, vf = outs[7:9]
                new_k.append(kf.reshape(NB, SEQ, ATT_KV_HEADS, HEAD_DIM))
                new_v.append(vf.reshape(NB, SEQ, ATT_KV_HEADS, HEAD_DIM))
                new_sd.append(_blockdiag_to_state(s_fin))
                new_sl.append(h_fin)
            else:
                xs = x
    return (xp.reshape(NB, SEQ, D), xs, jnp.stack(new_k, axis=1), jnp.stack(new_v, axis=1),
            jnp.stack(new_sd, axis=1), jnp.stack(new_sl, axis=1))
```

```python
import functools
import math

import numpy as np
import jax
import jax.numpy as jnp
from jax import lax
from jax.experimental import pallas as pl
from jax.experimental.pallas import tpu as pltpu

F32 = jnp.float32
BF16 = jnp.bfloat16

EPS = 1e-6
GRID_W = 64
HEAD_DIM = 64
ATT_Q_HEADS = 8
ATT_KV_HEADS = 2
ROPE_BASE = 10000.0
ROPE_PAIRS = HEAD_DIM // 4
DN_HEADS = 4
DN_DK = 64
DN_CHUNK = 64
LRU_WIDTH = 256
LRU_BLOCKS = 4
LRU_C = 8.0
N_MOD = 9
LANES = 128
N_Q = ATT_Q_HEADS * HEAD_DIM
N_KV = ATT_KV_HEADS * HEAD_DIM
N_ATT = N_Q + 2 * N_KV
N_DN = DN_HEADS * DN_DK
OFF_BA = 4 * N_DN
OFF_ATT = OFF_BA + 4 * DN_HEADS
OFF_LRU = OFF_ATT + N_ATT
OFF_GATE = OFF_LRU + 2 * LRU_WIDTH
V7X_VMEM_LIMIT_BYTES = 56 * 1024 * 1024


def _params(*sem):
    return pltpu.CompilerParams(dimension_semantics=sem, vmem_limit_bytes=V7X_VMEM_LIMIT_BYTES)


def _mm(a, b):
    return jnp.dot(a.astype(BF16), b.astype(BF16), preferred_element_type=F32)


def _mm_nt(a, b):
    return lax.dot_general(a.astype(BF16), b.astype(BF16), (((1,), (1,)), ((), ())), preferred_element_type=F32)


def _mm_tn(a, b):
    return lax.dot_general(a.astype(BF16), b.astype(BF16), (((0,), (0,)), ((), ())), preferred_element_type=F32)


def _split3(x):
    hi = x.astype(BF16)
    r = x - hi.astype(F32)
    mid = r.astype(BF16)
    lo = (r - mid.astype(F32)).astype(BF16)
    return hi, mid, lo


def _mm01(x, m01):
    hi, mid, lo = _split3(x)
    d = functools.partial(jnp.dot, preferred_element_type=F32)
    return d(hi, m01) + d(mid, m01) + d(lo, m01)


def _mm01x2(x, m01):
    hi = x.astype(BF16)
    lo = (x - hi.astype(F32)).astype(BF16)
    return jnp.dot(hi, m01, preferred_element_type=F32) + jnp.dot(lo, m01, preferred_element_type=F32)


def _mm01_nt(m01, x):
    hi, mid, lo = _split3(x)
    d = functools.partial(lax.dot_general, dimension_numbers=(((1,), (1,)), ((), ())), preferred_element_type=F32)
    return d(m01, hi) + d(m01, mid) + d(m01, lo)


def _mm3_many(As, Bs):
    n = As[0].shape[0]
    ah = [a.astype(BF16) for a in As]
    bh = [b.astype(BF16) for b in Bs]
    al = [(a - h.astype(F32)).astype(BF16) for a, h in zip(As, ah)]
    bl = [(b - h.astype(F32)).astype(BF16) for b, h in zip(Bs, bh)]
    top = [jnp.dot(jnp.concatenate([h, l], axis=0), b, preferred_element_type=F32) for h, l, b in zip(ah, al, bh)]
    low = [jnp.dot(h, b, preferred_element_type=F32) for h, b in zip(ah, bl)]
    return [(t[0:n] + t[n:2 * n]) + w for t, w in zip(top, low)]


def _silu(x):
    return x * jax.nn.sigmoid(x)


def _softplus(x):
    return jnp.maximum(x, 0.0) + jnp.log1p(jnp.exp(-jnp.abs(x)))


def _gelu_tanh(x):
    return x * (0.5 * (1.0 + jnp.tanh(0.7978845608028654 * (x + 0.044715 * (x * x * x)))))


def _norm_mod(x, nw, scale, shift):
    ms = jnp.mean(x * x, axis=-1, keepdims=True)
    y = (x * lax.rsqrt(ms + EPS)) * nw
    return y * (1.0 + scale) + shift


def _seg_masks(n, seg):
    r = lax.broadcasted_iota(jnp.int32, (n, n), 0)
    c = lax.broadcasted_iota(jnp.int32, (n, n), 1)
    return r, c, (r // seg) == (c // seg)


def _swap16(x):
    w = x.shape[1]
    lane = lax.broadcasted_iota(jnp.int32, x.shape, 1)
    return jnp.where((lane & 16) == 0, pltpu.roll(x, w - 16, axis=1), pltpu.roll(x, 16, axis=1))


def _mod_kernel(c_ref, w_ref, b_ref, o_ref):
    o_ref[...] = _mm(_silu(c_ref[...]), w_ref[...]) + b_ref[...]


def _mod_call(cond, w_mod, b_mod):
    L, D, N = w_mod.shape
    R = cond.shape[0]
    tn = D
    return pl.pallas_call(
        _mod_kernel,
        grid=(L, N // tn),
        in_specs=[pl.BlockSpec((R, D), lambda l, j: (0, 0)),
                  pl.BlockSpec((None, D, tn), lambda l, j: (l, 0, j)),
                  pl.BlockSpec((None, 1, tn), lambda l, j: (l, 0, j))],
        out_specs=pl.BlockSpec((None, R, tn), lambda l, j: (l, 0, j)),
        out_shape=jax.ShapeDtypeStruct((L, R, N), F32),
        compiler_params=_params("parallel", "parallel"),
        name="mod",
    )(cond, w_mod, b_mod.reshape(L, 1, N))


def _ffn_kernel(x_ref, mod_ref, nw_ref, wgu_ref, wd_ref, o_ref, *, sub, F, nc):
    x = x_ref[...]
    h = _norm_mod(x, nw_ref[sub:sub + 1, :], mod_ref[3 * sub + 1:3 * sub + 2, :],
                  mod_ref[3 * sub:3 * sub + 1, :]).astype(BF16)
    dot = functools.partial(jnp.dot, preferred_element_type=F32)
    cf = F // nc

    def gate_up(c):
        return (dot(h, wgu_ref[:, c * cf:(c + 1) * cf]), dot(h, wgu_ref[:, F + c * cf:F + (c + 1) * cf]))

    nxt = gate_up(0)
    acc = None
    for c in range(nc):
        g, u = nxt
        if c + 1 < nc:
            nxt = gate_up(c + 1)
        part = dot((_silu(g) * u).astype(BF16), wd_ref[c * cf:(c + 1) * cf, :])
        acc = part if acc is None else acc + part
    o_ref[...] = x + (0.5 * mod_ref[3 * sub + 2:3 * sub + 3, :]) * acc


def _ffn_call(x, mod, norm_w, w_gu, w_d, *, layer, sub, mod_off, tm=1024):
    B, T, D = x.shape
    F = w_d.shape[1]
    tm = min(tm, T)
    nc = F // 256 if F % 256 == 0 else 1
    return pl.pallas_call(
        functools.partial(_ffn_kernel, sub=sub, F=F, nc=nc),
        grid=(B, T // tm),
        in_specs=[pl.BlockSpec((None, tm, D), lambda b, t: (b, t, 0)),
                  pl.BlockSpec((None, None, N_MOD, D), lambda b, t: (layer, b + mod_off, 0, 0)),
                  pl.BlockSpec((None, 3, D), lambda b, t: (layer, 0, 0)),
                  pl.BlockSpec((None, D, 2 * F), lambda b, t: (layer, 0, 0), pipeline_mode=pl.Buffered(1)),
                  pl.BlockSpec((None, F, D), lambda b, t: (layer, 0, 0), pipeline_mode=pl.Buffered(1))],
        out_specs=pl.BlockSpec((None, tm, D), lambda b, t: (b, t, 0)),
        out_shape=jax.ShapeDtypeStruct((B, T, D), F32),
        compiler_params=_params("parallel", "parallel"),
        name="ffn",
    )(x, mod, norm_w, w_gu, w_d)


def _inproj_kernel(*refs, rope, emit_kv):
    (x_ref, mod_ref, nw_ref, w_ref, wba_ref, qw_ref, kw_ref, bd_ref) = refs[:8]
    refs = refs[8:]
    if rope:
        rc_ref, rs_ref = refs[:2]
        refs = refs[2:]
    qkv_ref, z_ref, xy_ref, q_ref, kT_ref, v_ref, ba_ref = refs[:7]
    refs = refs[7:]

    h = _norm_mod(x_ref[...], nw_ref[1:2, :], mod_ref[4:5, :], mod_ref[3:4, :]).astype(BF16)
    ya = jnp.dot(h, w_ref[:, 0:N_ATT], preferred_element_type=F32)
    aq = ya[:, 0:N_Q]
    ak = ya[:, N_Q:N_Q + N_KV]
    v = ya[:, N_Q + N_KV:N_ATT]
    bd = bd_ref[...]
    inv_hd = 1.0 / HEAD_DIM
    ssq = _mm01x2(aq * aq, bd)
    ssk = _mm01x2(ak * ak, bd[0:N_KV, 0:N_KV])
    ba_ref[...] = lax.dot_general(wba_ref[...], h, (((1,), (1,)), ((), ())), preferred_element_type=F32)
    y = jnp.dot(h, w_ref[:, N_ATT:], preferred_element_type=F32)
    qkv_ref[...] = y[:, 0:3 * N_DN]
    z_ref[...] = y[:, 3 * N_DN:4 * N_DN]
    xy_ref[...] = y[:, 4 * N_DN:4 * N_DN + 2 * LRU_WIDTH]
    qn = (aq * lax.rsqrt(ssq * inv_hd + EPS)) * qw_ref[...]
    kn = (ak * lax.rsqrt(ssk * inv_hd + EPS)) * kw_ref[...]
    if emit_kv:
        kf_ref, vf_ref = refs
        kf_ref[...] = kn
        vf_ref[...] = v
    if rope:
        rc = rc_ref[...]
        rs = rs_ref[...]
        kn = kn * rc + _swap16(kn) * rs
        qn = qn * jnp.concatenate([rc] * 4, axis=1) + _swap16(qn) * jnp.concatenate([rs] * 4, axis=1)
    qs = qn * (HEAD_DIM ** -0.5 * math.log2(math.e))
    for hh in range(ATT_Q_HEADS):
        q_ref[hh] = qs[:, hh * HEAD_DIM:(hh + 1) * HEAD_DIM].astype(BF16)
    kT = kn.T
    kT_ref[0] = kT[0:HEAD_DIM, :].astype(BF16)
    kT_ref[1] = kT[HEAD_DIM:2 * HEAD_DIM, :].astype(BF16)
    lane = lax.broadcasted_iota(jnp.int32, v.shape, 1)
    tail = jnp.where(lane == HEAD_DIM, 1.0, 0.0)
    v_ref[0] = jnp.where(lane < HEAD_DIM, v, tail).astype(BF16)
    v_ref[1] = jnp.where(lane < HEAD_DIM, pltpu.roll(v, HEAD_DIM, axis=1), tail).astype(BF16)


def _inproj_call(x, mod, norm_w, w_main, w_ba, qw, kw, bd512, rope_tabs, *, layer, mod_off, emit_kv, tm=1024):
    B, T, D = x.shape
    tm = min(tm, T)
    NW = w_main.shape[2]
    rope = rope_tabs is not None
    im = lambda b, t: (b, t, 0)
    in_specs = [pl.BlockSpec((None, tm, D), im),
                pl.BlockSpec((None, None, N_MOD, D), lambda b, t: (layer, b + mod_off, 0, 0)),
                pl.BlockSpec((None, 3, D), lambda b, t: (layer, 0, 0)),
                pl.BlockSpec((None, D, NW), lambda b, t: (layer, 0, 0)),
                pl.BlockSpec((None, 16, D), lambda b, t: (layer, 0, 0)),
                pl.BlockSpec((None, 1, 512), lambda b, t: (layer, 0, 0)),
                pl.BlockSpec((None, 1, 128), lambda b, t: (layer, 0, 0)),
                pl.BlockSpec((512, 512), lambda b, t: (0, 0))]
    args = [x, mod, norm_w, w_main, w_ba, qw, kw, bd512]
    if rope:
        in_specs += [pl.BlockSpec((tm, 128), lambda b, t: (t, 0))] * 2
        args += list(rope_tabs)
    out_shape = [jax.ShapeDtypeStruct((B, T, 768), F32), jax.ShapeDtypeStruct((B, T, 256), F32),
                 jax.ShapeDtypeStruct((B, T, 512), F32), jax.ShapeDtypeStruct((B, ATT_Q_HEADS, T, HEAD_DIM), BF16),
                 jax.ShapeDtypeStruct((B, ATT_KV_HEADS, HEAD_DIM, T), BF16),
                 jax.ShapeDtypeStruct((B, ATT_KV_HEADS, T, 128), BF16),
                 jax.ShapeDtypeStruct((B, 16, T), F32)]
    out_specs = [pl.BlockSpec((None, tm, 768), im), pl.BlockSpec((None, tm, 256), im), pl.BlockSpec((None, tm, 512), im),
                 pl.BlockSpec((None, ATT_Q_HEADS, tm, HEAD_DIM), lambda b, t: (b, 0, t, 0)),
                 pl.BlockSpec((None, ATT_KV_HEADS, HEAD_DIM, tm), lambda b, t: (b, 0, 0, t)),
                 pl.BlockSpec((None, ATT_KV_HEADS, tm, 128), lambda b, t: (b, 0, t, 0)),
                 pl.BlockSpec((None, 16, tm), lambda b, t: (b, 0, t))]
    if emit_kv:
        out_shape += [jax.ShapeDtypeStruct((B, T, 128), F32)] * 2
        out_specs += [pl.BlockSpec((None, tm, 128), im)] * 2
    return pl.pallas_call(
        functools.partial(_inproj_kernel, rope=rope, emit_kv=emit_kv),
        grid=(B, T // tm),
        in_specs=in_specs, out_specs=out_specs, out_shape=out_shape,
        compiler_params=_params("parallel", "parallel"),
        name="inproj",
    )(*args)


def _conv_window(main, prev, nxt, at_start, at_end):
    prev = jnp.where(at_start, 0.0, prev)
    nxt = jnp.where(at_end, 0.0, nxt)
    return jnp.concatenate([prev, main, nxt], axis=0)


def _conv4(win, w, n):
    return (w[0:1, :] * win[6:6 + n] + w[1:2, :] * win[7:7 + n]) + (w[2:3, :] * win[8:8 + n] + w[3:4, :] * win[9:9 + n])


def _dnchunk_kernel(q_ref, qp_ref, qn_ref, k_ref, kp_ref, kn_ref, v_ref, vp_ref, vn_ref, cwq_ref, cwk_ref, cwv_ref,
                    ba_ref, prm_ref, p_ref, qm_ref, o1_ref, o2_ref, gs_ref, *, G, groups):
    C = DN_CHUNK
    W = 2 * C
    r, c, same = _seg_masks(W, C)
    i_loc = r % C
    j_loc = c % C
    m01 = lambda mask: jnp.where(mask, 1.0, 0.0).astype(BF16)
    cum_f = m01(same & (i_loc <= j_loc))
    cum_b = m01(same & (i_loc >= j_loc))
    ones_bd = m01(same)
    eye = m01(r == c)
    eye_f = jnp.where(r == c, 1.0, 0.0)
    blk = tuple((r // s) == (c // s) for s in (8, 16, 32, 64))
    incl = (same & (i_loc >= j_loc), same & (i_loc <= j_loc))
    strict = (same & (i_loc > j_loc), same & (i_loc < j_loc))
    head0 = lax.broadcasted_iota(jnp.int32, (C, W), 1) < C
    row2 = lax.broadcasted_iota(jnp.int32, (2, W), 0)
    alog = prm_ref[0:2, :]
    dtb = prm_ref[2:4, :]

    jblk = pl.program_id(2)
    at_start = jblk == 0
    at_end = jblk == pl.num_programs(2) - 1

    def window(main_ref, prev_ref, next_ref):
        return _conv_window(main_ref[...], prev_ref[...], next_ref[...], at_start, at_end)

    def l2norm(x):
        return x * lax.rsqrt(_mm01(x * x, ones_bd) + EPS)

    def stack(x):
        return jnp.concatenate([jnp.where(head0, x, 0.0), jnp.where(head0, 0.0, x)], axis=0)

    win_q = window(q_ref, qp_ref, qn_ref)
    win_k = window(k_ref, kp_ref, kn_ref)
    win_v = window(v_ref, vp_ref, vn_ref)
    dot = functools.partial(jnp.dot, preferred_element_type=F32)
    col = lambda cols, gi, k: cols[gi][:, k:k + 1]

    gsz = G // groups
    for grp in range(groups):
        chunks = list(range(grp * gsz, (grp + 1) * gsz))
        r0 = grp * gsz * C
        n = gsz * C

        def conv_silu(win, cw_ref):
            return _silu(_conv4(win[r0:r0 + n + 16], cw_ref[...], n))

        q_all = l2norm(conv_silu(win_q, cwq_ref)) * (DN_DK ** -0.5)
        k_all = l2norm(conv_silu(win_k, cwk_ref))
        v_all = conv_silu(win_v, cwv_ref)
        rows_of = lambda x, gi: x[(gi - grp * gsz) * C:(gi - grp * gsz + 1) * C, :]

        bg = {gi: ba_ref[gi] for gi in chunks}
        beta = {gi: jax.nn.sigmoid(bg[gi][0:2, :]) for gi in chunks}
        g = {gi: -jnp.exp(alog) * _softplus(bg[gi][2:4, :] + dtb) for gi in chunks}
        gcf = {gi: _mm01(g[gi], cum_f) for gi in chunks}
        gcb = {gi: _mm01(g[gi], cum_b) for gi in chunks}
        tot = {gi: _mm01(g[gi], ones_bd) for gi in chunks}
        rows = {gi: jnp.concatenate([beta[gi], jnp.where(row2 == 0, gcf[gi], gcb[gi]), tot[gi],
                                     jnp.zeros((2, W), F32)], axis=0) for gi in chunks}
        cols = {gi: _mm01_nt(eye, rows[gi]) for gi in chunks}
        Kst = {gi: stack(rows_of(k_all, gi)) for gi in chunks}
        Qst = {gi: stack(rows_of(q_all, gi)) for gi in chunks}
        Vst = {gi: stack(rows_of(v_all, gi)) for gi in chunks}
        kq = {gi: _mm_nt(jnp.concatenate([Kst[gi], Qst[gi]], axis=0), Kst[gi]) for gi in chunks}

        chains = [(gi, d) for gi in chunks for d in range(2)]
        dec = [jnp.exp(jnp.where(incl[d], col(cols, gi, 2 + d) - rows[gi][2 + d:3 + d, :], -jnp.inf))
               for gi, d in chains]
        L = [(col(cols, gi, d) * kq[gi][0:W]) * jnp.where(strict[d], dec[i], 0.0) for i, (gi, d) in enumerate(chains)]
        QK = [kq[gi][W:2 * W] * dec[i] for i, (gi, d) in enumerate(chains)]
        D8f = [jnp.where(blk[0], x, 0.0) for x in L]
        D8 = [x.astype(BF16) for x in D8f]
        M = [dot(d8, d8) for d8 in D8]
        Mb = [m.astype(BF16) for m in M]
        DM = [dot(d8, mb) for d8, mb in zip(D8, Mb)]
        R = [(m - d8) - dm for m, d8, dm in zip(M, D8f, DM)]
        M = [dot(mb, mb) for mb in Mb]
        RM = [dot(r_.astype(BF16), m.astype(BF16)) for r_, m in zip(R, M)]
        Tm = [eye_f + ((r_ + m) + rm) for r_, m, rm in zip(R, M, RM)]
        for lvl in range(3):
            off = blk[lvl + 1] & jnp.logical_not(blk[lvl])
            Th = [t.astype(BF16) for t in Tm]
            Bm = [jnp.where(off, x, 0.0) for x in L]
            Bh = [b.astype(BF16) for b in Bm]
            Bl = [(b - h.astype(F32)).astype(BF16) for b, h in zip(Bm, Bh)]
            TB2 = [dot(th, jnp.concatenate([bh, bl], axis=1)) for th, bh, bl in zip(Th, Bh, Bl)]
            TB = [x[:, 0:W] + x[:, W:2 * W] for x in TB2]
            TBh = [x.astype(BF16) for x in TB]
            TBl = [(x - h.astype(F32)).astype(BF16) for x, h in zip(TB, TBh)]
            TBT2 = [dot(jnp.concatenate([h, lo], axis=0), th) for h, lo, th in zip(TBh, TBl, Th)]
            Tm = [t - (x[0:W] + x[W:2 * W]) for t, x in zip(Tm, TBT2)]
        rhs = [jnp.concatenate([(col(cols, gi, d) * jnp.exp(col(cols, gi, 2 + d))) * Kst[gi],
                                col(cols, gi, d) * Vst[gi]], axis=1) for gi, d in chains]
        X = _mm3_many(Tm, rhs)
        kd = [Kst[gi] * jnp.exp(col(cols, gi, 4 + d) - col(cols, gi, 2 + d)) for gi, d in chains]
        PQ = [_mm_tn(a, x) for a, x in zip(kd, X)]
        OO = [_mm(a, x) for a, x in zip(QK, X)]
        for i, (gi, d) in enumerate(chains):
            o1 = Qst[gi] * jnp.exp(col(cols, gi, 2 + d)) - OO[i][:, 0:W]
            o2 = OO[i][:, W:2 * W]
            p_ref[d, gi] = PQ[i][:, 0:W].astype(BF16)
            qm_ref[d, gi] = PQ[i][:, W:2 * W]
            o1_ref[d, gi] = (o1[0:C] + o1[C:W]).astype(BF16)
            o2_ref[d, gi] = o2[0:C] + o2[C:W]
            gs_ref[d, gi] = jnp.exp(rows[gi][4 + d:5 + d, :])


def _dnchunk_call(qkv, conv_w, ba, prm, *, layer, G=16):
    B, T, _ = qkv.shape
    C = DN_CHUNK
    W = 2 * C
    n = T // C
    G = min(G, n)
    rb = G * C // 8
    nb8 = T // 8
    sds = jax.ShapeDtypeStruct
    mat = lambda rows, dt: (sds((B, 2, 2, n, rows, W), dt),
                            pl.BlockSpec((None, None, 2, G, rows, W), lambda b, p, j: (b, p, 0, j, 0, 0)))
    outs = [mat(W, BF16), mat(W, F32), mat(C, BF16), mat(C, F32), mat(1, F32)]
    in_specs = []
    for part in range(3):
        in_specs += [pl.BlockSpec((None, G * C, W), lambda b, p, j, part=part: (b, j, 2 * part + p)),
                     pl.BlockSpec((None, 8, W), lambda b, p, j, part=part: (b, jnp.maximum(j * rb - 1, 0), 2 * part + p)),
                     pl.BlockSpec((None, 8, W),
                                  lambda b, p, j, part=part: (b, jnp.minimum((j + 1) * rb, nb8 - 1), 2 * part + p))]
    in_specs += [pl.BlockSpec((None, 4, W), lambda b, p, j, part=part: (layer, 0, 2 * part + p)) for part in range(3)]
    in_specs += [pl.BlockSpec((None, None, G, 4, W), lambda b, p, j: (b, p, j, 0, 0)),
                 pl.BlockSpec((None, None, 4, W), lambda b, p, j: (layer, p, 0, 0))]
    return pl.pallas_call(
        functools.partial(_dnchunk_kernel, G=G, groups=1),
        grid=(B, 2, n // G),
        in_specs=in_specs,
        out_specs=[o[1] for o in outs],
        out_shape=[o[0] for o in outs],
        compiler_params=_params("parallel", "parallel", "parallel"),
        name="dnchunk",
    )(*([qkv] * 9), conv_w, conv_w, conv_w, ba, prm)


def _dnscan_kernel(pf_ref, qf_ref, o1f_ref, o2f_ref, gf_ref, pb_ref, qb_ref, o1b_ref, o2b_ref, gb_ref, s0_ref,
                   of_ref, ob_ref, sfin_ref, s_sc, *, Gs):
    C = DN_CHUNK
    W = 2 * C
    j = pl.program_id(1)

    @pl.when(j == 0)
    def _():
        s_sc[...] = s0_ref[...]

    S = [[s_sc[p, d] for d in range(2)] for p in range(2)]
    fwd = (pf_ref, qf_ref, o1f_ref, o2f_ref, gf_ref, of_ref)
    bwd = (pb_ref, qb_ref, o1b_ref, o2b_ref, gb_ref, ob_ref)
    for i in range(Gs):
        for d, (P, Qm, O1, O2, GS, out) in enumerate((fwd, bwd)):
            ci = i if d == 0 else Gs - 1 - i
            for p in range(2):
                Sb = S[p][d].astype(BF16)
                out[ci * C:(ci + 1) * C, p * W:(p + 1) * W] = (
                    jnp.dot(O1[p, ci], Sb, preferred_element_type=F32) + O2[p, ci])
                S[p][d] = (GS[p, ci] * S[p][d] - jnp.dot(P[p, ci], Sb, preferred_element_type=F32)) + Qm[p, ci]
    for p in range(2):
        for d in range(2):
            s_sc[p, d] = S[p][d]

    @pl.when(j == pl.num_programs(1) - 1)
    def _():
        sfin_ref[...] = s_sc[...]


def _dnscan_call(ops, s0, *, T, Gs=16):
    B = s0.shape[0]
    C = DN_CHUNK
    W = 2 * C
    n = T // C
    Gs = min(Gs, n)
    nb = n // Gs
    specs = []
    for d in range(2):
        for a in ops:
            rows = a.shape[4]
            if d == 0:
                specs.append(pl.BlockSpec((None, 2, None, Gs, rows, W), lambda b, j: (b, 0, 0, j, 0, 0)))
            else:
                specs.append(pl.BlockSpec((None, 2, None, Gs, rows, W), lambda b, j: (b, 0, 1, nb - 1 - j, 0, 0)))
    st_spec = pl.BlockSpec((None, 2, 2, W, W), lambda b, j: (b, 0, 0, 0, 0))
    return pl.pallas_call(
        functools.partial(_dnscan_kernel, Gs=Gs),
        grid=(B, nb),
        in_specs=specs + [st_spec],
        out_specs=[pl.BlockSpec((None, Gs * C, 2 * W), lambda b, j: (b, j, 0)),
                   pl.BlockSpec((None, Gs * C, 2 * W), lambda b, j: (b, nb - 1 - j, 0)),
                   st_spec],
        out_shape=[jax.ShapeDtypeStruct((B, T, 2 * W), F32), jax.ShapeDtypeStruct((B, T, 2 * W), F32),
                   jax.ShapeDtypeStruct((B, 2, 2, W, W), F32)],
        scratch_shapes=[pltpu.VMEM((2, 2, W, W), F32)],
        compiler_params=_params("parallel", "arbitrary"),
        name="dnscan",
    )(*ops, *ops, s0)


def _attn_kernel(*refs, seg_blocks, tq, rt):
    q_ref = refs[0]
    nseg = len(seg_blocks)
    e_ref, o_ref = refs[1 + 2 * nseg:3 + 2 * nseg]
    G = ATT_Q_HEADS // ATT_KV_HEADS
    rows = G * tq
    rt = min(rt, rows)
    for g in range(ATT_KV_HEADS):
        q = q_ref[g * G:(g + 1) * G].reshape(rows, HEAD_DIM)
        blocks = [(si, j, kb) for si, (nblk, kb) in enumerate(seg_blocks) for j in range(nblk)]

        def scores(b):
            si, j, kb = blocks[b]
            return jnp.dot(q, refs[1 + 2 * si][g, :, j * kb:(j + 1) * kb], preferred_element_type=F32)

        m = None
        acc = None
        s_next = scores(0)
        for b, (si, j, kb) in enumerate(blocks):
            s_all = s_next
            if b + 1 < len(blocks):
                s_next = scores(b + 1)
            v = refs[2 + 2 * si][g, j * kb:(j + 1) * kb, :]
            S = [s_all[r * rt:(r + 1) * rt] for r in range(rows // rt)]
            smax = [jnp.max(s, axis=-1, keepdims=True) for s in S]
            if m is None:
                m_new = smax
            else:
                m_new = [jnp.maximum(a, b_) for a, b_ in zip(m, smax)]
                alpha = [jnp.exp2(a - b_) for a, b_ in zip(m, m_new)]
            P = [jnp.exp2((s - mn).astype(BF16)) for s, mn in zip(S, m_new)]
            pv = jnp.dot(jnp.concatenate(P, axis=0), v, preferred_element_type=F32)
            acc = pv if acc is None else jnp.concatenate(alpha, axis=0) * acc + pv
            m = m_new
        o = (acc / acc[:, HEAD_DIM:HEAD_DIM + 1]).astype(BF16)
        out = jnp.dot(o[0:tq], e_ref[0], preferred_element_type=F32)
        for hh in range(1, G):
            out = out + jnp.dot(o[hh * tq:(hh + 1) * tq], e_ref[hh], preferred_element_type=F32)
        o_ref[:, g * G * HEAD_DIM:(g + 1) * G * HEAD_DIM] = out.astype(BF16)


def _attn_call(q, segs, place, *, grid, q_map, seg_maps, out_map, out_rows, tq, kb, rt=512):
    G = ATT_Q_HEADS // ATT_KV_HEADS
    in_specs = [pl.BlockSpec((None, ATT_Q_HEADS, tq, HEAD_DIM), q_map)]
    args = [q]
    seg_blocks = []
    for (kT, v, S), (k_map, v_map) in zip(segs, seg_maps):
        blk = min(kb, S)
        seg_blocks.append((S // blk, blk))
        in_specs.append(pl.BlockSpec((None,) * (kT.ndim - 3) + (ATT_KV_HEADS, HEAD_DIM, S), k_map))
        in_specs.append(pl.BlockSpec((None,) * (v.ndim - 3) + (ATT_KV_HEADS, S, 2 * HEAD_DIM), v_map))
        args += [kT, v]
    in_specs.append(pl.BlockSpec((G, 2 * HEAD_DIM, G * HEAD_DIM), lambda b, t: (0, 0, 0)))
    args.append(place)
    return pl.pallas_call(
        functools.partial(_attn_kernel, seg_blocks=tuple(seg_blocks), tq=tq, rt=rt),
        grid=grid,
        in_specs=in_specs,
        out_specs=pl.BlockSpec((None, tq, ATT_Q_HEADS * HEAD_DIM), out_map),
        out_shape=jax.ShapeDtypeStruct(out_rows + (ATT_Q_HEADS * HEAD_DIM,), BF16),
        compiler_params=_params("parallel", "parallel"),
        name="attn",
    )(*args)


def _lru_kernel(xf_ref, xfp_ref, xfn_ref, xb_ref, xbp_ref, xbn_ref, cw_ref, cb_ref, wr_ref, br_ref, wi_ref, bi_ref,
                lam_ref, h0_ref, hf_ref, hb_ref, hfin_ref, h_sc, a_sc, u_sc, *, SB, tt):
    W = LRU_WIDTH
    j = pl.program_id(1)
    last = pl.num_programs(1) - 1
    cw = cw_ref[...]
    cb = cb_ref[...]
    rid = lax.broadcasted_iota(jnp.int32, (8, W), 0)

    @pl.when(j == 0)
    def _():
        for s in range(SB):
            for d in range(2):
                h_sc[2 * s + d] = jnp.broadcast_to(h0_ref[s, d:d + 1, :], (8, W))

    def gates(x_ref, p_ref, n_ref, at_start, at_end, d):
        x = jnp.concatenate(
            [_conv4(_conv_window(x_ref[s, :, 0:W], p_ref[s, :, 0:W], n_ref[s, :, 0:W], at_start, at_end), cw, tt)
             for s in range(SB)], axis=0) + cb
        rg = jax.nn.sigmoid(_mm(x, wr_ref[d]) + br_ref[d:d + 1, :])
        ig = jax.nn.sigmoid(_mm(x, wi_ref[d]) + bi_ref[d:d + 1, :])
        log_a = (-LRU_C * rg) * _softplus(-lam_ref[d:d + 1, :])
        a = jnp.exp(log_a)
        a_sc[d] = a
        u_sc[d] = jnp.sqrt(-jnp.tanh(log_a) * (a * a + 1.0)) * (ig * x)

    gates(xf_ref, xfp_ref, xfn_ref, j == 0, j == last, 0)
    gates(xb_ref, xbp_ref, xbn_ref, j == last, j == 0, 1)

    chains = [(s, d) for s in range(SB) for d in range(2)]
    nblk = tt // 8

    def scan8(j8, hs):
        blk = [j8 if d == 0 else nblk - 1 - j8 for _, d in chains]
        row0 = [pl.multiple_of(s * tt + b * 8, 8) for (s, _), b in zip(chains, blk)]
        a8 = [a_sc[d, pl.ds(r0, 8), :] for (_, d), r0 in zip(chains, row0)]
        u8 = [u_sc[d, pl.ds(r0, 8), :] for (_, d), r0 in zip(chains, row0)]
        hs = list(hs)
        out = [jnp.zeros((8, W), F32)] * len(chains)
        for step in range(8):
            for i, (_, d) in enumerate(chains):
                r = step if d == 0 else 7 - step
                hs[i] = (jnp.broadcast_to(a8[i][r:r + 1, :], (8, W)) * hs[i]
                         + jnp.broadcast_to(u8[i][r:r + 1, :], (8, W)))
                out[i] = jnp.where(rid == r, hs[i], out[i])
        for i, (s, d) in enumerate(chains):
            dst = hf_ref if d == 0 else hb_ref
            dst[s, pl.ds(pl.multiple_of(blk[i] * 8, 8), 8), :] = out[i]
        return tuple(hs)

    hs = lax.fori_loop(0, nblk, scan8, tuple(h_sc[i] for i in range(len(chains))))
    for i in range(len(chains)):
        h_sc[i] = hs[i]

    @pl.when(j == last)
    def _():
        for i, (s, d) in enumerate(chains):
            hfin_ref[s, d:d + 1, :] = hs[i][0:1, :]


def _lru_call(xy, conv_w, conv_b, wr, br, wi, bi, lam, h0, *, layer, h0_layer, SB=4):
    B, T, _ = xy.shape
    W = LRU_WIDTH
    tt = min(256, T)
    nt = T // tt
    SB = min(SB, B)
    rb = tt // 8
    nb8 = T // 8
    lmap = lambda b, j: (layer, 0, 0)
    h0_spec = (pl.BlockSpec((SB, 2, W), lambda b, j: (b, 0, 0)) if h0_layer is None
               else pl.BlockSpec((SB, None, 2, W), lambda b, j: (b, h0_layer, 0, 0)))
    fwd = lambda b, j: j
    bwd = lambda b, j: nt - 1 - j
    tiles = []
    for tile in (fwd, bwd):
        tiles += [pl.BlockSpec((SB, tt, 2 * W), lambda b, j, tile=tile: (b, tile(b, j), 0)),
                  pl.BlockSpec((SB, 8, 2 * W), lambda b, j, tile=tile: (b, jnp.maximum(tile(b, j) * rb - 1, 0), 0)),
                  pl.BlockSpec((SB, 8, 2 * W),
                               lambda b, j, tile=tile: (b, jnp.minimum((tile(b, j) + 1) * rb, nb8 - 1), 0))]
    return pl.pallas_call(
        functools.partial(_lru_kernel, SB=SB, tt=tt),
        grid=(B // SB, nt),
        in_specs=tiles + [pl.BlockSpec((None, 4, W), lmap),
                          pl.BlockSpec((None, 1, W), lmap),
                          pl.BlockSpec((None, 2, W, W), lambda b, j: (layer, 0, 0, 0)),
                          pl.BlockSpec((None, 2, W), lmap),
                          pl.BlockSpec((None, 2, W, W), lambda b, j: (layer, 0, 0, 0)),
                          pl.BlockSpec((None, 2, W), lmap),
                          pl.BlockSpec((None, 2, W), lmap),
                          h0_spec],
        out_specs=[pl.BlockSpec((SB, tt, W), lambda b, j: (b, j, 0)),
                   pl.BlockSpec((SB, tt, W), lambda b, j: (b, nt - 1 - j, 0)),
                   pl.BlockSpec((SB, 2, W), lambda b, j: (b, 0, 0))],
        out_shape=[jax.ShapeDtypeStruct((B, T, W), F32), jax.ShapeDtypeStruct((B, T, W), F32),
                   jax.ShapeDtypeStruct((B, 2, W), F32)],
        scratch_shapes=[pltpu.VMEM((2 * SB, 8, W), F32), pltpu.VMEM((2, SB * tt, W), F32),
                        pltpu.VMEM((2, SB * tt, W), F32)],
        compiler_params=_params("parallel", "arbitrary"),
        name="lru",
    )(*([xy] * 6), conv_w, conv_b, wr, br, wi, bi, lam, h0)


def _merge_kernel(x_ref, mod_ref, nw_ref, of_ref, ob_ref, onw_ref, bd_ref, z_ref, oatt_ref, hf_ref, hb_ref, xy_ref,
                  wg_ref, wpa_ref, wpb_ref, wpc_ref, wo_ref, o_ref):
    D = x_ref.shape[-1]
    x = x_ref[...]
    h = _norm_mod(x, nw_ref[1:2, :], mod_ref[4:5, :], mod_ref[3:4, :]).astype(BF16)
    gates = jax.nn.sigmoid(jnp.dot(h, wg_ref[...], preferred_element_type=F32))
    odn = of_ref[...] + ob_ref[...]
    ms = _mm01(odn * odn, bd_ref[...]) * (1.0 / DN_DK)
    odn = (odn * lax.rsqrt(ms + EPS)) * onw_ref[...]
    a = _mm(odn * _silu(z_ref[...]), wpa_ref[...])
    b = jnp.dot(oatt_ref[...], wpb_ref[...], preferred_element_type=F32)
    c = _mm(_gelu_tanh(xy_ref[:, LRU_WIDTH:2 * LRU_WIDTH]) * (hf_ref[...] + hb_ref[...]), wpc_ref[...])
    merged = (gates[:, 0:D] * a + gates[:, D:2 * D] * b) + gates[:, 2 * D:3 * D] * c
    o_ref[...] = x + mod_ref[5:6, :] * _mm(merged, wo_ref[...])


def _merge_call(x, mod, norm_w, o_f, o_b, onw, bd256, z, o_att, h_f, h_b, xy, w_gate, w_pa, w_pb, w_pc, w_o, *,
                layer, mod_off, tm=512):
    B, T, D = x.shape
    tm = min(tm, T)
    im = lambda b, t: (b, t, 0)
    wspec = lambda w: pl.BlockSpec((None,) + w.shape[1:], lambda b, t: (layer, 0, 0), pipeline_mode=pl.Buffered(1))
    return pl.pallas_call(
        _merge_kernel,
        grid=(B, T // tm),
        in_specs=[pl.BlockSpec((None, tm, D), im),
                  pl.BlockSpec((None, None, N_MOD, D), lambda b, t: (layer, b + mod_off, 0, 0)),
                  pl.BlockSpec((None, 3, D), lambda b, t: (layer, 0, 0)),
                  pl.BlockSpec((None, tm, 256), im), pl.BlockSpec((None, tm, 256), im),
                  pl.BlockSpec((None, 1, 256), lambda b, t: (layer, 0, 0)),
                  pl.BlockSpec((256, 256), lambda b, t: (0, 0)),
                  pl.BlockSpec((None, tm, 256), im),
                  pl.BlockSpec((None, tm, 512), im),
                  pl.BlockSpec((None, tm, 256), im), pl.BlockSpec((None, tm, 256), im), pl.BlockSpec((None, tm, 512), im),
                  wspec(w_gate), wspec(w_pa), wspec(w_pb), wspec(w_pc), wspec(w_o)],
        out_specs=pl.BlockSpec((None, tm, D), im),
        out_shape=jax.ShapeDtypeStruct((B, T, D), F32),
        compiler_params=_params("parallel", "parallel"),
        name="merge",
    )(x, mod, norm_w, o_f, o_b, onw, bd256, z, o_att, h_f, h_b, xy, w_gate, w_pa, w_pb, w_pc, w_o)


def _blockdiag_ones(n, seg):
    i = np.arange(n)
    return jnp.asarray((i[:, None] // seg) == (i[None, :] // seg), BF16)


def _placement():
    G = ATT_Q_HEADS // ATT_KV_HEADS
    e = np.zeros((G, 2 * HEAD_DIM, G * HEAD_DIM), np.float32)
    d = np.arange(HEAD_DIM)
    for hh in range(G):
        e[hh, d, hh * HEAD_DIM + d] = 1.0
    return jnp.asarray(e, BF16)


def _with_ones_column(v):
    one = jnp.ones(v.shape[:-1] + (1,), v.dtype)
    zero = jnp.zeros(v.shape[:-1] + (HEAD_DIM - 1,), v.dtype)
    return jnp.concatenate([v, one, zero], axis=-1).astype(BF16)


def _rope_tables(n_tokens):
    rows = n_tokens // GRID_W
    row = jnp.broadcast_to(jnp.arange(rows, dtype=F32)[:, None], (rows, GRID_W)).reshape(-1)
    col = jnp.broadcast_to(jnp.arange(GRID_W, dtype=F32)[None, :], (rows, GRID_W)).reshape(-1)
    freqs = ROPE_BASE ** (-jnp.arange(ROPE_PAIRS, dtype=F32) / ROPE_PAIRS)
    ang = jnp.stack([row[:, None] * freqs, col[:, None] * freqs], axis=1)
    cos = jnp.cos(ang)[:, :, None, :]
    sin = jnp.sin(ang)[:, :, None, :]
    c = jnp.broadcast_to(cos, (n_tokens, 2, 2, ROPE_PAIRS)).reshape(n_tokens, HEAD_DIM)
    s = jnp.concatenate([-sin, sin], axis=2).reshape(n_tokens, HEAD_DIM)
    return jnp.concatenate([c, c], axis=1), jnp.concatenate([s, s], axis=1)


def _ba_layout(ba, n):
    B = ba.shape[0]
    x = ba.reshape(B, 2, 2, 2, 2, n, DN_CHUNK)
    x = x.transpose(0, 3, 5, 1, 2, 4, 6)
    return x.reshape(B, 2, n, 4, 2 * DN_CHUNK)


def _state_to_blockdiag(s):
    B = s.shape[0]
    x = s.reshape(B, 2, 2, 2, DN_DK, DN_DK)
    z = jnp.zeros_like(x[:, :, :, 0])
    top = jnp.concatenate([x[:, :, :, 0], z], axis=-1)
    bot = jnp.concatenate([z, x[:, :, :, 1]], axis=-1)
    return jnp.concatenate([top, bot], axis=-2).transpose(0, 2, 1, 3, 4)


def _blockdiag_to_state(sb):
    B = sb.shape[0]
    x = sb.transpose(0, 2, 1, 3, 4)
    h0 = x[..., 0:DN_DK, 0:DN_DK]
    h1 = x[..., DN_DK:, DN_DK:]
    return jnp.stack([h0, h1], axis=3).reshape(B, 2, DN_HEADS, DN_DK, DN_DK)


def _lru_blockdiag(w):
    L = w.shape[0]
    bw = LRU_WIDTH // LRU_BLOCKS
    out = jnp.zeros((L, 2, LRU_WIDTH, LRU_WIDTH), w.dtype)
    for n in range(LRU_BLOCKS):
        out = out.at[:, :, n * bw:(n + 1) * bw, n * bw:(n + 1) * bw].set(w[:, :, n])
    return out


def kernel(x_prompt, x_sample, cache_k, cache_v, state_delta, state_lru, c, c_ctx, w_mod, b_mod, norm_w, ffn1_wgu,
           ffn1_wd, ffn2_wgu, ffn2_wd, w_in, dn_conv_w, dn_a_log, dn_dt_bias, dn_onorm_w, att_qnorm_w, att_knorm_w,
           lru_conv_w, lru_conv_b, lru_wr, lru_br, lru_wi, lru_bi, lru_lam, w_pa, w_pb, w_pc, w_o):
    NB, SEQ, D = x_prompt.shape
    DB, DSEQ, _ = x_sample.shape
    L = w_mod.shape[0]
    PAST = cache_k.shape[2]
    TC = NB * SEQ

    cond = jnp.zeros((16, D), F32).at[0].set(c_ctx).at[1:1 + DB].set(c)
    mod = _mod_call(cond, w_mod, b_mod).reshape(L, 16, N_MOD, D)

    bf = lambda w: w.astype(BF16)
    ffn1_wgu, ffn1_wd, ffn2_wgu, ffn2_wd = bf(ffn1_wgu), bf(ffn1_wd), bf(ffn2_wgu), bf(ffn2_wd)
    w_main = bf(jnp.concatenate([w_in[:, :, OFF_ATT:OFF_LRU], w_in[:, :, 0:OFF_BA], w_in[:, :, OFF_LRU:OFF_GATE]],
                                axis=-1))
    w_ba = bf(jnp.swapaxes(w_in[:, :, OFF_BA:OFF_ATT], 1, 2))
    w_gate = bf(w_in[:, :, OFF_GATE:])
    w_pa, w_pb, w_pc, w_o = bf(w_pa), bf(w_pb), bf(w_pc), bf(w_o)
    qw = jnp.tile(att_qnorm_w, (1, ATT_Q_HEADS)).reshape(L, 1, ATT_Q_HEADS * HEAD_DIM)
    kw = jnp.tile(att_knorm_w, (1, ATT_KV_HEADS)).reshape(L, 1, ATT_KV_HEADS * HEAD_DIM)
    onw = jnp.tile(dn_onorm_w, (1, DN_HEADS)).reshape(L, 1, DN_HEADS * DN_DK)
    pr = lambda p: jnp.repeat(p.reshape(L, 2, 2, 2), DN_CHUNK, axis=-1).reshape(L, 2, 2, 2 * DN_CHUNK).transpose(0, 2, 1, 3)
    dn_prm = jnp.concatenate([pr(dn_a_log), pr(dn_dt_bias)], axis=2)
    wr_bd, wi_bd = bf(_lru_blockdiag(lru_wr)), bf(_lru_blockdiag(lru_wi))
    lru_cb = lru_conv_b.reshape(L, 1, LRU_WIDTH)
    bd512 = _blockdiag_ones(512, HEAD_DIM)
    bd256 = _blockdiag_ones(256, DN_DK)
    place = _placement()
    rope_tabs = _rope_tables(DSEQ)
    cache_kT = bf(cache_k.transpose(0, 1, 3, 4, 2))
    cache_v1 = _with_ones_column(cache_v.transpose(0, 1, 3, 2, 4))
    s0_lat = _state_to_blockdiag(state_delta.transpose(1, 0, 2, 3, 4, 5).reshape(L * DB, 2, DN_HEADS, DN_DK, DN_DK))
    s0_lat = s0_lat.reshape(L, DB, 2, 2, 2 * DN_DK, 2 * DN_DK)
    s0_ctx = jnp.zeros((NB, 2, 2, 2 * DN_DK, 2 * DN_DK), F32)
    h0_ctx = jnp.zeros((NB, 2, LRU_WIDTH), F32)

    xp = x_prompt.reshape(1, TC, D)
    xs = x_sample
    new_k, new_v, new_sd, new_sl = [], [], [], []
    for l in range(L):
        for ctx in (True, False):
            x = xp if ctx else xs
            mod_off = 0 if ctx else 1
            seq = SEQ if ctx else DSEQ
            nseq = NB if ctx else DB
            shp = x.shape[:2]
            x = _ffn_call(x, mod, norm_w, ffn1_wgu, ffn1_wd, layer=l, sub=0, mod_off=mod_off)
            outs = _inproj_call(x, mod, norm_w, w_main, w_ba, qw, kw, bd512, None if ctx else rope_tabs,
                                layer=l, mod_off=mod_off, emit_kv=ctx)
            qkv, z, xy, q_hm, kT, v_bf, ba = outs[:7]
            n = seq // DN_CHUNK
            ba_l = _ba_layout(ba.reshape(shp[0], 16, -1, seq).transpose(0, 2, 1, 3).reshape(nseq, 16, seq), n)
            ops = _dnchunk_call(qkv.reshape(nseq, seq, 768), dn_conv_w, ba_l, dn_prm, layer=l)
            o_f, o_b, s_fin = _dnscan_call(ops, s0_ctx if ctx else s0_lat[l], T=seq)
            if ctx:
                o_att = _attn_call(
                    q_hm, [(kT, v_bf, SEQ)], place, grid=(NB, 1),
                    q_map=lambda s, t: (0, 0, s, 0),
                    seg_maps=[(lambda s, t: (0, 0, 0, s), lambda s, t: (0, 0, s, 0))],
                    out_map=lambda s, t: (0, s, 0), out_rows=(1, TC), tq=SEQ, kb=512)
            else:
                tq = 128
                o_att = _attn_call(
                    q_hm, [(cache_kT, cache_v1, PAST), (kT, v_bf, DSEQ)], place, grid=(DB, DSEQ // tq),
                    q_map=lambda b, t: (b, 0, t, 0),
                    seg_maps=[(lambda b, t: (b, l, 0, 0, 0), lambda b, t: (b, l, 0, 0, 0)),
                              (lambda b, t: (b, 0, 0, 0), lambda b, t: (b, 0, 0, 0))],
                    out_map=lambda b, t: (b, t, 0), out_rows=(DB, DSEQ), tq=tq, kb=512)
            if ctx:
                h_f, h_b, h_fin = _lru_call(xy.reshape(nseq, seq, 512), lru_conv_w, lru_cb, wr_bd, lru_br, wi_bd,
                                            lru_bi, lru_lam, h0_ctx, layer=l, h0_layer=None)
            else:
                h_f, h_b, h_fin = _lru_call(xy, lru_conv_w, lru_cb, wr_bd, lru_br, wi_bd, lru_bi, lru_lam, state_lru,
                                            layer=l, h0_layer=l)
            x = _merge_call(x, mod, norm_w, o_f.reshape(shp + (256,)), o_b.reshape(shp + (256,)), onw, bd256, z, o_att,
                            h_f.reshape(shp + (256,)), h_b.reshape(shp + (256,)), xy, w_gate, w_pa, w_pb, w_pc, w_o,
                            layer=l, mod_off=mod_off)
            x = _ffn_call(x, mod, norm_w, ffn2_wgu, ffn2_wd, layer=l, sub=2, mod_off=mod_off)
            if ctx:
                xp = x
                kf, vf = outs[7:9]
                new_k.append(kf.reshape(NB, SEQ, ATT_KV_HEADS, HEAD_DIM))
                new_v.append(vf.reshape(NB, SEQ, ATT_KV_HEADS, HEAD_DIM))
                new_sd.append(_blockdiag_to_state(s_fin))
                new_sl.append(h_fin)
            else:
                xs = x
    return (xp.reshape(NB, SEQ, D), xs, jnp.stack(new_k, axis=1), jnp.stack(new_v, axis=1),
            jnp.stack(new_sd, axis=1), jnp.stack(new_sl, axis=1))
```
